```python
import jax, jax.numpy as jnp
from jax import lax
import numpy as np

D_MODEL = 1024
BATCH = 16
SEQ = 4096
DEPTH = 2

N_MIXERS = 2
N_META = 16
HEAD_DIM = 64
N_HEADS = 16
N_KV_HEADS = 2
GROUP = N_HEADS // N_KV_HEADS
Q_COLS = N_HEADS * HEAD_DIM
KV_COLS = N_KV_HEADS * HEAD_DIM
ROT_DIM = HEAD_DIM // 4
ROPE_THETA = 500000.0
ATTN_SCALE = HEAD_DIM ** -0.5
WINDOW = 128
BLOCK = 128
IDX_HEADS = 8
IDX_DIM = 64
IDX_ROT_DIM = IDX_DIM // 2
IDX_SCALE = IDX_DIM ** -0.5
IDX_W_SCALE = IDX_HEADS ** -0.5
TOPK_MAX = 256
IN_COLS_A = Q_COLS + 2 * KV_COLS
IN_COLS_B = Q_COLS + 2 * KV_COLS + IDX_HEADS * IDX_DIM + IDX_DIM + IDX_HEADS
N_EXPERTS = 32
TOP_K = 4
D_FF = 1024
SWIGLU_LIMIT = 7.0
SWIGLU_ALPHA = 1.702
MOE_BLOCK = 512
ALPHA = (2 * DEPTH) ** 0.25
BETA = (8 * DEPTH) ** -0.25
LN_EPS = 1e-5
NEG = -1e30
N_LAYERS_A = (DEPTH + 1) // 2
N_LAYERS_B = DEPTH // 2

kernel_name = 'hybrid_swa_sink_dsa_moe_deepnorm'


def layer_norm(x, g, b):
    xf = x.astype(jnp.float32)
    mu = jnp.mean(xf, -1, keepdims=True)
    var = jnp.mean(jnp.square(xf - mu), -1, keepdims=True)
    return ((xf - mu) * lax.rsqrt(var + LN_EPS) * g + b).astype(x.dtype)


def rope(x, pos, rot_dim):
    half = rot_dim // 2
    inv_freq = ROPE_THETA ** (-jnp.arange(half, dtype=jnp.float32) / half)
    ang = pos.astype(jnp.float32)[:, None] * inv_freq[None, :]
    cos = jnp.cos(ang)[:, None, :]
    sin = jnp.sin(ang)[:, None, :]
    x1 = x[..., :half].astype(jnp.float32)
    x2 = x[..., half:rot_dim].astype(jnp.float32)
    rot = jnp.concatenate([x1 * cos - x2 * sin, x2 * cos + x1 * sin], -1).astype(x.dtype)
    return jnp.concatenate([rot, x[..., rot_dim:]], -1)


def to_blocks(a):
    B, S = a.shape[:2]
    return jnp.moveaxis(a.reshape((B, S // BLOCK, BLOCK) + a.shape[2:]), 1, 0)


def from_blocks(a):
    a = jnp.moveaxis(a, 0, 1)
    return a.reshape((a.shape[0], a.shape[1] * a.shape[2]) + a.shape[3:])


def meta_causal_mask():
    i = jnp.arange(N_META)
    return i[None, :] <= i[:, None]


def split_heads(q, k, v, L):
    B = q.shape[0]
    pos = jnp.arange(L)
    q = rope(q.reshape(B, L, N_HEADS, HEAD_DIM), pos, ROT_DIM)
    q = q.reshape(B, L, N_KV_HEADS, GROUP, HEAD_DIM)
    k = rope(k.reshape(B, L, N_KV_HEADS, HEAD_DIM), pos, ROT_DIM)
    v = v.reshape(B, L, N_KV_HEADS, HEAD_DIM)
    return q, k, v


def swa_sink_mixer(h, w_in, b_in, sinks, w_out):
    B, L, _ = h.shape
    proj = jnp.einsum('bld,de->ble', h, w_in) + b_in
    q, k, v = jnp.split(proj, [Q_COLS, Q_COLS + KV_COLS], axis=-1)
    q, k, v = split_heads(q, k, v, L)
    sink = sinks.reshape(N_KV_HEADS, GROUP).astype(jnp.float32)[None, :, :, None, None]

    def probs_with_sink(logits):
        s = jnp.broadcast_to(sink, logits.shape[:-1] + (1,))
        p = jax.nn.softmax(jnp.concatenate([logits, s], -1), axis=-1)
        return p[..., :-1]

    qm, km, vm = q[:, :N_META], k[:, :N_META], v[:, :N_META]
    s_mm = jnp.einsum('bikgd,bjkd->bkgij', qm, km, preferred_element_type=jnp.float32) * ATTN_SCALE
    s_mm = jnp.where(meta_causal_mask(), s_mm, NEG)
    out_m = jnp.einsum('bkgij,bjkd->bikgd', probs_with_sink(s_mm).astype(v.dtype), vm)

    qb = to_blocks(q[:, N_META:])
    kb = to_blocks(k[:, N_META:])
    vb = to_blocks(v[:, N_META:])
    k_band = jnp.concatenate([jnp.concatenate([jnp.zeros_like(kb[:1]), kb[:-1]], 0), kb], 2)
    v_band = jnp.concatenate([jnp.concatenate([jnp.zeros_like(vb[:1]), vb[:-1]], 0), vb], 2)
    nb = qb.shape[0]
    ii = jnp.arange(BLOCK)[:, None]
    jj = jnp.arange(2 * BLOCK)[None, :]
    rel = BLOCK + ii - jj
    band_ok = (rel >= 0) & (rel < WINDOW)

    def block_fn(args):
        q_blk, k_blk, v_blk, n = args
        s_band = jnp.einsum('bikgd,bjkd->bkgij', q_blk, k_blk, preferred_element_type=jnp.float32) * ATTN_SCALE
        valid = band_ok & ((n > 0) | (jj >= BLOCK))
        s_band = jnp.where(valid, s_band, NEG)
        s_meta = jnp.einsum('bikgd,bmkd->bkgim', q_blk, km, preferred_element_type=jnp.float32) * ATTN_SCALE
        p = probs_with_sink(jnp.concatenate([s_meta, s_band], -1)).astype(v_blk.dtype)
        return (jnp.einsum('bkgim,bmkd->bikgd', p[..., :N_META], vm)
                + jnp.einsum('bkgij,bjkd->bikgd', p[..., N_META:], v_blk))

    out_r = lax.map(block_fn, (qb, k_band, v_band, jnp.arange(nb)))
    out = jnp.concatenate([out_m, from_blocks(out_r)], 1).reshape(B, L, Q_COLS)
    return jnp.einsum('ble,ed->bld', out, w_out)


def dsa_mixer(h, w_in, b_in, idx_k_g, idx_k_b, w_out):
    B, L, _ = h.shape
    S = L - N_META
    topk = min(TOPK_MAX, S // 4)
    proj = jnp.einsum('bld,de->ble', h, w_in) + b_in
    cuts = np.cumsum([Q_COLS, KV_COLS, KV_COLS, IDX_HEADS * IDX_DIM, IDX_DIM]).tolist()
    q, k, v, qi, ki, wi = jnp.split(proj, cuts, axis=-1)
    q, k, v = split_heads(q, k, v, L)
    pos = jnp.arange(L)
    qi = rope(qi.reshape(B, L, IDX_HEADS, IDX_DIM), pos, IDX_ROT_DIM)
    ki = rope(layer_norm(ki, idx_k_g, idx_k_b)[:, :, None, :], pos, IDX_ROT_DIM)[:, :, 0]
    wi = wi * IDX_W_SCALE

    qm, km, vm = q[:, :N_META], k[:, :N_META], v[:, :N_META]
    s_mm = jnp.einsum('bikgd,bjkd->bkgij', qm, km, preferred_element_type=jnp.float32) * ATTN_SCALE
    s_mm = jnp.where(meta_causal_mask(), s_mm, NEG)
    out_m = jnp.einsum('bkgij,bjkd->bikgd', jax.nn.softmax(s_mm, -1).astype(v.dtype), vm)

    kr, vr = k[:, N_META:], v[:, N_META:]
    kir = ki[:, N_META:]
    s_idx = jnp.arange(S)
    qb = to_blocks(q[:, N_META:])
    qib = to_blocks(qi[:, N_META:])
    wib = to_blocks(wi[:, N_META:])
    nb = qb.shape[0]
    gather_rows = jax.vmap(lambda arr, idx: arr[idx])

    def block_fn(args):
        q_blk, qi_blk, wi_blk, n = args
        t = n * BLOCK + jnp.arange(BLOCK)
        dots = jnp.einsum('bihd,bsd->bhis', qi_blk, kir, preferred_element_type=jnp.float32) * IDX_SCALE
        score = jnp.einsum('bhis,bih->bis', jax.nn.relu(dots), wi_blk.astype(jnp.float32))
        score = jnp.where(s_idx[None, :] <= t[:, None], score, NEG)
        _, sel = lax.top_k(score, topk)
        sel_ok = sel <= t[None, :, None]
        k_sel = gather_rows(kr, sel)
        v_sel = gather_rows(vr, sel)
        s_sel = jnp.einsum('bikgd,bijkd->bkgij', q_blk, k_sel, preferred_element_type=jnp.float32) * ATTN_SCALE
        s_sel = jnp.where(sel_ok[:, None, None], s_sel, NEG)
        s_meta = jnp.einsum('bikgd,bmkd->bkgim', q_blk, km, preferred_element_type=jnp.float32) * ATTN_SCALE
        p = jax.nn.softmax(jnp.concatenate([s_meta, s_sel], -1), -1).astype(v_sel.dtype)
        return (jnp.einsum('bkgim,bmkd->bikgd', p[..., :N_META], vm)
                + jnp.einsum('bkgij,bijkd->bikgd', p[..., N_META:], v_sel))

    out_r = lax.map(block_fn, (qb, qib, wib, jnp.arange(nb)))
    out = jnp.concatenate([out_m, from_blocks(out_r)], 1).reshape(B, L, Q_COLS)
    return jnp.einsum('ble,ed->bld', out, w_out)


def moe_ffn(h, w_router, b_router, w_gu, b_gu, w_down, b_down):
    B, L, D = h.shape
    x = h.reshape(-1, D)
    N = x.shape[0]
    logits = (jnp.einsum('nd,de->ne', x, w_router) + b_router).astype(jnp.float32)
    top_logit, top_e = lax.top_k(logits, TOP_K)
    gates = jax.nn.softmax(top_logit, -1)
    A = N * TOP_K
    flat_e = top_e.reshape(-1)
    flat_tok = jnp.repeat(jnp.arange(N, dtype=jnp.int32), TOP_K)
    flat_g = gates.reshape(-1)
    order = jnp.argsort(flat_e)
    e_s, tok_s, g_s = flat_e[order], flat_tok[order], flat_g[order]
    counts = jnp.bincount(flat_e, length=N_EXPERTS)
    padded = (counts + MOE_BLOCK - 1) // MOE_BLOCK * MOE_BLOCK
    start = jnp.cumsum(counts) - counts
    pend = jnp.cumsum(padded)
    pstart = pend - padded
    dest = pstart[e_s] + (jnp.arange(A) - start[e_s])
    P = (-(-A // MOE_BLOCK) + N_EXPERTS) * MOE_BLOCK
    nblk = P // MOE_BLOCK
    tok_pad = jnp.full((P,), N, jnp.int32).at[dest].set(tok_s)
    g_pad = jnp.zeros((P,), jnp.float32).at[dest].set(g_s)
    blk_e = jnp.minimum(jnp.searchsorted(pend, jnp.arange(nblk) * MOE_BLOCK, side='right'), N_EXPERTS - 1)
    x_pad = jnp.concatenate([x, jnp.zeros((1, D), x.dtype)], 0)

    def expert_block(args):
        tok, g, e = args
        xb = x_pad[tok]
        gu = xb @ w_gu[e] + b_gu[e]
        gate = jnp.minimum(gu[:, :D_FF], SWIGLU_LIMIT)
        up = jnp.clip(gu[:, D_FF:], -SWIGLU_LIMIT, SWIGLU_LIMIT)
        act = gate * jax.nn.sigmoid(SWIGLU_ALPHA * gate) * (up + 1)
        y = act @ w_down[e] + b_down[e]
        return y * g[:, None].astype(y.dtype)

    y = lax.map(expert_block, (tok_pad.reshape(nblk, MOE_BLOCK), g_pad.reshape(nblk, MOE_BLOCK), blk_e))
    out = jax.ops.segment_sum(y.reshape(P, D), tok_pad, num_segments=N + 1)[:N]
    return out.reshape(B, L, D)


def setup_inputs(seed: int = 0) -> dict:
    key = jax.random.key(seed)
    ks = jax.random.split(key, 24)
    nrm = lambda k, shape, s: jax.random.normal(k, shape, jnp.float32) * s
    D = D_MODEL
    return {
        'x': nrm(ks[0], (BATCH, SEQ, D), 1.0),
        'meta_tokens': nrm(ks[1], (N_META, D), 1.0),
        'w_in_a': nrm(ks[2], (N_LAYERS_A, D, IN_COLS_A), D ** -0.5),
        'b_in_a': nrm(ks[3], (N_LAYERS_A, IN_COLS_A), 0.02),
        'sinks_a': nrm(ks[4], (N_LAYERS_A, N_HEADS), 1.0),
        'w_out_a': nrm(ks[5], (N_LAYERS_A, Q_COLS, D), BETA * Q_COLS ** -0.5),
        'w_in_b': nrm(ks[6], (N_LAYERS_B, D, IN_COLS_B), D ** -0.5),
        'b_in_b': nrm(ks[7], (N_LAYERS_B, IN_COLS_B), 0.02),
        'idx_k_norm_g': 1.0 + nrm(ks[8], (N_LAYERS_B, IDX_DIM), 0.02),
        'idx_k_norm_b': nrm(ks[9], (N_LAYERS_B, IDX_DIM), 0.02),
        'w_out_b': nrm(ks[10], (N_LAYERS_B, Q_COLS, D), BETA * Q_COLS ** -0.5),
        'ln_mix_g': 1.0 + nrm(ks[11], (DEPTH, D), 0.02),
        'ln_mix_b': nrm(ks[12], (DEPTH, D), 0.02),
        'w_router': nrm(ks[13], (DEPTH, D, N_EXPERTS), D ** -0.5),
        'b_router': nrm(ks[14], (DEPTH, N_EXPERTS), 0.01),
        'w_gate_up': nrm(ks[15], (DEPTH, N_EXPERTS, D, 2 * D_FF), D ** -0.5),
        'b_gate_up': nrm(ks[16], (DEPTH, N_EXPERTS, 2 * D_FF), 0.02),
        'w_down': nrm(ks[17], (DEPTH, N_EXPERTS, D_FF, D), BETA * D_FF ** -0.5),
        'b_down': nrm(ks[18], (DEPTH, N_EXPERTS, D), 0.02),
        'ln_ffn_g': 1.0 + nrm(ks[19], (DEPTH, D), 0.02),
        'ln_ffn_b': nrm(ks[20], (DEPTH, D), 0.02),
    }


def reference(x, meta_tokens, w_in_a, b_in_a, sinks_a, w_out_a, w_in_b, b_in_b,
              idx_k_norm_g, idx_k_norm_b, w_out_b, ln_mix_g, ln_mix_b, w_router, b_router,
              w_gate_up, b_gate_up, w_down, b_down, ln_ffn_g, ln_ffn_b):
    B = x.shape[0]
    meta = jnp.broadcast_to(meta_tokens[None].astype(x.dtype), (B, N_META, x.shape[-1]))
    h = jnp.concatenate([meta, x], 1)
    for i in range(DEPTH):
        j = i // N_MIXERS
        if i % N_MIXERS == 0:
            mix = swa_sink_mixer(h, w_in_a[j], b_in_a[j], sinks_a[j], w_out_a[j])
        else:
            mix = dsa_mixer(h, w_in_b[j], b_in_b[j], idx_k_norm_g[j], idx_k_norm_b[j], w_out_b[j])
        h = layer_norm(ALPHA * h + mix, ln_mix_g[i], ln_mix_b[i])
        ffn = moe_ffn(h, w_router[i], b_router[i], w_gate_up[i], b_gate_up[i], w_down[i], b_down[i])
        h = layer_norm(ALPHA * h + ffn, ln_ffn_g[i], ln_ffn_b[i])
    return h[:, N_META:]
```

```python
import functools

import jax
import jax.numpy as jnp
import numpy as np
from jax import lax
from jax.experimental import pallas as pl
from jax.experimental.pallas import tpu as pltpu

N_META = 16
HEAD_DIM = 64
N_HEADS = 16
N_KV_HEADS = 2
GROUP = N_HEADS // N_KV_HEADS
Q_COLS = N_HEADS * HEAD_DIM
KV_COLS = N_KV_HEADS * HEAD_DIM
ROT_DIM = HEAD_DIM // 4
ROPE_THETA = 500000.0
ATTN_SCALE = HEAD_DIM ** -0.5
BLOCK = 128
IDX_HEADS = 8
IDX_DIM = 64
IDX_ROT_DIM = IDX_DIM // 2
IDX_SCALE = IDX_DIM ** -0.5
IDX_W_SCALE = IDX_HEADS ** -0.5
TOPK_MAX = 256
N_EXPERTS = 32
TOP_K = 4
SWIGLU_LIMIT = 7.0
SWIGLU_ALPHA = 1.702
LN_EPS = 1e-5
NEG = -1e30
INT_MIN = -2147483648

LANES = 128
ROW_TILE = 512
MOE_ROWS = 512
FF_CHUNK = 256
IDX_CHUNK_BLOCKS = 4
VMEM_LIMIT = 48 * 1024 * 1024

F32 = jnp.float32
BF16 = jnp.bfloat16


def _cparams(*sem):
    return pltpu.CompilerParams(dimension_semantics=sem, vmem_limit_bytes=VMEM_LIMIT)


def _dot(a, b):
    return jnp.dot(a, b, preferred_element_type=F32)


def _dot_nt(a, b):
    return lax.dot_general(a, b, (((1,), (1,)), ((), ())), preferred_element_type=F32)


def _layer_norm(z, g, b):
    mu = jnp.mean(z, axis=-1, keepdims=True)
    d = z - mu
    var = jnp.mean(d * d, axis=-1, keepdims=True)
    return d * lax.rsqrt(var + LN_EPS) * g + b


def _rope128(blk, tab_ref, half):
    return (blk * tab_ref[0] + pltpu.roll(blk, half, 1) * tab_ref[1]
            + pltpu.roll(blk, LANES - half, 1) * tab_ref[2])


def _inproj_kernel(*refs, dsa):
    if dsa:
        (h_ref, wq_ref, wkv_ref, wqi_ref, wki_ref, wvt_ref, wwt_ref, bq_ref, bkv_ref, bqi_ref,
         bki_ref, bvt_ref, bwt_ref, tabq_ref, tabi_ref, g_ref, bn_ref,
         q_ref, kv_ref, qi_ref, ki_ref, vt_ref, wt_ref) = refs
    else:
        h_ref, wq_ref, wkv_ref, bq_ref, bkv_ref, tabq_ref, q_ref, kv_ref = refs
    x = h_ref[...].astype(BF16)
    q = _dot(x, wq_ref[...]) + bq_ref[...]
    for j in range(Q_COLS // LANES):
        sl = slice(j * LANES, (j + 1) * LANES)
        q_ref[:, sl] = _rope128(q[:, sl], tabq_ref, ROT_DIM // 2).astype(BF16)
    kv = _dot(x, wkv_ref[...]) + bkv_ref[...]
    kv_ref[:, :KV_COLS] = _rope128(kv[:, :KV_COLS], tabq_ref, ROT_DIM // 2).astype(BF16)
    kv_ref[:, KV_COLS:] = kv[:, KV_COLS:].astype(BF16)
    if not dsa:
        return
    qi = _dot(x, wqi_ref[...]) + bqi_ref[...]
    for j in range(IDX_HEADS * IDX_DIM // LANES):
        sl = slice(j * LANES, (j + 1) * LANES)
        qi_ref[:, sl] = _rope128(qi[:, sl], tabi_ref, IDX_ROT_DIM // 2).astype(BF16)
    ki = _dot(x, wki_ref[...]) + bki_ref[...]
    real = lax.broadcasted_iota(jnp.int32, ki.shape, 1) < IDX_DIM
    mu = jnp.sum(jnp.where(real, ki, 0.0), axis=-1, keepdims=True) * (1.0 / IDX_DIM)
    d = jnp.where(real, ki - mu, 0.0)
    var = jnp.sum(d * d, axis=-1, keepdims=True) * (1.0 / IDX_DIM)
    kn = d * lax.rsqrt(var + LN_EPS) * g_ref[...] + bn_ref[...]
    ki_ref[...] = _rope128(kn, tabi_ref, IDX_ROT_DIM // 2)[:, :IDX_DIM].astype(BF16)
    vt = _dot_nt(wvt_ref[...], x) + bvt_ref[...]
    for j in range(vt_ref.shape[0]):
        vt_ref[j] = vt[:, j * BLOCK:(j + 1) * BLOCK].astype(BF16)
    wt_ref[...] = (_dot_nt(wwt_ref[...], x) + bwt_ref[...]) * IDX_W_SCALE


def _rope_tables(pos, rot_dim):
    half = rot_dim // 2
    inv_freq = ROPE_THETA ** (-jnp.arange(half, dtype=F32) / half)
    ang = pos.astype(F32)[:, None] * inv_freq[None, :]
    cos, sin = jnp.cos(ang), jnp.sin(ang)
    n = pos.shape[0]
    rest = HEAD_DIM - rot_dim
    c = jnp.concatenate([cos, cos, jnp.ones((n, rest), F32)], 1)
    s_up = jnp.concatenate([jnp.zeros((n, half), F32), sin, jnp.zeros((n, rest), F32)], 1)
    s_lo = jnp.concatenate([-sin, jnp.zeros((n, HEAD_DIM - half), F32)], 1)
    return jnp.stack([jnp.tile(t, (1, LANES // HEAD_DIM)) for t in (c, s_up, s_lo)])


def _inproj(h, wp, tabq, tabi, *, dsa, seq):
    m, d = h.shape
    tm = min(ROW_TILE, m)
    nsb = seq // tm
    row = lambda i: (i, 0)
    const = lambda i: (0, 0)
    tab_map = lambda i: (0, i % nsb, 0)
    full = lambda a: pl.BlockSpec(a.shape, const)
    names = ("wq", "wkv", "wqi", "wki", "wvt", "wwt", "bq", "bkv", "bqi", "bki", "bvt", "bwt") if dsa \
        else ("wq", "wkv", "bq", "bkv")
    ins = [h] + [wp[k] for k in names] + [tabq]
    in_specs = [pl.BlockSpec((tm, d), row)] + [full(wp[k]) for k in names] \
        + [pl.BlockSpec((3, tm, LANES), tab_map)]
    out_shape = [jax.ShapeDtypeStruct((m, Q_COLS), BF16), jax.ShapeDtypeStruct((m, 2 * KV_COLS), BF16)]
    out_specs = [pl.BlockSpec((tm, Q_COLS), row), pl.BlockSpec((tm, 2 * KV_COLS), row)]
    if dsa:
        ins += [tabi, wp["g"], wp["bn"]]
        in_specs += [pl.BlockSpec((3, tm, LANES), tab_map), full(wp["g"]), full(wp["bn"])]
        out_shape += [jax.ShapeDtypeStruct((m, IDX_HEADS * IDX_DIM), BF16),
                      jax.ShapeDtypeStruct((m, IDX_DIM), BF16),
                      jax.ShapeDtypeStruct((m // BLOCK, KV_COLS, BLOCK), BF16),
                      jax.ShapeDtypeStruct((IDX_HEADS, m), F32)]
        out_specs += [pl.BlockSpec((tm, IDX_HEADS * IDX_DIM), row),
                      pl.BlockSpec((tm, IDX_DIM), row),
                      pl.BlockSpec((tm // BLOCK, KV_COLS, BLOCK), lambda i: (i, 0, 0)),
                      pl.BlockSpec((IDX_HEADS, tm), lambda i: (0, i))]
    return pl.pallas_call(
        functools.partial(_inproj_kernel, dsa=dsa), grid=(m // tm,), in_specs=in_specs,
        out_specs=out_specs, out_shape=out_shape, compiler_params=_cparams("parallel"),
        name="inproj_dsa" if dsa else "inproj_swa")(*ins)


def _stack_heads(ref, first, width=HEAD_DIM, scale=None):
    parts = [ref[:, (first + g) * width:(first + g + 1) * width] for g in range(GROUP)]
    x = jnp.concatenate(parts, axis=0)
    return x if scale is None else x * scale


def _local_attn_kernel(sink_ref, q_ref, kvc_ref, kvp_ref, kvm_ref, o_ref, *, banded):
    n = pl.program_id(1)
    rows = GROUP * BLOCK
    r = lax.broadcasted_iota(jnp.int32, (rows, BLOCK), 0) & (BLOCK - 1)
    c = lax.broadcasted_iota(jnp.int32, (rows, BLOCK), 1)
    mask_c = c <= r
    mask_p = (c > r) & (n > 0)
    mask_m = c < N_META
    for kh in range(N_KV_HEADS):
        ks = slice(kh * HEAD_DIM, (kh + 1) * HEAD_DIM)
        vs = slice(KV_COLS + kh * HEAD_DIM, KV_COLS + (kh + 1) * HEAD_DIM)
        qs = _stack_heads(q_ref, kh * GROUP, scale=ATTN_SCALE)
        sink = jnp.concatenate(
            [jnp.full((BLOCK, 1), sink_ref[kh * GROUP + g], F32) for g in range(GROUP)], axis=0)
        parts = [(jnp.where(mask_c, _dot_nt(qs, kvc_ref[:, ks]), NEG), kvc_ref[:, vs])]
        if banded:
            parts.append((jnp.where(mask_p, _dot_nt(qs, kvp_ref[:, ks]), NEG), kvp_ref[:, vs]))
            parts.append((jnp.where(mask_m, _dot_nt(qs, kvm_ref[:, ks]), NEG), kvm_ref[:, vs]))
        m = sink
        for s, _ in parts:
            m = jnp.maximum(m, jnp.max(s, axis=-1, keepdims=True))
        den = jnp.exp(sink - m)
        out = jnp.zeros((rows, HEAD_DIM), F32)
        for s, v in parts:
            p = jnp.exp(s - m)
            den = den + jnp.sum(p, axis=-1, keepdims=True)
            out = out + _dot(p.astype(BF16), v)
        out = out / den
        for g in range(GROUP):
            h = kh * GROUP + g
            o_ref[:, h * HEAD_DIM:(h + 1) * HEAD_DIM] = out[g * BLOCK:(g + 1) * BLOCK].astype(BF16)


def _local_attn(sinks, q, kv, kvm, *, banded, batch):
    m = q.shape[0]
    nb = m // BLOCK // batch
    qmap = lambda b, n, s: (b * nb + n, 0)
    pmap = lambda b, n, s: (b * nb + jnp.maximum(n - 1, 0), 0)
    mmap = lambda b, n, s: (0, 0)
    grid_spec = pltpu.PrefetchScalarGridSpec(
        num_scalar_prefetch=1, grid=(batch, nb),
        in_specs=[pl.BlockSpec((BLOCK, Q_COLS), qmap), pl.BlockSpec((BLOCK, 2 * KV_COLS), qmap),
                  pl.BlockSpec((BLOCK, 2 * KV_COLS), pmap), pl.BlockSpec((BLOCK, 2 * KV_COLS), mmap)],
        out_specs=pl.BlockSpec((BLOCK, Q_COLS), qmap))
    return pl.pallas_call(
        functools.partial(_local_attn_kernel, banded=banded), grid_spec=grid_spec,
        out_shape=jax.ShapeDtypeStruct((m, Q_COLS), BF16),
        compiler_params=_cparams("parallel", "parallel"),
        name="swa_attn" if banded else "meta_attn")(sinks, q, kv, kv, kvm)


def _sortable(x):
    bits = lax.bitcast_convert_type(x, jnp.int32)
    return jnp.where(bits < 0, bits ^ jnp.int32(0x7FFFFFFF), bits)


def _dsa_kernel(q_ref, qi_ref, wt_ref, k3_ref, vt3_ref, ki3_ref, kvm_ref, vmt_ref, o_ref,
                key_s, acc_s, *, ksel, idx_bits):
    n = pl.program_id(1)
    cb = IDX_CHUNK_BLOCKS
    ck = cb * BLOCK
    n_chunks = n // cb + 1
    lane = lax.broadcasted_iota(jnp.int32, (1, BLOCK), 1)
    t_row = n * BLOCK + lane

    qis = _stack_heads(qi_ref, 0, IDX_DIM)
    wts = [wt_ref[h:h + 1, :] for h in range(IDX_HEADS)]

    def score_chunk(c, carry):
        kic = ki3_ref[pl.ds(c * cb, cb)].reshape(ck, IDX_DIM)
        d = _dot_nt(kic, qis)
        sc = jnp.zeros((ck, BLOCK), F32)
        for h in range(IDX_HEADS):
            sc = sc + jnp.maximum(d[:, h * BLOCK:(h + 1) * BLOCK], 0.0) * wts[h]
        sc = sc * IDX_SCALE
        s_idx = c * ck + lax.broadcasted_iota(jnp.int32, (ck, BLOCK), 0)
        key = jnp.where(s_idx <= t_row, _sortable(sc), INT_MIN)
        key_s[pl.ds(c * cb, cb)] = key.reshape(cb, BLOCK, BLOCK)
        return carry

    lax.fori_loop(0, n_chunks, score_chunk, 0)

    def count(pred):
        def body(c, acc):
            keys = key_s[pl.ds(c * cb, cb)].reshape(ck, BLOCK)
            s_idx = c * ck + lax.broadcasted_iota(jnp.int32, (ck, BLOCK), 0)
            hit = jnp.where(pred(keys, s_idx), 1, 0).astype(jnp.int32)
            return acc + jnp.sum(hit.reshape(ck // 8, 8, BLOCK), axis=0)
        acc = lax.fori_loop(0, n_chunks, body, jnp.zeros((8, BLOCK), jnp.int32))
        return jnp.sum(acc, axis=0, keepdims=True)

    def thr_bit(i, tu):
        cand_u = tu | lax.shift_left(jnp.int32(1), 31 - i)
        cand = cand_u ^ jnp.int32(INT_MIN)
        return jnp.where(count(lambda k, s: k >= cand) >= ksel, cand_u, tu)

    thr_u = lax.fori_loop(0, 32, thr_bit, jnp.zeros((1, BLOCK), jnp.int32))
    thr = thr_u ^ jnp.int32(INT_MIN)
    n_gt = count(lambda k, s: k > thr)
    need = ksel - n_gt

    def tie_bit(i, x):
        cand = x | lax.shift_left(jnp.int32(1), idx_bits - 1 - i)
        return jnp.where(count(lambda k, s: (k == thr) & (s < cand)) <= need - 1, cand, x)

    tie_x = lax.fori_loop(0, idx_bits, tie_bit, jnp.zeros((1, BLOCK), jnp.int32))

    qs = [_stack_heads(q_ref, kh * GROUP, scale=ATTN_SCALE) for kh in range(N_KV_HEADS)]
    krow = lax.broadcasted_iota(jnp.int32, (BLOCK, BLOCK), 0)
    bias_m = jnp.where(krow < N_META, 0.0, NEG)
    bias_m = jnp.concatenate([bias_m] * GROUP, axis=1)
    m0, l0 = [], []
    for kh in range(N_KV_HEADS):
        hs = slice(kh * HEAD_DIM, (kh + 1) * HEAD_DIM)
        s = _dot_nt(kvm_ref[:, hs], qs[kh]) + bias_m
        m = jnp.max(s, axis=0, keepdims=True)
        p = jnp.exp(s - m)
        m0.append(m)
        l0.append(jnp.sum(p, axis=0, keepdims=True))
        acc_s[kh] = _dot(vmt_ref[hs, :], p.astype(BF16))

    def attend(j, carry):
        ms, ls = carry
        keys = key_s[j]
        s_idx = j * BLOCK + krow
        sel = ((keys > thr) | ((keys == thr) & (s_idx <= tie_x))) & (s_idx <= t_row)
        bias = jnp.where(sel, 0.0, NEG)
        bias = jnp.concatenate([bias] * GROUP, axis=1)
        kj = k3_ref[j]
        vtj = vt3_ref[j]
        new_m, new_l = [], []
        for kh in range(N_KV_HEADS):
            hs = slice(kh * HEAD_DIM, (kh + 1) * HEAD_DIM)
            s = _dot_nt(kj[:, hs], qs[kh]) + bias
            m = jnp.maximum(ms[kh], jnp.max(s, axis=0, keepdims=True))
            alpha = jnp.exp(ms[kh] - m)
            p = jnp.exp(s - m)
            new_m.append(m)
            new_l.append(ls[kh] * alpha + jnp.sum(p, axis=0, keepdims=True))
            acc_s[kh] = acc_s[kh] * alpha + _dot(vtj[hs, :], p.astype(BF16))
        return tuple(new_m), tuple(new_l)

    _, ls = lax.fori_loop(0, n + 1, attend, (tuple(m0), tuple(l0)))
    for kh in range(N_KV_HEADS):
        out_t = acc_s[kh] / ls[kh]
        for g in range(GROUP):
            h = kh * GROUP + g
            o_ref[:, h * HEAD_DIM:(h + 1) * HEAD_DIM] = out_t[:, g * BLOCK:(g + 1) * BLOCK].T.astype(BF16)


def _dsa_attn(q, qi, wt, k3, vt3, ki3, kvm, vmt, *, batch):
    m = q.shape[0]
    nb = m // BLOCK // batch
    seq = nb * BLOCK
    assert nb % IDX_CHUNK_BLOCKS == 0
    ksel = min(TOPK_MAX, seq // 4)
    idx_bits = max(1, int(np.ceil(np.log2(seq))))
    qmap = lambda b, n: (b * nb + n, 0)
    bmap = lambda b, n: (b, 0, 0)
    return pl.pallas_call(
        functools.partial(_dsa_kernel, ksel=ksel, idx_bits=idx_bits), grid=(batch, nb),
        in_specs=[pl.BlockSpec((BLOCK, Q_COLS), qmap),
                  pl.BlockSpec((BLOCK, IDX_HEADS * IDX_DIM), qmap),
                  pl.BlockSpec((IDX_HEADS, BLOCK), lambda b, n: (0, b * nb + n)),
                  pl.BlockSpec((nb, BLOCK, KV_COLS), bmap),
                  pl.BlockSpec((nb, KV_COLS, BLOCK), bmap),
                  pl.BlockSpec((nb, BLOCK, IDX_DIM), bmap),
                  pl.BlockSpec((BLOCK, 2 * KV_COLS), lambda b, n: (0, 0)),
                  pl.BlockSpec((None, KV_COLS, BLOCK), lambda b, n: (0, 0, 0))],
        out_specs=pl.BlockSpec((BLOCK, Q_COLS), qmap),
        out_shape=jax.ShapeDtypeStruct((m, Q_COLS), BF16),
        scratch_shapes=[pltpu.VMEM((nb, BLOCK, BLOCK), jnp.int32),
                        pltpu.VMEM((N_KV_HEADS, HEAD_DIM, GROUP * BLOCK), F32)],
        compiler_params=_cparams("parallel", "arbitrary"), name="dsa_attn",
    )(q, qi, wt, k3, vt3, ki3, kvm, vmt)


def _outproj_kernel(a_ref, w_ref, h_ref, g_ref, b_ref, wr_ref, br_ref, tri_ref,
                    h1_ref, route_ref, cnt_ref, carry_s, *, alpha):
    i = pl.program_id(0)

    @pl.when(i == 0)
    def _():
        carry_s[...] = jnp.zeros_like(carry_s)

    y = _dot(a_ref[...], w_ref[...])
    h1 = _layer_norm(alpha * h_ref[...] + y, g_ref[...], b_ref[...])
    h1_ref[...] = h1
    logits = _dot(h1.astype(BF16), wr_ref[...]) + br_ref[...]
    lane = lax.broadcasted_iota(jnp.int32, logits.shape, 1)
    idxs, vals = [], []
    rem = logits
    for _ in range(TOP_K):
        mx = jnp.max(rem, axis=-1, keepdims=True)
        ix = jnp.min(jnp.where(rem == mx, lane, LANES), axis=-1, keepdims=True)
        idxs.append(ix)
        vals.append(mx)
        rem = jnp.where(lane == ix, -jnp.inf, rem)
    exps = [jnp.exp(v - vals[0]) for v in vals]
    den = exps[0] + exps[1] + exps[2] + exps[3]
    sel = jnp.where(rem == -jnp.inf, 1.0, 0.0)
    before = _dot(tri_ref[...], sel.astype(BF16)) + carry_s[...]
    route = jnp.zeros(logits.shape, F32)
    for k in range(TOP_K):
        rank = jnp.sum(jnp.where(lane == idxs[k], before, 0.0), axis=-1, keepdims=True)
        route = jnp.where(lane == k, idxs[k].astype(F32), route)
        route = jnp.where(lane == TOP_K + k, exps[k] / den, route)
        route = jnp.where(lane == 2 * TOP_K + k, rank, route)
    route_ref[...] = route
    carry_s[...] = carry_s[...] + jnp.sum(sel, axis=0, keepdims=True)
    cnt_ref[...] = carry_s[...]


def _outproj(a, w, h, g, b, wr, br, *, alpha):
    m, d = h.shape
    tm = min(ROW_TILE, m)
    tri = jnp.tril(jnp.ones((tm, tm), BF16), -1)
    row = lambda i: (i, 0)
    const = lambda i: (0, 0)
    full = lambda x: pl.BlockSpec(x.shape, const)
    return pl.pallas_call(
        functools.partial(_outproj_kernel, alpha=alpha), grid=(m // tm,),
        in_specs=[pl.BlockSpec((tm, Q_COLS), row), full(w), pl.BlockSpec((tm, d), row),
                  full(g), full(b), full(wr), full(br), full(tri)],
        out_specs=[pl.BlockSpec((tm, d), row), pl.BlockSpec((tm, LANES), row),
                   pl.BlockSpec((1, LANES), const)],
        out_shape=[jax.ShapeDtypeStruct((m, d), F32), jax.ShapeDtypeStruct((m, LANES), F32),
                   jax.ShapeDtypeStruct((1, LANES), F32)],
        scratch_shapes=[pltpu.VMEM((1, LANES), F32)],
        compiler_params=_cparams("arbitrary"), name="outproj_router")(a, w, h, g, b, wr, br, tri)


def _scatter_kernel(dest_ref, h_ref, xs_in, xs_out, sem):
    del xs_in
    tm = h_ref.shape[0]

    def row_copy(r, d):
        return pltpu.make_async_copy(h_ref.at[pl.ds(r, 1)], xs_out.at[pl.ds(d, 1)], sem)

    def issue(r, c):
        for k in range(TOP_K):
            row_copy(r, dest_ref[0, 0, r * TOP_K + k]).start()
        return c

    def drain(r, c):
        for k in range(TOP_K):
            row_copy(r, dest_ref[0, 0, r * TOP_K + k]).wait()
        return c

    lax.fori_loop(0, tm, issue, 0)
    lax.fori_loop(0, tm, drain, 0)


def _scatter_rows(dest3, h, n_rows):
    m, d = h.shape
    tm = dest3.shape[-1] // TOP_K
    xs = jnp.zeros((n_rows, d), F32)
    return pl.pallas_call(
        _scatter_kernel, grid=(m // tm,),
        in_specs=[pl.BlockSpec((1, 1, tm * TOP_K), lambda i: (i, 0, 0), memory_space=pltpu.SMEM),
                  pl.BlockSpec((tm, d), lambda i: (i, 0)),
                  pl.BlockSpec(memory_space=pl.ANY)],
        out_specs=pl.BlockSpec(memory_space=pl.ANY),
        out_shape=jax.ShapeDtypeStruct((n_rows, d), F32),
        scratch_shapes=[pltpu.SemaphoreType.DMA],
        input_output_aliases={2: 0},
        compiler_params=_cparams("arbitrary"), name="moe_scatter")(dest3, h, xs)


def _expert_ffn(x, wgu_ref, bgu_ref, wd_ref, bd_ref):
    ff = wd_ref.shape[0]
    acc = jnp.zeros((x.shape[0], wd_ref.shape[1]), F32)
    for c in range(ff // FF_CHUNK):
        gs = slice(c * FF_CHUNK, (c + 1) * FF_CHUNK)
        us = slice(ff + c * FF_CHUNK, ff + (c + 1) * FF_CHUNK)
        gate = jnp.minimum(_dot(x, wgu_ref[:, gs]) + bgu_ref[:, gs], SWIGLU_LIMIT)
        up = jnp.clip(_dot(x, wgu_ref[:, us]) + bgu_ref[:, us], -SWIGLU_LIMIT, SWIGLU_LIMIT)
        act = gate * jax.nn.sigmoid(SWIGLU_ALPHA * gate) * (up + 1.0)
        acc = acc + _dot(act.astype(BF16), wd_ref[gs, :])
    return acc + bd_ref[...]


def _gmm_kernel(be_ref, nu_ref, xs_ref, wgu_ref, bgu_ref, wd_ref, bd_ref, y_ref):
    del be_ref
    i = pl.program_id(0)

    @pl.when(i < nu_ref[0])
    def _():
        y_ref[...] = _expert_ffn(xs_ref[...].astype(BF16), wgu_ref, bgu_ref, wd_ref, bd_ref)

    @pl.when(i >= nu_ref[0])
    def _():
        y_ref[...] = jnp.zeros_like(y_ref)


def _gmm(blk_e, n_used, xs, wgu, bgu, wd, bd):
    p, d = xs.shape
    ff = wd.shape[1]
    emap = lambda i, be, nu: (be[i], 0, 0)
    grid_spec = pltpu.PrefetchScalarGridSpec(
        num_scalar_prefetch=2, grid=(p // MOE_ROWS,),
        in_specs=[pl.BlockSpec((MOE_ROWS, d), lambda i, be, nu: (i, 0)),
                  pl.BlockSpec((None, d, 2 * ff), emap), pl.BlockSpec((None, 1, 2 * ff), emap),
                  pl.BlockSpec((None, ff, d), emap), pl.BlockSpec((None, 1, d), emap)],
        out_specs=pl.BlockSpec((MOE_ROWS, d), lambda i, be, nu: (i, 0)))
    return pl.pallas_call(
        _gmm_kernel, grid_spec=grid_spec, out_shape=jax.ShapeDtypeStruct((p, d), F32),
        compiler_params=_cparams("arbitrary"), name="moe_experts")(blk_e, n_used, xs, wgu, bgu, wd, bd)


def _combine_kernel(dest_ref, route_ref, h_ref, g_ref, b_ref, ys_ref, o_ref, ybuf, sem, *, alpha):
    tm = h_ref.shape[0]

    def row_copy(r, k):
        d = dest_ref[0, 0, r * TOP_K + k]
        return pltpu.make_async_copy(ys_ref.at[pl.ds(d, 1)], ybuf.at[k, pl.ds(r, 1)], sem)

    def issue(r, c):
        for k in range(TOP_K):
            row_copy(r, k).start()
        return c

    def drain(r, c):
        for k in range(TOP_K):
            row_copy(r, k).wait()
        return c

    lax.fori_loop(0, tm, issue, 0)
    lax.fori_loop(0, tm, drain, 0)
    z = alpha * h_ref[...]
    for k in range(TOP_K):
        z = z + route_ref[:, TOP_K + k:TOP_K + k + 1] * ybuf[k]
    o_ref[...] = _layer_norm(z, g_ref[...], b_ref[...])


def _combine(dest3, route, h, g, b, ys, *, alpha):
    m, d = h.shape
    tm = dest3.shape[-1] // TOP_K
    row = lambda i: (i, 0)
    const = lambda i: (0, 0)
    return pl.pallas_call(
        functools.partial(_combine_kernel, alpha=alpha), grid=(m // tm,),
        in_specs=[pl.BlockSpec((1, 1, tm * TOP_K), lambda i: (i, 0, 0), memory_space=pltpu.SMEM),
                  pl.BlockSpec((tm, LANES), row), pl.BlockSpec((tm, d), row),
                  pl.BlockSpec(g.shape, const), pl.BlockSpec(b.shape, const),
                  pl.BlockSpec(memory_space=pl.ANY)],
        out_specs=pl.BlockSpec((tm, d), row),
        out_shape=jax.ShapeDtypeStruct((m, d), F32),
        scratch_shapes=[pltpu.VMEM((TOP_K, tm, d), F32), pltpu.SemaphoreType.DMA],
        compiler_params=_cparams("arbitrary"), name="moe_combine")(dest3, route, h, g, b, ys)


def _meta_moe_kernel(hm_ref, gm_ref, wgu_ref, bgu_ref, wd_ref, bd_ref, g_ref, b_ref, o_ref, acc_s,
                     *, alpha):
    e = pl.program_id(0)

    @pl.when(e == 0)
    def _():
        acc_s[...] = jnp.zeros_like(acc_s)

    y = _expert_ffn(hm_ref[...].astype(BF16), wgu_ref, bgu_ref, wd_ref, bd_ref)
    lane = lax.broadcasted_iota(jnp.int32, gm_ref.shape, 1)
    ge = jnp.sum(jnp.where(lane == e, gm_ref[...], 0.0), axis=-1, keepdims=True)
    acc_s[...] = acc_s[...] + ge * y

    @pl.when(e == pl.num_programs(0) - 1)
    def _():
        o_ref[...] = _layer_norm(alpha * hm_ref[...] + acc_s[...], g_ref[...], b_ref[...])


def _meta_moe(hm, gm, wgu, bgu, wd, bd, g, b, *, alpha):
    m, d = hm.shape
    ff = wd.shape[1]
    const = lambda e: (0, 0)
    emap = lambda e: (e, 0, 0)
    return pl.pallas_call(
        functools.partial(_meta_moe_kernel, alpha=alpha), grid=(wgu.shape[0],),
        in_specs=[pl.BlockSpec((m, d), const), pl.BlockSpec((m, LANES), const),
                  pl.BlockSpec((None, d, 2 * ff), emap), pl.BlockSpec((None, 1, 2 * ff), emap),
                  pl.BlockSpec((None, ff, d), emap), pl.BlockSpec((None, 1, d), emap),
                  pl.BlockSpec(g.shape, const), pl.BlockSpec(b.shape, const)],
        out_specs=pl.BlockSpec((m, d), const),
        out_shape=jax.ShapeDtypeStruct((m, d), F32),
        scratch_shapes=[pltpu.VMEM((m, d), F32)],
        compiler_params=_cparams("arbitrary"), name="meta_experts")(hm, gm, wgu, bgu, wd, bd, g, b)


def _moe(h1, route, cnt, hm1, route_m, ep, g, b, *, alpha):
    m, d = h1.shape
    n_blk = -(-m * TOP_K // MOE_ROWS) + N_EXPERTS
    e = route[:, :TOP_K].astype(jnp.int32)
    rank = route[:, 2 * TOP_K:3 * TOP_K].astype(jnp.int32)
    counts = cnt[0, :N_EXPERTS].astype(jnp.int32)
    padded = (counts + MOE_ROWS - 1) // MOE_ROWS * MOE_ROWS
    pend = jnp.cumsum(padded)
    dest = (pend - padded)[e] + rank
    blk_e = jnp.minimum(jnp.searchsorted(pend, jnp.arange(n_blk, dtype=jnp.int32) * MOE_ROWS, side="right"),
                        N_EXPERTS - 1).astype(jnp.int32)
    n_used = (pend[-1:] // MOE_ROWS).astype(jnp.int32)
    tm = min(ROW_TILE // 2, m)
    dest3 = dest.reshape(m // tm, 1, tm * TOP_K)
    xs = _scatter_rows(dest3, h1, n_blk * MOE_ROWS)
    ys = _gmm(blk_e, n_used, xs, ep["wgu"], ep["bgu"], ep["wd"], ep["bd"])
    h2 = _combine(dest3, route, h1, g, b, ys, alpha=alpha)
    em = route_m[:, :TOP_K].astype(jnp.int32)
    gm = jnp.sum(jax.nn.one_hot(em, LANES, dtype=F32) * route_m[:, TOP_K:2 * TOP_K, None], axis=1)
    hm2 = _meta_moe(hm1, gm, ep["wgu"], ep["bgu"], ep["wd"], ep["bd"], g, b, alpha=alpha)
    return h2, hm2


def _split_in_weights(w, b, dsa, g=None, bn=None):
    cuts = np.cumsum([Q_COLS, KV_COLS, KV_COLS, IDX_HEADS * IDX_DIM, IDX_DIM]).tolist()
    wb = w.astype(BF16)
    row = lambda v: v.reshape(1, -1)
    wp = {"wq": wb[:, :cuts[0]], "wkv": wb[:, cuts[0]:cuts[2]],
          "bq": row(b[:cuts[0]]), "bkv": row(b[cuts[0]:cuts[2]])}
    if dsa:
        pad = LANES - IDX_DIM
        wp.update({
            "wqi": wb[:, cuts[2]:cuts[3]], "bqi": row(b[cuts[2]:cuts[3]]),
            "wki": jnp.pad(wb[:, cuts[3]:cuts[4]], ((0, 0), (0, pad))),
            "bki": row(jnp.pad(b[cuts[3]:cuts[4]], (0, pad))),
            "wvt": wb[:, cuts[1]:cuts[2]].T, "bvt": b[cuts[1]:cuts[2]].reshape(-1, 1),
            "wwt": wb[:, cuts[4]:].T, "bwt": b[cuts[4]:].reshape(-1, 1),
            "g": row(jnp.pad(g, (0, pad))), "bn": row(jnp.pad(bn, (0, pad)))})
    return wp


def kernel(x, meta_tokens, w_in_a, b_in_a, sinks_a, w_out_a, w_in_b, b_in_b, idx_k_norm_g, idx_k_norm_b, w_out_b, ln_mix_g, ln_mix_b, w_router, b_router, w_gate_up, b_gate_up, w_down, b_down, ln_ffn_g, ln_ffn_b):
    batch, seq, d = x.shape
    depth = ln_mix_g.shape[0]
    alpha = float((2 * depth) ** 0.25)
    row = lambda v: v.reshape(1, -1)

    h = x.reshape(batch * seq, d)
    hm = jnp.zeros((BLOCK, d), F32).at[:N_META].set(meta_tokens.astype(F32))
    tab_q = _rope_tables(jnp.arange(N_META, N_META + seq), ROT_DIM)
    tab_i = _rope_tables(jnp.arange(N_META, N_META + seq), IDX_ROT_DIM)
    tab_qm = _rope_tables(jnp.arange(BLOCK), ROT_DIM)
    tab_im = _rope_tables(jnp.arange(BLOCK), IDX_ROT_DIM)
    no_sinks = jnp.full((N_HEADS,), NEG, F32)

    for i in range(depth):
        j = i // 2
        dsa = i % 2 == 1
        if dsa:
            wp = _split_in_weights(w_in_b[j], b_in_b[j], True, idx_k_norm_g[j], idx_k_norm_b[j])
            w_out, sinks = w_out_b[j], no_sinks
        else:
            wp = _split_in_weights(w_in_a[j], b_in_a[j], False)
            w_out, sinks = w_out_a[j], sinks_a[j].astype(F32)
        pr = _inproj(h, wp, tab_q, tab_i, dsa=dsa, seq=seq)
        pm = _inproj(hm, wp, tab_qm, tab_im, dsa=dsa, seq=BLOCK)
        if dsa:
            q, kv, qi, ki, vt3, wt = pr
            nb = seq // BLOCK
            k3 = kv[:, :KV_COLS].reshape(batch * nb, BLOCK, KV_COLS)
            ki3 = ki.reshape(batch * nb, BLOCK, IDX_DIM)
            att = _dsa_attn(q, qi, wt, k3, vt3, ki3, pm[1], pm[4], batch=batch)
        else:
            q, kv = pr
            att = _local_attn(sinks, q, kv, pm[1], banded=True, batch=batch)
        att_m = _local_attn(sinks, pm[0], pm[1], pm[1], banded=False, batch=1)

        w_out = w_out.astype(BF16)
        wr = jnp.pad(w_router[i].astype(BF16), ((0, 0), (0, LANES - N_EXPERTS)))
        br = row(jnp.pad(b_router[i].astype(F32), (0, LANES - N_EXPERTS), constant_values=NEG))
        g1, b1 = row(ln_mix_g[i]), row(ln_mix_b[i])
        h1, route, cnt = _outproj(att, w_out, h, g1, b1, wr, br, alpha=alpha)
        hm1, route_m, _ = _outproj(att_m, w_out, hm, g1, b1, wr, br, alpha=alpha)

        ep = {"wgu": w_gate_up[i].astype(BF16), "bgu": b_gate_up[i][:, None, :],
              "wd": w_down[i].astype(BF16), "bd": b_down[i][:, None, :]}
        h, hm = _moe(h1, route, cnt, hm1, route_m, ep, row(ln_ffn_g[i]), row(ln_ffn_b[i]), alpha=alpha)
    return h.reshape(batch, seq, d)
```

```python
import functools

import jax
import jax.numpy as jnp
import numpy as np
from jax import lax
from jax.experimental import pallas as pl
from jax.experimental.pallas import tpu as pltpu

N_META = 16
HEAD_DIM = 64
N_HEADS = 16
N_KV_HEADS = 2
GROUP = N_HEADS // N_KV_HEADS
Q_COLS = N_HEADS * HEAD_DIM
KV_COLS = N_KV_HEADS * HEAD_DIM
ROT_DIM = HEAD_DIM // 4
ROPE_THETA = 500000.0
ATTN_SCALE = HEAD_DIM ** -0.5
BLOCK = 128
IDX_HEADS = 8
IDX_DIM = 64
IDX_ROT_DIM = IDX_DIM // 2
IDX_SCALE = IDX_DIM ** -0.5
IDX_W_SCALE = IDX_HEADS ** -0.5
TOPK_MAX = 256
N_EXPERTS = 32
TOP_K = 4
SWIGLU_LIMIT = 7.0
SWIGLU_ALPHA = 1.702
LN_EPS = 1e-5
NEG = -1e30
INT_MIN = -2147483648

LANES = 128
ROW_TILE = 512
MOE_ROWS = 512
FF_CHUNK = 256
IDX_CHUNK_BLOCKS = 4
ATT_CHUNK_BLOCKS = 2
V_ROWS = HEAD_DIM + 16
COL_TILE = 256
HEADS_PER_TILE = COL_TILE // BLOCK
N_COL_TILES = GROUP // HEADS_PER_TILE
PIPE = 8
VMEM_LIMIT = 48 * 1024 * 1024

F32 = jnp.float32
BF16 = jnp.bfloat16


def _cparams(*sem):
    return pltpu.CompilerParams(dimension_semantics=sem, vmem_limit_bytes=VMEM_LIMIT)


def _dot(a, b):
    return jnp.dot(a, b, preferred_element_type=F32)


def _dot_nt(a, b):
    return lax.dot_general(a, b, (((1,), (1,)), ((), ())), preferred_element_type=F32)


def _dot_tn(a, b):
    return lax.dot_general(a, b, (((0,), (0,)), ((), ())), preferred_element_type=F32)


def _layer_norm(z, g, b):
    mu = jnp.mean(z, axis=-1, keepdims=True)
    d = z - mu
    var = jnp.mean(d * d, axis=-1, keepdims=True)
    return d * lax.rsqrt(var + LN_EPS) * g + b


def _rope128(blk, tab_ref, half):
    return (blk * tab_ref[0] + pltpu.roll(blk, half, 1) * tab_ref[1]
            + pltpu.roll(blk, LANES - half, 1) * tab_ref[2])


def _inproj_kernel(*refs, dsa):
    if dsa:
        (h_ref, wq_ref, wk_ref, wvt_ref, wqi_ref, wki_ref, wwt_ref, bq_ref, bk_ref, bvt_ref,
         bqi_ref, bki_ref, bwt_ref, tabq_ref, tabi_ref, g_ref, bn_ref,
         q_ref, k_ref, vt_ref, qi_ref, ki_ref, wt_ref) = refs
    else:
        h_ref, wq_ref, wk_ref, wvt_ref, bq_ref, bk_ref, bvt_ref, tabq_ref, q_ref, k_ref, vt_ref = refs
    x = h_ref[...].astype(BF16)
    q = _dot(x, wq_ref[...]) + bq_ref[...]
    for j in range(Q_COLS // LANES):
        sl = slice(j * LANES, (j + 1) * LANES)
        q_ref[:, sl] = _rope128(q[:, sl], tabq_ref, ROT_DIM // 2).astype(BF16)
    k = _dot(x, wk_ref[...]) + bk_ref[...]
    k_ref[...] = _rope128(k, tabq_ref, ROT_DIM // 2).astype(BF16)
    vt = _dot_nt(wvt_ref[...], x) + bvt_ref[...]
    for j in range(vt_ref.shape[0]):
        vt_ref[j] = vt[:, j * BLOCK:(j + 1) * BLOCK].astype(BF16)
    if not dsa:
        return
    qi = _dot(x, wqi_ref[...]) + bqi_ref[...]
    for j in range(IDX_HEADS * IDX_DIM // LANES):
        sl = slice(j * LANES, (j + 1) * LANES)
        qi_ref[:, sl] = _rope128(qi[:, sl], tabi_ref, IDX_ROT_DIM // 2).astype(BF16)
    ki = _dot(x, wki_ref[...]) + bki_ref[...]
    real = lax.broadcasted_iota(jnp.int32, ki.shape, 1) < IDX_DIM
    mu = jnp.sum(jnp.where(real, ki, 0.0), axis=-1, keepdims=True) * (1.0 / IDX_DIM)
    d = jnp.where(real, ki - mu, 0.0)
    var = jnp.sum(d * d, axis=-1, keepdims=True) * (1.0 / IDX_DIM)
    kn = d * lax.rsqrt(var + LN_EPS) * g_ref[...] + bn_ref[...]
    ki_ref[...] = _rope128(kn, tabi_ref, IDX_ROT_DIM // 2)[:, :IDX_DIM].astype(BF16)
    wt_ref[...] = (_dot_nt(wwt_ref[...], x) + bwt_ref[...]) * IDX_W_SCALE


def _rope_tables(pos, rot_dim):
    half = rot_dim // 2
    inv_freq = ROPE_THETA ** (-jnp.arange(half, dtype=F32) / half)
    ang = pos.astype(F32)[:, None] * inv_freq[None, :]
    cos, sin = jnp.cos(ang), jnp.sin(ang)
    n = pos.shape[0]
    rest = HEAD_DIM - rot_dim
    c = jnp.concatenate([cos, cos, jnp.ones((n, rest), F32)], 1)
    s_up = jnp.concatenate([jnp.zeros((n, half), F32), sin, jnp.zeros((n, rest), F32)], 1)
    s_lo = jnp.concatenate([-sin, jnp.zeros((n, HEAD_DIM - half), F32)], 1)
    return jnp.stack([jnp.tile(t, (1, LANES // HEAD_DIM)) for t in (c, s_up, s_lo)])


def _inproj(h, wp, tabq, tabi, *, dsa, seq):
    m, d = h.shape
    tm = min(ROW_TILE, m)
    nsb = seq // tm
    row = lambda i: (i, 0)
    const = lambda i: (0, 0)
    tab_map = lambda i: (0, i % nsb, 0)
    full = lambda a: pl.BlockSpec(a.shape, const)
    names = ("wq", "wk", "wvt", "wqi", "wki", "wwt", "bq", "bk", "bvt", "bqi", "bki", "bwt") if dsa \
        else ("wq", "wk", "wvt", "bq", "bk", "bvt")
    ins = [h] + [wp[k] for k in names] + [tabq]
    in_specs = [pl.BlockSpec((tm, d), row)] + [full(wp[k]) for k in names] \
        + [pl.BlockSpec((3, tm, LANES), tab_map)]
    out_shape = [jax.ShapeDtypeStruct((m, Q_COLS), BF16), jax.ShapeDtypeStruct((m, KV_COLS), BF16),
                 jax.ShapeDtypeStruct((m // BLOCK, KV_COLS, BLOCK), BF16)]
    out_specs = [pl.BlockSpec((tm, Q_COLS), row), pl.BlockSpec((tm, KV_COLS), row),
                 pl.BlockSpec((tm // BLOCK, KV_COLS, BLOCK), lambda i: (i, 0, 0))]
    if dsa:
        ins += [tabi, wp["g"], wp["bn"]]
        in_specs += [pl.BlockSpec((3, tm, LANES), tab_map), full(wp["g"]), full(wp["bn"])]
        out_shape += [jax.ShapeDtypeStruct((m, IDX_HEADS * IDX_DIM), BF16),
                      jax.ShapeDtypeStruct((m, IDX_DIM), BF16),
                      jax.ShapeDtypeStruct((IDX_HEADS, m), F32)]
        out_specs += [pl.BlockSpec((tm, IDX_HEADS * IDX_DIM), row),
                      pl.BlockSpec((tm, IDX_DIM), row),
                      pl.BlockSpec((IDX_HEADS, tm), lambda i: (0, i))]
    return pl.pallas_call(
        functools.partial(_inproj_kernel, dsa=dsa), grid=(m // tm,), in_specs=in_specs,
        out_specs=out_specs, out_shape=out_shape, compiler_params=_cparams("parallel"),
        name="inproj_dsa" if dsa else "inproj_swa")(*ins)


def _stack_heads(ref, first, count, width=HEAD_DIM, scale=None):
    parts = [ref[:, (first + g) * width:(first + g + 1) * width] for g in range(count)]
    x = jnp.concatenate(parts, axis=0)
    return x if scale is None else x * scale


def _store_tile_t(o_ref, first_head, acc):
    out_t = acc[:HEAD_DIM] / acc[HEAD_DIM:HEAD_DIM + 1]
    for g in range(HEADS_PER_TILE):
        h = first_head + g
        o_ref[h * HEAD_DIM:(h + 1) * HEAD_DIM, :] = out_t[:, g * BLOCK:(g + 1) * BLOCK].astype(BF16)


def _values_with_ones(vt, kh):
    hs = slice(kh * HEAD_DIM, (kh + 1) * HEAD_DIM)
    return jnp.concatenate([vt[hs, :], jnp.ones((V_ROWS - HEAD_DIM, vt.shape[1]), BF16)], axis=0)


def _mask_bias(sel):
    bias = jnp.where(sel, 0.0, NEG).astype(BF16)
    return jnp.concatenate([bias] * HEADS_PER_TILE, axis=1)


def _local_attn_kernel(sink_ref, q_ref, kc_ref, kp_ref, km_ref, vtc_ref, vtp_ref, vtm_ref, o_ref,
                       *, banded):
    n = pl.program_id(1)
    r = lax.broadcasted_iota(jnp.int32, (BLOCK, BLOCK), 0)
    c = lax.broadcasted_iota(jnp.int32, (BLOCK, BLOCK), 1)
    blocks = [(kc_ref, vtc_ref, _mask_bias(r <= c))]
    if banded:
        blocks.append((kp_ref, vtp_ref, _mask_bias((r > c) & (n > 0))))
        blocks.append((km_ref, vtm_ref, _mask_bias(r < N_META)))
    sink_row = lax.broadcasted_iota(jnp.int32, (V_ROWS, COL_TILE), 0) == HEAD_DIM
    tiles = [(kh, kh * GROUP + ct * HEADS_PER_TILE)
             for kh in range(N_KV_HEADS) for ct in range(N_COL_TILES)]
    ks = [[k_ref[:, kh * HEAD_DIM:(kh + 1) * HEAD_DIM] for k_ref, _, _ in blocks]
          for kh in range(N_KV_HEADS)]
    vs = [[_values_with_ones(vt_ref[...], kh) for _, vt_ref, _ in blocks] for kh in range(N_KV_HEADS)]

    def logits(i):
        kh, first = tiles[i]
        qs = _stack_heads(q_ref, first, HEADS_PER_TILE, scale=ATTN_SCALE)
        return [_dot_nt(k, qs) for k in ks[kh]]

    sss = [logits(i) for i in range(PIPE)]
    for i, (kh, first) in enumerate(tiles):
        sink = jnp.concatenate(
            [jnp.full((1, BLOCK), sink_ref[first + g], F32) for g in range(HEADS_PER_TILE)], axis=1)
        ss = [s.astype(BF16) + bias for s, (_, _, bias) in zip(sss[i], blocks)]
        m = sink
        for s in ss:
            m = jnp.maximum(m, jnp.max(s, axis=0, keepdims=True).astype(F32))
        mb = m.astype(BF16)
        ps = [jnp.exp(s - mb) for s in ss]
        if i + PIPE < len(tiles):
            sss.append(logits(i + PIPE))
        acc = jnp.where(sink_row, jnp.exp(sink - mb.astype(F32)), 0.0)
        for p, v in zip(ps, vs[kh]):
            acc = acc + _dot(v, p)
        _store_tile_t(o_ref, first, acc)


def _local_attn(sinks, q, k, vt3, km, vtm, *, banded, batch):
    m = q.shape[0]
    nb = m // BLOCK // batch
    cur = lambda b, n, s: (b * nb + n, 0)
    prev = lambda b, n, s: (b * nb + jnp.maximum(n - 1, 0), 0)
    cur3 = lambda b, n, s: (b * nb + n, 0, 0)
    prev3 = lambda b, n, s: (b * nb + jnp.maximum(n - 1, 0), 0, 0)
    grid_spec = pltpu.PrefetchScalarGridSpec(
        num_scalar_prefetch=1, grid=(batch, nb),
        in_specs=[pl.BlockSpec((BLOCK, Q_COLS), cur), pl.BlockSpec((BLOCK, KV_COLS), cur),
                  pl.BlockSpec((BLOCK, KV_COLS), prev),
                  pl.BlockSpec((BLOCK, KV_COLS), lambda b, n, s: (0, 0)),
                  pl.BlockSpec((None, KV_COLS, BLOCK), cur3), pl.BlockSpec((None, KV_COLS, BLOCK), prev3),
                  pl.BlockSpec((None, KV_COLS, BLOCK), lambda b, n, s: (0, 0, 0))],
        out_specs=pl.BlockSpec((None, Q_COLS, BLOCK), cur3))
    return pl.pallas_call(
        functools.partial(_local_attn_kernel, banded=banded), grid_spec=grid_spec,
        out_shape=jax.ShapeDtypeStruct((m // BLOCK, Q_COLS, BLOCK), BF16),
        compiler_params=_cparams("parallel", "parallel"),
        name="swa_attn" if banded else "meta_attn")(sinks, q, k, k, km, vt3, vt3, vtm)


def _sortable(x):
    bits = lax.bitcast_convert_type(x, jnp.int32)
    return jnp.where(bits < 0, bits ^ jnp.int32(0x7FFFFFFF), bits)


def _dsa_kernel(q_ref, qi_ref, wt_ref, k3_ref, vt3_ref, ki3_ref, km_ref, vtm_ref, o_ref,
                key_s, acc_s, *, ksel, idx_bits):
    n = pl.program_id(1)
    cb = IDX_CHUNK_BLOCKS
    ck = cb * BLOCK
    n_chunks = n // cb + 1
    lane = lax.broadcasted_iota(jnp.int32, (1, BLOCK), 1)
    t_row = n * BLOCK + lane

    qis = _stack_heads(qi_ref, 0, IDX_HEADS, IDX_DIM)
    wts = [wt_ref[h:h + 1, :] for h in range(IDX_HEADS)]

    def score_chunk(c, carry):
        kic = ki3_ref[pl.ds(c * cb, cb)].reshape(ck, IDX_DIM)
        d = _dot_nt(kic, qis)
        sc = jnp.zeros((ck, BLOCK), F32)
        for h in range(IDX_HEADS):
            sc = sc + jnp.maximum(d[:, h * BLOCK:(h + 1) * BLOCK], 0.0) * wts[h]
        sc = sc * IDX_SCALE
        s_idx = c * ck + lax.broadcasted_iota(jnp.int32, (ck, BLOCK), 0)
        key = jnp.where(s_idx <= t_row, _sortable(sc), INT_MIN)
        key_s[pl.ds(c * cb, cb)] = key.reshape(cb, BLOCK, BLOCK)
        return carry

    lax.fori_loop(0, n_chunks, score_chunk, 0)

    def count(pred):
        def body(c, acc):
            keys = key_s[pl.ds(c * cb, cb)].reshape(ck, BLOCK)
            s_idx = c * ck + lax.broadcasted_iota(jnp.int32, (ck, BLOCK), 0)
            hit = jnp.where(pred(keys, s_idx), 1, 0).astype(jnp.int32)
            return acc + jnp.sum(hit.reshape(ck // 8, 8, BLOCK), axis=0)
        acc = lax.fori_loop(0, n_chunks, body, jnp.zeros((8, BLOCK), jnp.int32))
        return jnp.sum(acc, axis=0, keepdims=True)

    def thr_bit(i, tu):
        cand_u = tu | lax.shift_left(jnp.int32(1), 31 - i)
        cand = cand_u ^ jnp.int32(INT_MIN)
        return jnp.where(count(lambda k, s: k >= cand) >= ksel, cand_u, tu)

    thr_u = lax.fori_loop(0, 32, thr_bit, jnp.zeros((1, BLOCK), jnp.int32))
    thr = thr_u ^ jnp.int32(INT_MIN)
    n_gt = count(lambda k, s: k > thr)
    n_ge = count(lambda k, s: k >= thr)
    need = ksel - n_gt

    def tie_bit(i, x):
        cand = x | lax.shift_left(jnp.int32(1), idx_bits - 1 - i)
        return jnp.where(count(lambda k, s: (k == thr) & (s < cand)) <= need - 1, cand, x)

    def tie_search():
        return lax.fori_loop(0, idx_bits, tie_bit, jnp.zeros((1, BLOCK), jnp.int32))

    excess = jnp.max(jnp.where((n_ge > ksel) & (thr_u != 0), 1, 0))
    tie_x = lax.cond(excess > 0, tie_search,
                     lambda: jnp.full((1, BLOCK), (1 << idx_bits) - 1, jnp.int32))

    tiles = [(kh, kh * GROUP + ct * HEADS_PER_TILE)
             for kh in range(N_KV_HEADS) for ct in range(N_COL_TILES)]
    qs = [_stack_heads(q_ref, first, HEADS_PER_TILE, scale=ATTN_SCALE) for _, first in tiles]
    krow = lax.broadcasted_iota(jnp.int32, (BLOCK, BLOCK), 0)
    bias_m = _mask_bias(krow < N_META)
    m0 = []
    for i, (kh, _) in enumerate(tiles):
        hs = slice(kh * HEAD_DIM, (kh + 1) * HEAD_DIM)
        s = _dot_nt(km_ref[:, hs], qs[i]).astype(BF16) + bias_m
        m = jnp.max(s, axis=0, keepdims=True)
        m0.append(m)
        acc_s[i] = _dot(_values_with_ones(vtm_ref[...], kh), jnp.exp(s - m))

    ab = ATT_CHUNK_BLOCKS
    ak = ab * BLOCK
    arow = lax.broadcasted_iota(jnp.int32, (ak, BLOCK), 0)

    def attend(c, ms):
        keys = key_s[pl.ds(c * ab, ab)].reshape(ak, BLOCK)
        s_idx = c * ak + arow
        sel = ((keys > thr) | ((keys == thr) & (s_idx <= tie_x))) & (s_idx <= t_row)
        bias = _mask_bias(sel)
        kc = k3_ref[pl.ds(c * ab, ab)].reshape(ak, KV_COLS)
        vtc = jnp.concatenate([vt3_ref[c * ab + g] for g in range(ab)], axis=1)
        vs = [_values_with_ones(vtc, kh) for kh in range(N_KV_HEADS)]
        ks = [kc[:, kh * HEAD_DIM:(kh + 1) * HEAD_DIM] for kh in range(N_KV_HEADS)]
        logits = lambda i: _dot_nt(ks[tiles[i][0]], qs[i])
        ss = [logits(i) for i in range(PIPE)]
        new_m = []
        for i, (kh, _) in enumerate(tiles):
            s = ss[i].astype(BF16) + bias
            m = jnp.maximum(ms[i], jnp.max(s, axis=0, keepdims=True))
            alpha = jnp.exp(ms[i].astype(F32) - m.astype(F32))
            p = jnp.exp(s - m)
            new_m.append(m)
            if i + PIPE < len(tiles):
                ss.append(logits(i + PIPE))
            acc_s[i] = acc_s[i] * alpha + _dot(vs[kh], p)
        return tuple(new_m)

    lax.fori_loop(0, n // ab + 1, attend, tuple(m0))
    for i, (_, first) in enumerate(tiles):
        _store_tile_t(o_ref, first, acc_s[i])


def _dsa_attn(q, qi, wt, k3, vt3, ki3, km, vtm, *, batch):
    m = q.shape[0]
    nb = m // BLOCK // batch
    seq = nb * BLOCK
    assert nb % IDX_CHUNK_BLOCKS == 0 and IDX_CHUNK_BLOCKS % ATT_CHUNK_BLOCKS == 0
    ksel = min(TOPK_MAX, seq // 4)
    idx_bits = max(1, int(np.ceil(np.log2(seq))))
    qmap = lambda b, n: (b * nb + n, 0)
    bmap = lambda b, n: (b, 0, 0)
    return pl.pallas_call(
        functools.partial(_dsa_kernel, ksel=ksel, idx_bits=idx_bits), grid=(batch, nb),
        in_specs=[pl.BlockSpec((BLOCK, Q_COLS), qmap),
                  pl.BlockSpec((BLOCK, IDX_HEADS * IDX_DIM), qmap),
                  pl.BlockSpec((IDX_HEADS, BLOCK), lambda b, n: (0, b * nb + n)),
                  pl.BlockSpec((nb, BLOCK, KV_COLS), bmap),
                  pl.BlockSpec((nb, KV_COLS, BLOCK), bmap),
                  pl.BlockSpec((nb, BLOCK, IDX_DIM), bmap),
                  pl.BlockSpec((BLOCK, KV_COLS), lambda b, n: (0, 0)),
                  pl.BlockSpec((None, KV_COLS, BLOCK), lambda b, n: (0, 0, 0))],
        out_specs=pl.BlockSpec((None, Q_COLS, BLOCK), lambda b, n: (b * nb + n, 0, 0)),
        out_shape=jax.ShapeDtypeStruct((m // BLOCK, Q_COLS, BLOCK), BF16),
        scratch_shapes=[pltpu.VMEM((nb, BLOCK, BLOCK), jnp.int32),
                        pltpu.VMEM((N_KV_HEADS * N_COL_TILES, V_ROWS, COL_TILE), F32)],
        compiler_params=_cparams("parallel", "arbitrary"), name="dsa_attn",
    )(q, qi, wt, k3, vt3, ki3, km, vtm)


def _outproj_kernel(a_ref, w_ref, h_ref, g_ref, b_ref, wr_ref, br_ref, tri_ref,
                    h1_ref, route_ref, cnt_ref, carry_s, *, alpha):
    i = pl.program_id(0)

    @pl.when(i == 0)
    def _():
        carry_s[...] = jnp.zeros_like(carry_s)

    y = jnp.concatenate([_dot_tn(a_ref[j], w_ref[...]) for j in range(a_ref.shape[0])], axis=0)
    h1 = _layer_norm(alpha * h_ref[...] + y, g_ref[...], b_ref[...])
    h1_ref[...] = h1
    logits = _dot(h1.astype(BF16), wr_ref[...]) + br_ref[...]
    lane = lax.broadcasted_iota(jnp.int32, logits.shape, 1)
    idxs, vals = [], []
    rem = logits
    for _ in range(TOP_K):
        mx = jnp.max(rem, axis=-1, keepdims=True)
        ix = jnp.min(jnp.where(rem == mx, lane, LANES), axis=-1, keepdims=True)
        idxs.append(ix)
        vals.append(mx)
        rem = jnp.where(lane == ix, -jnp.inf, rem)
    exps = [jnp.exp(v - vals[0]) for v in vals]
    den = exps[0] + exps[1] + exps[2] + exps[3]
    sel = jnp.where(rem == -jnp.inf, 1.0, 0.0)
    before = _dot(tri_ref[...], sel.astype(BF16)) + carry_s[...]
    route = jnp.zeros(logits.shape, F32)
    for k in range(TOP_K):
        rank = jnp.sum(jnp.where(lane == idxs[k], before, 0.0), axis=-1, keepdims=True)
        route = jnp.where(lane == k, idxs[k].astype(F32), route)
        route = jnp.where(lane == TOP_K + k, exps[k] / den, route)
        route = jnp.where(lane == 2 * TOP_K + k, rank, route)
    route_ref[...] = route
    carry_s[...] = carry_s[...] + jnp.sum(sel, axis=0, keepdims=True)
    cnt_ref[...] = carry_s[...]


def _outproj(a_t, w, h, g, b, wr, br, *, alpha):
    m, d = h.shape
    tm = min(ROW_TILE, m)
    tri = jnp.tril(jnp.ones((tm, tm), BF16), -1)
    row = lambda i: (i, 0)
    const = lambda i: (0, 0)
    full = lambda x: pl.BlockSpec(x.shape, const)
    return pl.pallas_call(
        functools.partial(_outproj_kernel, alpha=alpha), grid=(m // tm,),
        in_specs=[pl.BlockSpec((tm // BLOCK, Q_COLS, BLOCK), lambda i: (i, 0, 0)), full(w),
                  pl.BlockSpec((tm, d), row), full(g), full(b), full(wr), full(br), full(tri)],
        out_specs=[pl.BlockSpec((tm, d), row), pl.BlockSpec((tm, LANES), row),
                   pl.BlockSpec((1, LANES), const)],
        out_shape=[jax.ShapeDtypeStruct((m, d), F32), jax.ShapeDtypeStruct((m, LANES), F32),
                   jax.ShapeDtypeStruct((1, LANES), F32)],
        scratch_shapes=[pltpu.VMEM((1, LANES), F32)],
        compiler_params=_cparams("arbitrary"), name="outproj_router")(a_t, w, h, g, b, wr, br, tri)


def _scatter_kernel(dest_ref, h_ref, xs_in, xs_out, sem):
    del xs_in
    tm = h_ref.shape[0]

    def row_copy(r, d):
        return pltpu.make_async_copy(h_ref.at[pl.ds(r, 1)], xs_out.at[pl.ds(d, 1)], sem)

    def issue(r, c):
        for k in range(TOP_K):
            row_copy(r, dest_ref[0, 0, r * TOP_K + k]).start()
        return c

    def drain(r, c):
        for k in range(TOP_K):
            row_copy(r, dest_ref[0, 0, r * TOP_K + k]).wait()
        return c

    lax.fori_loop(0, tm, issue, 0)
    lax.fori_loop(0, tm, drain, 0)


def _scatter_rows(dest3, h, n_rows):
    m, d = h.shape
    tm = dest3.shape[-1] // TOP_K
    xs = jnp.zeros((n_rows, d), F32)
    return pl.pallas_call(
        _scatter_kernel, grid=(m // tm,),
        in_specs=[pl.BlockSpec((1, 1, tm * TOP_K), lambda i: (i, 0, 0), memory_space=pltpu.SMEM),
                  pl.BlockSpec((tm, d), lambda i: (i, 0)),
                  pl.BlockSpec(memory_space=pl.ANY)],
        out_specs=pl.BlockSpec(memory_space=pl.ANY),
        out_shape=jax.ShapeDtypeStruct((n_rows, d), F32),
        scratch_shapes=[pltpu.SemaphoreType.DMA],
        input_output_aliases={2: 0},
        compiler_params=_cparams("arbitrary"), name="moe_scatter")(dest3, h, xs)


def _expert_ffn(x, wgu_ref, bgu_ref, wd_ref, bd_ref):
    ff = wd_ref.shape[0]
    acc = jnp.zeros((x.shape[0], wd_ref.shape[1]), F32)
    for c in range(ff // FF_CHUNK):
        gs = slice(c * FF_CHUNK, (c + 1) * FF_CHUNK)
        us = slice(ff + c * FF_CHUNK, ff + (c + 1) * FF_CHUNK)
        gate = jnp.minimum(_dot(x, wgu_ref[:, gs]) + bgu_ref[:, gs], SWIGLU_LIMIT)
        up = jnp.clip(_dot(x, wgu_ref[:, us]) + bgu_ref[:, us], -SWIGLU_LIMIT, SWIGLU_LIMIT)
        act = gate * jax.nn.sigmoid(SWIGLU_ALPHA * gate) * (up + 1.0)
        acc = acc + _dot(act.astype(BF16), wd_ref[gs, :])
    return acc + bd_ref[...]


def _gmm_kernel(be_ref, nu_ref, xs_ref, wgu_ref, bgu_ref, wd_ref, bd_ref, y_ref):
    del be_ref
    i = pl.program_id(0)

    @pl.when(i < nu_ref[0])
    def _():
        y_ref[...] = _expert_ffn(xs_ref[...].astype(BF16), wgu_ref, bgu_ref, wd_ref, bd_ref)

    @pl.when(i >= nu_ref[0])
    def _():
        y_ref[...] = jnp.zeros_like(y_ref)


def _gmm(blk_e, n_used, xs, wgu, bgu, wd, bd):
    p, d = xs.shape
    ff = wd.shape[1]
    emap = lambda i, be, nu: (be[i], 0, 0)
    grid_spec = pltpu.PrefetchScalarGridSpec(
        num_scalar_prefetch=2, grid=(p // MOE_ROWS,),
        in_specs=[pl.BlockSpec((MOE_ROWS, d), lambda i, be, nu: (i, 0)),
                  pl.BlockSpec((None, d, 2 * ff), emap), pl.BlockSpec((None, 1, 2 * ff), emap),
                  pl.BlockSpec((None, ff, d), emap), pl.BlockSpec((None, 1, d), emap)],
        out_specs=pl.BlockSpec((MOE_ROWS, d), lambda i, be, nu: (i, 0)))
    return pl.pallas_call(
        _gmm_kernel, grid_spec=grid_spec, out_shape=jax.ShapeDtypeStruct((p, d), F32),
        compiler_params=_cparams("arbitrary"), name="moe_experts")(blk_e, n_used, xs, wgu, bgu, wd, bd)


def _combine_kernel(dest_ref, route_ref, h_ref, g_ref, b_ref, ys_ref, o_ref, ybuf, sem, *, alpha):
    tm = h_ref.shape[0]

    def row_copy(r, k):
        d = dest_ref[0, 0, r * TOP_K + k]
        return pltpu.make_async_copy(ys_ref.at[pl.ds(d, 1)], ybuf.at[k, pl.ds(r, 1)], sem)

    def issue(r, c):
        for k in range(TOP_K):
            row_copy(r, k).start()
        return c

    def drain(r, c):
        for k in range(TOP_K):
            row_copy(r, k).wait()
        return c

    lax.fori_loop(0, tm, issue, 0)
    lax.fori_loop(0, tm, drain, 0)
    z = alpha * h_ref[...]
    for k in range(TOP_K):
        z = z + route_ref[:, TOP_K + k:TOP_K + k + 1] * ybuf[k]
    o_ref[...] = _layer_norm(z, g_ref[...], b_ref[...])


def _combine(dest3, route, h, g, b, ys, *, alpha):
    m, d = h.shape
    tm = dest3.shape[-1] // TOP_K
    row = lambda i: (i, 0)
    const = lambda i: (0, 0)
    return pl.pallas_call(
        functools.partial(_combine_kernel, alpha=alpha), grid=(m // tm,),
        in_specs=[pl.BlockSpec((1, 1, tm * TOP_K), lambda i: (i, 0, 0), memory_space=pltpu.SMEM),
                  pl.BlockSpec((tm, LANES), row), pl.BlockSpec((tm, d), row),
                  pl.BlockSpec(g.shape, const), pl.BlockSpec(b.shape, const),
                  pl.BlockSpec(memory_space=pl.ANY)],
        out_specs=pl.BlockSpec((tm, d), row),
        out_shape=jax.ShapeDtypeStruct((m, d), F32),
        scratch_shapes=[pltpu.VMEM((TOP_K, tm, d), F32), pltpu.SemaphoreType.DMA],
        compiler_params=_cparams("arbitrary"), name="moe_combine")(dest3, route, h, g, b, ys)


def _meta_moe_kernel(hm_ref, gm_ref, wgu_ref, bgu_ref, wd_ref, bd_ref, g_ref, b_ref, o_ref, acc_s,
                     *, alpha):
    e = pl.program_id(0)

    @pl.when(e == 0)
    def _():
        acc_s[...] = jnp.zeros_like(acc_s)

    y = _expert_ffn(hm_ref[...].astype(BF16), wgu_ref, bgu_ref, wd_ref, bd_ref)
    lane = lax.broadcasted_iota(jnp.int32, gm_ref.shape, 1)
    ge = jnp.sum(jnp.where(lane == e, gm_ref[...], 0.0), axis=-1, keepdims=True)
    acc_s[...] = acc_s[...] + ge * y

    @pl.when(e == pl.num_programs(0) - 1)
    def _():
        o_ref[...] = _layer_norm(alpha * hm_ref[...] + acc_s[...], g_ref[...], b_ref[...])


def _meta_moe(hm, gm, wgu, bgu, wd, bd, g, b, *, alpha):
    m, d = hm.shape
    ff = wd.shape[1]
    const = lambda e: (0, 0)
    emap = lambda e: (e, 0, 0)
    return pl.pallas_call(
        functools.partial(_meta_moe_kernel, alpha=alpha), grid=(wgu.shape[0],),
        in_specs=[pl.BlockSpec((m, d), const), pl.BlockSpec((m, LANES), const),
                  pl.BlockSpec((None, d, 2 * ff), emap), pl.BlockSpec((None, 1, 2 * ff), emap),
                  pl.BlockSpec((None, ff, d), emap), pl.BlockSpec((None, 1, d), emap),
                  pl.BlockSpec(g.shape, const), pl.BlockSpec(b.shape, const)],
        out_specs=pl.BlockSpec((m, d), const),
        out_shape=jax.ShapeDtypeStruct((m, d), F32),
        scratch_shapes=[pltpu.VMEM((m, d), F32)],
        compiler_params=_cparams("arbitrary"), name="meta_experts")(hm, gm, wgu, bgu, wd, bd, g, b)


def _moe(h1, route, cnt, hm1, route_m, ep, g, b, *, alpha):
    m, d = h1.shape
    n_blk = -(-m * TOP_K // MOE_ROWS) + N_EXPERTS
    e = route[:, :TOP_K].astype(jnp.int32)
    rank = route[:, 2 * TOP_K:3 * TOP_K].astype(jnp.int32)
    counts = cnt[0, :N_EXPERTS].astype(jnp.int32)
    padded = (counts + MOE_ROWS - 1) // MOE_ROWS * MOE_ROWS
    pend = jnp.cumsum(padded)
    dest = (pend - padded)[e] + rank
    blk_start = jnp.arange(n_blk, dtype=jnp.int32) * MOE_ROWS
    blk_e = jnp.minimum(jnp.sum((pend[None, :] <= blk_start[:, None]).astype(jnp.int32), axis=1),
                        N_EXPERTS - 1)
    n_used = (pend[-1:] // MOE_ROWS).astype(jnp.int32)
    tm = min(ROW_TILE // 2, m)
    dest3 = dest.reshape(m // tm, 1, tm * TOP_K)
    xs = _scatter_rows(dest3, h1, n_blk * MOE_ROWS)
    ys = _gmm(blk_e, n_used, xs, ep["wgu"], ep["bgu"], ep["wd"], ep["bd"])
    h2 = _combine(dest3, route, h1, g, b, ys, alpha=alpha)
    em = route_m[:, :TOP_K].astype(jnp.int32)
    gm = jnp.sum(jax.nn.one_hot(em, LANES, dtype=F32) * route_m[:, TOP_K:2 * TOP_K, None], axis=1)
    hm2 = _meta_moe(hm1, gm, ep["wgu"], ep["bgu"], ep["wd"], ep["bd"], g, b, alpha=alpha)
    return h2, hm2


def _split_in_weights(w, b, dsa, g=None, bn=None):
    cuts = np.cumsum([Q_COLS, KV_COLS, KV_COLS, IDX_HEADS * IDX_DIM, IDX_DIM]).tolist()
    wb = w.astype(BF16)
    row = lambda v: v.reshape(1, -1)
    col = lambda v: v.reshape(-1, 1)
    wp = {"wq": wb[:, :cuts[0]], "bq": row(b[:cuts[0]]),
          "wk": wb[:, cuts[0]:cuts[1]], "bk": row(b[cuts[0]:cuts[1]]),
          "wvt": wb[:, cuts[1]:cuts[2]].T, "bvt": col(b[cuts[1]:cuts[2]])}
    if dsa:
        pad = LANES - IDX_DIM
        wp.update({
            "wqi": wb[:, cuts[2]:cuts[3]], "bqi": row(b[cuts[2]:cuts[3]]),
            "wki": jnp.pad(wb[:, cuts[3]:cuts[4]], ((0, 0), (0, pad))),
            "bki": row(jnp.pad(b[cuts[3]:cuts[4]], (0, pad))),
            "wwt": wb[:, cuts[4]:].T, "bwt": col(b[cuts[4]:]),
            "g": row(jnp.pad(g, (0, pad))), "bn": row(jnp.pad(bn, (0, pad)))})
    return wp


def kernel(x, meta_tokens, w_in_a, b_in_a, sinks_a, w_out_a, w_in_b, b_in_b, idx_k_norm_g, idx_k_norm_b, w_out_b, ln_mix_g, ln_mix_b, w_router, b_router, w_gate_up, b_gate_up, w_down, b_down, ln_ffn_g, ln_ffn_b):
    batch, seq, d = x.shape
    depth = ln_mix_g.shape[0]
    alpha = float((2 * depth) ** 0.25)
    row = lambda v: v.reshape(1, -1)
    nb = seq // BLOCK

    h = x.reshape(batch * seq, d)
    hm = jnp.zeros((BLOCK, d), F32).at[:N_META].set(meta_tokens.astype(F32))
    tab_q = _rope_tables(jnp.arange(N_META, N_META + seq), ROT_DIM)
    tab_i = _rope_tables(jnp.arange(N_META, N_META + seq), IDX_ROT_DIM)
    tab_qm = _rope_tables(jnp.arange(BLOCK), ROT_DIM)
    tab_im = _rope_tables(jnp.arange(BLOCK), IDX_ROT_DIM)
    no_sinks = jnp.full((N_HEADS,), NEG, F32)

    for i in range(depth):
        j = i // 2
        dsa = i % 2 == 1
        if dsa:
            wp = _split_in_weights(w_in_b[j], b_in_b[j], True, idx_k_norm_g[j], idx_k_norm_b[j])
            w_out, sinks = w_out_b[j], no_sinks
        else:
            wp = _split_in_weights(w_in_a[j], b_in_a[j], False)
            w_out, sinks = w_out_a[j], sinks_a[j].astype(F32)
        pr = _inproj(h, wp, tab_q, tab_i, dsa=dsa, seq=seq)
        pm = _inproj(hm, wp, tab_qm, tab_im, dsa=dsa, seq=BLOCK)
        q, k, vt3 = pr[:3]
        qm, km, vtm = pm[:3]
        if dsa:
            qi, ki, wt = pr[3:]
            att = _dsa_attn(q, qi, wt, k.reshape(batch * nb, BLOCK, KV_COLS), vt3,
                            ki.reshape(batch * nb, BLOCK, IDX_DIM), km, vtm, batch=batch)
        else:
            att = _local_attn(sinks, q, k, vt3, km, vtm, banded=True, batch=batch)
        att_m = _local_attn(sinks, qm, km, vtm, km, vtm, banded=False, batch=1)

        w_out = w_out.astype(BF16)
        wr = jnp.pad(w_router[i].astype(BF16), ((0, 0), (0, LANES - N_EXPERTS)))
        br = row(jnp.pad(b_router[i].astype(F32), (0, LANES - N_EXPERTS), constant_values=NEG))
        g1, b1 = row(ln_mix_g[i]), row(ln_mix_b[i])
        h1, route, cnt = _outproj(att, w_out, h, g1, b1, wr, br, alpha=alpha)
        hm1, route_m, _ = _outproj(att_m, w_out, hm, g1, b1, wr, br, alpha=alpha)

        ep = {"wgu": w_gate_up[i].astype(BF16), "bgu": b_gate_up[i][:, None, :],
              "wd": w_down[i].astype(BF16), "bd": b_down[i][:, None, :]}
        h, hm = _moe(h1, route, cnt, hm1, route_m, ep, row(ln_ffn_g[i]), row(ln_ffn_b[i]), alpha=alpha)
    return h.reshape(batch, seq, d)
```

```python
import functools

import jax
import jax.numpy as jnp
import numpy as np
from jax import lax
from jax.experimental import pallas as pl
from jax.experimental.pallas import tpu as pltpu

N_META = 16
HEAD_DIM = 64
N_HEADS = 16
N_KV_HEADS = 2
GROUP = N_HEADS // N_KV_HEADS
Q_COLS = N_HEADS * HEAD_DIM
KV_COLS = N_KV_HEADS * HEAD_DIM
ROT_DIM = HEAD_DIM // 4
ROPE_THETA = 500000.0
ATTN_SCALE = HEAD_DIM ** -0.5
BLOCK = 128
IDX_HEADS = 8
IDX_DIM = 64
IDX_ROT_DIM = IDX_DIM // 2
IDX_SCALE = IDX_DIM ** -0.5
IDX_W_SCALE = IDX_HEADS ** -0.5
TOPK_MAX = 256
N_EXPERTS = 32
TOP_K = 4
SWIGLU_LIMIT = 7.0
SWIGLU_ALPHA = 1.702
LN_EPS = 1e-5
NEG = -1e30
INT_MIN = -2147483648

LANES = 128
TOKEN_ROWS = 8
ROW_TILE = 512
MOE_ROWS = 512
FF_CHUNK = 256
DMA_UNROLL = 8
IDX_CHUNK_BLOCKS = 4
ATT_CHUNK_BLOCKS = 2
V_ROWS = HEAD_DIM + 16
COL_TILE = 256
HEADS_PER_TILE = COL_TILE // BLOCK
N_COL_TILES = GROUP // HEADS_PER_TILE
PIPE = 8
VMEM_LIMIT = 48 * 1024 * 1024

F32 = jnp.float32
BF16 = jnp.bfloat16


def _cparams(*sem):
    return pltpu.CompilerParams(dimension_semantics=sem, vmem_limit_bytes=VMEM_LIMIT)


def _dot(a, b):
    return jnp.dot(a, b, preferred_element_type=F32)


def _dot_nt(a, b):
    return lax.dot_general(a, b, (((1,), (1,)), ((), ())), preferred_element_type=F32)


def _dot_tn(a, b):
    return lax.dot_general(a, b, (((0,), (0,)), ((), ())), preferred_element_type=F32)


def _layer_norm(z, g, b):
    mu = jnp.mean(z, axis=-1, keepdims=True)
    d = z - mu
    var = jnp.mean(d * d, axis=-1, keepdims=True)
    return d * lax.rsqrt(var + LN_EPS) * g + b


def _store_token_tiles(ref, x):
    for j in range(TOKEN_ROWS):
        ref[pl.ds(j, x.shape[0], stride=TOKEN_ROWS), :] = x[:, j * LANES:(j + 1) * LANES]


def _load_token_tiles(ref, rows):
    return jnp.concatenate([ref[pl.ds(j, rows, stride=TOKEN_ROWS), :] for j in range(TOKEN_ROWS)], axis=1)


def _rope128(blk, tab_ref, half):
    return (blk * tab_ref[0] + pltpu.roll(blk, half, 1) * tab_ref[1]
            + pltpu.roll(blk, LANES - half, 1) * tab_ref[2])


def _inproj_kernel(*refs, dsa):
    if dsa:
        (h_ref, wq_ref, wk_ref, wvt_ref, wqi_ref, wki_ref, wwt_ref, bq_ref, bk_ref, bvt_ref,
         bqi_ref, bki_ref, bwt_ref, tabq_ref, tabi_ref, g_ref, bn_ref,
         q_ref, k_ref, vt_ref, qi_ref, ki_ref, wt_ref) = refs
    else:
        h_ref, wq_ref, wk_ref, wvt_ref, bq_ref, bk_ref, bvt_ref, tabq_ref, q_ref, k_ref, vt_ref = refs
    x = h_ref[...].astype(BF16)
    q = _dot(x, wq_ref[...]) + bq_ref[...]
    for j in range(Q_COLS // LANES):
        sl = slice(j * LANES, (j + 1) * LANES)
        q_ref[:, sl] = _rope128(q[:, sl], tabq_ref, ROT_DIM // 2).astype(BF16)
    k = _dot(x, wk_ref[...]) + bk_ref[...]
    k_ref[...] = _rope128(k, tabq_ref, ROT_DIM // 2).astype(BF16)
    vt = _dot_nt(wvt_ref[...], x) + bvt_ref[...]
    for j in range(vt_ref.shape[0]):
        vt_ref[j] = vt[:, j * BLOCK:(j + 1) * BLOCK].astype(BF16)
    if not dsa:
        return
    qi = _dot(x, wqi_ref[...]) + bqi_ref[...]
    for j in range(IDX_HEADS * IDX_DIM // LANES):
        sl = slice(j * LANES, (j + 1) * LANES)
        qi_ref[:, sl] = _rope128(qi[:, sl], tabi_ref, IDX_ROT_DIM // 2).astype(BF16)
    ki = _dot(x, wki_ref[...]) + bki_ref[...]
    real = lax.broadcasted_iota(jnp.int32, ki.shape, 1) < IDX_DIM
    mu = jnp.sum(jnp.where(real, ki, 0.0), axis=-1, keepdims=True) * (1.0 / IDX_DIM)
    d = jnp.where(real, ki - mu, 0.0)
    var = jnp.sum(d * d, axis=-1, keepdims=True) * (1.0 / IDX_DIM)
    kn = d * lax.rsqrt(var + LN_EPS) * g_ref[...] + bn_ref[...]
    ki_ref[...] = _rope128(kn, tabi_ref, IDX_ROT_DIM // 2)[:, :IDX_DIM].astype(BF16)
    wt_ref[...] = (_dot_nt(wwt_ref[...], x) + bwt_ref[...]) * IDX_W_SCALE


def _rope_tables(pos, rot_dim):
    half = rot_dim // 2
    inv_freq = ROPE_THETA ** (-jnp.arange(half, dtype=F32) / half)
    ang = pos.astype(F32)[:, None] * inv_freq[None, :]
    cos, sin = jnp.cos(ang), jnp.sin(ang)
    n = pos.shape[0]
    rest = HEAD_DIM - rot_dim
    c = jnp.concatenate([cos, cos, jnp.ones((n, rest), F32)], 1)
    s_up = jnp.concatenate([jnp.zeros((n, half), F32), sin, jnp.zeros((n, rest), F32)], 1)
    s_lo = jnp.concatenate([-sin, jnp.zeros((n, HEAD_DIM - half), F32)], 1)
    return jnp.stack([jnp.tile(t, (1, LANES // HEAD_DIM)) for t in (c, s_up, s_lo)])


def _inproj(h, wp, tabq, tabi, *, dsa, seq):
    m, d = h.shape
    tm = min(ROW_TILE, m)
    nsb = seq // tm
    row = lambda i: (i, 0)
    const = lambda i: (0, 0)
    tab_map = lambda i: (0, i % nsb, 0)
    full = lambda a: pl.BlockSpec(a.shape, const)
    names = ("wq", "wk", "wvt", "wqi", "wki", "wwt", "bq", "bk", "bvt", "bqi", "bki", "bwt") if dsa \
        else ("wq", "wk", "wvt", "bq", "bk", "bvt")
    ins = [h] + [wp[k] for k in names] + [tabq]
    in_specs = [pl.BlockSpec((tm, d), row)] + [full(wp[k]) for k in names] \
        + [pl.BlockSpec((3, tm, LANES), tab_map)]
    out_shape = [jax.ShapeDtypeStruct((m, Q_COLS), BF16), jax.ShapeDtypeStruct((m, KV_COLS), BF16),
                 jax.ShapeDtypeStruct((m // BLOCK, KV_COLS, BLOCK), BF16)]
    out_specs = [pl.BlockSpec((tm, Q_COLS), row), pl.BlockSpec((tm, KV_COLS), row),
                 pl.BlockSpec((tm // BLOCK, KV_COLS, BLOCK), lambda i: (i, 0, 0))]
    if dsa:
        ins += [tabi, wp["g"], wp["bn"]]
        in_specs += [pl.BlockSpec((3, tm, LANES), tab_map), full(wp["g"]), full(wp["bn"])]
        out_shape += [jax.ShapeDtypeStruct((m, IDX_HEADS * IDX_DIM), BF16),
                      jax.ShapeDtypeStruct((m, IDX_DIM), BF16),
                      jax.ShapeDtypeStruct((IDX_HEADS, m), F32)]
        out_specs += [pl.BlockSpec((tm, IDX_HEADS * IDX_DIM), row),
                      pl.BlockSpec((tm, IDX_DIM), row),
                      pl.BlockSpec((IDX_HEADS, tm), lambda i: (0, i))]
    return pl.pallas_call(
        functools.partial(_inproj_kernel, dsa=dsa), grid=(m // tm,), in_specs=in_specs,
        out_specs=out_specs, out_shape=out_shape, compiler_params=_cparams("parallel"),
        name="inproj_dsa" if dsa else "inproj_swa")(*ins)


def _stack_heads(ref, first, count, width=HEAD_DIM, scale=None):
    parts = [ref[:, (first + g) * width:(first + g + 1) * width] for g in range(count)]
    x = jnp.concatenate(parts, axis=0)
    return x if scale is None else x * scale


def _store_tile_t(o_ref, first_head, acc):
    out_t = acc[:HEAD_DIM] / acc[HEAD_DIM:HEAD_DIM + 1]
    for g in range(HEADS_PER_TILE):
        h = first_head + g
        o_ref[h * HEAD_DIM:(h + 1) * HEAD_DIM, :] = out_t[:, g * BLOCK:(g + 1) * BLOCK].astype(BF16)


def _values_with_ones(vt, kh):
    hs = slice(kh * HEAD_DIM, (kh + 1) * HEAD_DIM)
    return jnp.concatenate([vt[hs, :], jnp.ones((V_ROWS - HEAD_DIM, vt.shape[1]), BF16)], axis=0)


def _mask_bias(sel):
    bias = jnp.where(sel, 0.0, NEG).astype(BF16)
    return jnp.concatenate([bias] * HEADS_PER_TILE, axis=1)


def _local_attn_kernel(sink_ref, q_ref, kc_ref, kp_ref, km_ref, vtc_ref, vtp_ref, vtm_ref, o_ref,
                       *, banded):
    n = pl.program_id(1)
    r = lax.broadcasted_iota(jnp.int32, (BLOCK, BLOCK), 0)
    c = lax.broadcasted_iota(jnp.int32, (BLOCK, BLOCK), 1)
    blocks = [(kc_ref, vtc_ref, _mask_bias(r <= c))]
    if banded:
        blocks.append((kp_ref, vtp_ref, _mask_bias((r > c) & (n > 0))))
        blocks.append((km_ref, vtm_ref, _mask_bias(r < N_META)))
    sink_row = lax.broadcasted_iota(jnp.int32, (V_ROWS, COL_TILE), 0) == HEAD_DIM
    tiles = [(kh, kh * GROUP + ct * HEADS_PER_TILE)
             for kh in range(N_KV_HEADS) for ct in range(N_COL_TILES)]
    ks = [[k_ref[:, kh * HEAD_DIM:(kh + 1) * HEAD_DIM] for k_ref, _, _ in blocks]
          for kh in range(N_KV_HEADS)]
    vs = [[_values_with_ones(vt_ref[...], kh) for _, vt_ref, _ in blocks] for kh in range(N_KV_HEADS)]

    def logits(i):
        kh, first = tiles[i]
        qs = _stack_heads(q_ref, first, HEADS_PER_TILE, scale=ATTN_SCALE)
        return [_dot_nt(k, qs) for k in ks[kh]]

    sss = [logits(i) for i in range(PIPE)]
    for i, (kh, first) in enumerate(tiles):
        sink = jnp.concatenate(
            [jnp.full((1, BLOCK), sink_ref[first + g], F32) for g in range(HEADS_PER_TILE)], axis=1)
        ss = [s.astype(BF16) + bias for s, (_, _, bias) in zip(sss[i], blocks)]
        m = sink
        for s in ss:
            m = jnp.maximum(m, jnp.max(s, axis=0, keepdims=True).astype(F32))
        mb = m.astype(BF16)
        ps = [jnp.exp(s - mb) for s in ss]
        if i + PIPE < len(tiles):
            sss.append(logits(i + PIPE))
        acc = jnp.where(sink_row, jnp.exp(sink - mb.astype(F32)), 0.0)
        for p, v in zip(ps, vs[kh]):
            acc = acc + _dot(v, p)
        _store_tile_t(o_ref, first, acc)


def _local_attn(sinks, q, k, vt3, km, vtm, *, banded, batch):
    m = q.shape[0]
    nb = m // BLOCK // batch
    cur = lambda b, n, s: (b * nb + n, 0)
    prev = lambda b, n, s: (b * nb + jnp.maximum(n - 1, 0), 0)
    cur3 = lambda b, n, s: (b * nb + n, 0, 0)
    prev3 = lambda b, n, s: (b * nb + jnp.maximum(n - 1, 0), 0, 0)
    grid_spec = pltpu.PrefetchScalarGridSpec(
        num_scalar_prefetch=1, grid=(batch, nb),
        in_specs=[pl.BlockSpec((BLOCK, Q_COLS), cur), pl.BlockSpec((BLOCK, KV_COLS), cur),
                  pl.BlockSpec((BLOCK, KV_COLS), prev),
                  pl.BlockSpec((BLOCK, KV_COLS), lambda b, n, s: (0, 0)),
                  pl.BlockSpec((None, KV_COLS, BLOCK), cur3), pl.BlockSpec((None, KV_COLS, BLOCK), prev3),
                  pl.BlockSpec((None, KV_COLS, BLOCK), lambda b, n, s: (0, 0, 0))],
        out_specs=pl.BlockSpec((None, Q_COLS, BLOCK), cur3))
    return pl.pallas_call(
        functools.partial(_local_attn_kernel, banded=banded), grid_spec=grid_spec,
        out_shape=jax.ShapeDtypeStruct((m // BLOCK, Q_COLS, BLOCK), BF16),
        compiler_params=_cparams("parallel", "parallel"),
        name="swa_attn" if banded else "meta_attn")(sinks, q, k, k, km, vt3, vt3, vtm)


def _sortable(x):
    bits = lax.bitcast_convert_type(x, jnp.int32)
    return jnp.where(bits < 0, bits ^ jnp.int32(0x7FFFFFFF), bits)


def _dsa_kernel(q_ref, qi_ref, wt_ref, k3_ref, vt3_ref, ki3_ref, km_ref, vtm_ref, o_ref,
                key_s, acc_s, *, ksel, idx_bits):
    n = pl.program_id(1)
    cb = IDX_CHUNK_BLOCKS
    ck = cb * BLOCK
    n_chunks = n // cb + 1
    lane = lax.broadcasted_iota(jnp.int32, (1, BLOCK), 1)
    t_row = n * BLOCK + lane

    qis = _stack_heads(qi_ref, 0, IDX_HEADS, IDX_DIM)
    wts = [wt_ref[h:h + 1, :] for h in range(IDX_HEADS)]

    def score_chunk(c, carry):
        kic = ki3_ref[pl.ds(c * cb, cb)].reshape(ck, IDX_DIM)
        d = _dot_nt(kic, qis)
        sc = jnp.zeros((ck, BLOCK), F32)
        for h in range(IDX_HEADS):
            sc = sc + jnp.maximum(d[:, h * BLOCK:(h + 1) * BLOCK], 0.0) * wts[h]
        sc = sc * IDX_SCALE
        s_idx = c * ck + lax.broadcasted_iota(jnp.int32, (ck, BLOCK), 0)
        key = jnp.where(s_idx <= t_row, _sortable(sc), INT_MIN)
        key_s[pl.ds(c * cb, cb)] = key.reshape(cb, BLOCK, BLOCK)
        return carry

    lax.fori_loop(0, n_chunks, score_chunk, 0)

    def count(pred):
        def body(c, acc):
            keys = key_s[pl.ds(c * cb, cb)].reshape(ck, BLOCK)
            s_idx = c * ck + lax.broadcasted_iota(jnp.int32, (ck, BLOCK), 0)
            hit = jnp.where(pred(keys, s_idx), 1, 0).astype(jnp.int32)
            return acc + jnp.sum(hit.reshape(ck // 8, 8, BLOCK), axis=0)
        acc = lax.fori_loop(0, n_chunks, body, jnp.zeros((8, BLOCK), jnp.int32))
        return jnp.sum(acc, axis=0, keepdims=True)

    def thr_bit(i, tu):
        cand_u = tu | lax.shift_left(jnp.int32(1), 31 - i)
        cand = cand_u ^ jnp.int32(INT_MIN)
        return jnp.where(count(lambda k, s: k >= cand) >= ksel, cand_u, tu)

    thr_u = lax.fori_loop(0, 32, thr_bit, jnp.zeros((1, BLOCK), jnp.int32))
    thr = thr_u ^ jnp.int32(INT_MIN)
    n_gt = count(lambda k, s: k > thr)
    n_ge = count(lambda k, s: k >= thr)
    need = ksel - n_gt

    def tie_bit(i, x):
        cand = x | lax.shift_left(jnp.int32(1), idx_bits - 1 - i)
        return jnp.where(count(lambda k, s: (k == thr) & (s < cand)) <= need - 1, cand, x)

    def tie_search():
        return lax.fori_loop(0, idx_bits, tie_bit, jnp.zeros((1, BLOCK), jnp.int32))

    excess = jnp.max(jnp.where((n_ge > ksel) & (thr_u != 0), 1, 0))
    tie_x = lax.cond(excess > 0, tie_search,
                     lambda: jnp.full((1, BLOCK), (1 << idx_bits) - 1, jnp.int32))

    tiles = [(kh, kh * GROUP + ct * HEADS_PER_TILE)
             for kh in range(N_KV_HEADS) for ct in range(N_COL_TILES)]
    qs = [_stack_heads(q_ref, first, HEADS_PER_TILE, scale=ATTN_SCALE) for _, first in tiles]
    krow = lax.broadcasted_iota(jnp.int32, (BLOCK, BLOCK), 0)
    bias_m = _mask_bias(krow < N_META)
    m0 = []
    for i, (kh, _) in enumerate(tiles):
        hs = slice(kh * HEAD_DIM, (kh + 1) * HEAD_DIM)
        s = _dot_nt(km_ref[:, hs], qs[i]).astype(BF16) + bias_m
        m = jnp.max(s, axis=0, keepdims=True)
        m0.append(m)
        acc_s[i] = _dot(_values_with_ones(vtm_ref[...], kh), jnp.exp(s - m))

    ab = ATT_CHUNK_BLOCKS
    ak = ab * BLOCK
    arow = lax.broadcasted_iota(jnp.int32, (ak, BLOCK), 0)

    def attend(c, ms):
        keys = key_s[pl.ds(c * ab, ab)].reshape(ak, BLOCK)
        s_idx = c * ak + arow
        sel = ((keys > thr) | ((keys == thr) & (s_idx <= tie_x))) & (s_idx <= t_row)
        bias = _mask_bias(sel)
        kc = k3_ref[pl.ds(c * ab, ab)].reshape(ak, KV_COLS)
        vtc = jnp.concatenate([vt3_ref[c * ab + g] for g in range(ab)], axis=1)
        vs = [_values_with_ones(vtc, kh) for kh in range(N_KV_HEADS)]
        ks = [kc[:, kh * HEAD_DIM:(kh + 1) * HEAD_DIM] for kh in range(N_KV_HEADS)]
        logits = lambda i: _dot_nt(ks[tiles[i][0]], qs[i])
        ss = [logits(i) for i in range(PIPE)]
        new_m = []
        for i, (kh, _) in enumerate(tiles):
            s = ss[i].astype(BF16) + bias
            m = jnp.maximum(ms[i], jnp.max(s, axis=0, keepdims=True))
            alpha = jnp.exp(ms[i].astype(F32) - m.astype(F32))
            p = jnp.exp(s - m)
            new_m.append(m)
            if i + PIPE < len(tiles):
                ss.append(logits(i + PIPE))
            acc_s[i] = acc_s[i] * alpha + _dot(vs[kh], p)
        return tuple(new_m)

    lax.fori_loop(0, n // ab + 1, attend, tuple(m0))
    for i, (_, first) in enumerate(tiles):
        _store_tile_t(o_ref, first, acc_s[i])


def _dsa_attn(q, qi, wt, k3, vt3, ki3, km, vtm, *, batch):
    m = q.shape[0]
    nb = m // BLOCK // batch
    seq = nb * BLOCK
    assert nb % IDX_CHUNK_BLOCKS == 0 and IDX_CHUNK_BLOCKS % ATT_CHUNK_BLOCKS == 0
    ksel = min(TOPK_MAX, seq // 4)
    idx_bits = max(1, int(np.ceil(np.log2(seq))))
    qmap = lambda b, n: (b * nb + n, 0)
    bmap = lambda b, n: (b, 0, 0)
    return pl.pallas_call(
        functools.partial(_dsa_kernel, ksel=ksel, idx_bits=idx_bits), grid=(batch, nb),
        in_specs=[pl.BlockSpec((BLOCK, Q_COLS), qmap),
                  pl.BlockSpec((BLOCK, IDX_HEADS * IDX_DIM), qmap),
                  pl.BlockSpec((IDX_HEADS, BLOCK), lambda b, n: (0, b * nb + n)),
                  pl.BlockSpec((nb, BLOCK, KV_COLS), bmap),
                  pl.BlockSpec((nb, KV_COLS, BLOCK), bmap),
                  pl.BlockSpec((nb, BLOCK, IDX_DIM), bmap),
                  pl.BlockSpec((BLOCK, KV_COLS), lambda b, n: (0, 0)),
                  pl.BlockSpec((None, KV_COLS, BLOCK), lambda b, n: (0, 0, 0))],
        out_specs=pl.BlockSpec((None, Q_COLS, BLOCK), lambda b, n: (b * nb + n, 0, 0)),
        out_shape=jax.ShapeDtypeStruct((m // BLOCK, Q_COLS, BLOCK), BF16),
        scratch_shapes=[pltpu.VMEM((nb, BLOCK, BLOCK), jnp.int32),
                        pltpu.VMEM((N_KV_HEADS * N_COL_TILES, V_ROWS, COL_TILE), F32)],
        compiler_params=_cparams("parallel", "arbitrary"), name="dsa_attn",
    )(q, qi, wt, k3, vt3, ki3, km, vtm)


def _outproj_kernel(a_ref, w_ref, h_ref, g_ref, b_ref, wr_ref, br_ref, tri_ref,
                    h1_ref, h1t_ref, route_ref, cnt_ref, carry_s, *, alpha):
    i = pl.program_id(0)

    @pl.when(i == 0)
    def _():
        carry_s[...] = jnp.zeros_like(carry_s)

    y = jnp.concatenate([_dot_tn(a_ref[j], w_ref[...]) for j in range(a_ref.shape[0])], axis=0)
    h1 = _layer_norm(alpha * h_ref[...] + y, g_ref[...], b_ref[...])
    h1_ref[...] = h1
    _store_token_tiles(h1t_ref, h1)
    logits = _dot(h1.astype(BF16), wr_ref[...]) + br_ref[...]
    lane = lax.broadcasted_iota(jnp.int32, logits.shape, 1)
    idxs, vals = [], []
    rem = logits
    for _ in range(TOP_K):
        mx = jnp.max(rem, axis=-1, keepdims=True)
        ix = jnp.min(jnp.where(rem == mx, lane, LANES), axis=-1, keepdims=True)
        idxs.append(ix)
        vals.append(mx)
        rem = jnp.where(lane == ix, -jnp.inf, rem)
    exps = [jnp.exp(v - vals[0]) for v in vals]
    den = exps[0] + exps[1] + exps[2] + exps[3]
    sel = jnp.where(rem == -jnp.inf, 1.0, 0.0)
    before = _dot(tri_ref[...], sel.astype(BF16)) + carry_s[...]
    route = jnp.zeros(logits.shape, F32)
    for k in range(TOP_K):
        rank = jnp.sum(jnp.where(lane == idxs[k], before, 0.0), axis=-1, keepdims=True)
        route = jnp.where(lane == k, idxs[k].astype(F32), route)
        route = jnp.where(lane == TOP_K + k, exps[k] / den, route)
        route = jnp.where(lane == 2 * TOP_K + k, rank, route)
    route_ref[...] = route
    carry_s[...] = carry_s[...] + jnp.sum(sel, axis=0, keepdims=True)
    cnt_ref[...] = carry_s[...]


def _outproj(a_t, w, h, g, b, wr, br, *, alpha):
    m, d = h.shape
    tm = min(ROW_TILE, m)
    tri = jnp.tril(jnp.ones((tm, tm), BF16), -1)
    row = lambda i: (i, 0)
    const = lambda i: (0, 0)
    full = lambda x: pl.BlockSpec(x.shape, const)
    return pl.pallas_call(
        functools.partial(_outproj_kernel, alpha=alpha), grid=(m // tm,),
        in_specs=[pl.BlockSpec((tm // BLOCK, Q_COLS, BLOCK), lambda i: (i, 0, 0)), full(w),
                  pl.BlockSpec((tm, d), row), full(g), full(b), full(wr), full(br), full(tri)],
        out_specs=[pl.BlockSpec((tm, d), row), pl.BlockSpec((tm * TOKEN_ROWS, LANES), row),
                   pl.BlockSpec((tm, LANES), row), pl.BlockSpec((1, LANES), const)],
        out_shape=[jax.ShapeDtypeStruct((m, d), F32), jax.ShapeDtypeStruct((m * TOKEN_ROWS, LANES), F32),
                   jax.ShapeDtypeStruct((m, LANES), F32), jax.ShapeDtypeStruct((1, LANES), F32)],
        scratch_shapes=[pltpu.VMEM((1, LANES), F32)],
        compiler_params=_cparams("arbitrary"), name="outproj_router")(a_t, w, h, g, b, wr, br, tri)


def _expert_ffn(x, wgu_ref, bgu_ref, wd_ref, bd_ref):
    ff = wd_ref.shape[0]
    acc = jnp.zeros((x.shape[0], wd_ref.shape[1]), F32)
    for c in range(ff // FF_CHUNK):
        gs = slice(c * FF_CHUNK, (c + 1) * FF_CHUNK)
        us = slice(ff + c * FF_CHUNK, ff + (c + 1) * FF_CHUNK)
        gate = jnp.minimum(_dot(x, wgu_ref[:, gs]) + bgu_ref[:, gs], SWIGLU_LIMIT)
        up = jnp.clip(_dot(x, wgu_ref[:, us]) + bgu_ref[:, us], -SWIGLU_LIMIT, SWIGLU_LIMIT)
        act = gate * jax.nn.sigmoid(SWIGLU_ALPHA * gate) * (up + 1.0)
        acc = acc + _dot(act.astype(BF16), wd_ref[gs, :])
    return acc + bd_ref[...]


def _experts_kernel(be_ref, nu_ref, tok_ref, tok_next_ref, dst_ref, h_ref, wgu_ref, bgu_ref, wd_ref,
                    bd_ref, yk_ref, xbuf, ybuf, gsem, ssem):
    del be_ref
    i = pl.program_id(0)
    n_used = nu_ref[0]
    slot = i % 2
    rows = xbuf.shape[1] // TOKEN_ROWS
    tile = lambda t: pl.ds(pl.multiple_of(t * TOKEN_ROWS, TOKEN_ROWS), TOKEN_ROWS)

    def gather(src_ref, s):
        def body(r8, c):
            for u in range(DMA_UNROLL):
                r = r8 * DMA_UNROLL + u
                pltpu.make_async_copy(h_ref.at[tile(src_ref[0, 0, r])],
                                      xbuf.at[s, tile(r)], gsem.at[s]).start()
            return c
        lax.fori_loop(0, rows // DMA_UNROLL, body, 0)

    def scatter(s):
        def body(r8, c):
            for u in range(DMA_UNROLL):
                r = r8 * DMA_UNROLL + u
                pltpu.make_async_copy(ybuf.at[s, tile(r)],
                                      yk_ref.at[tile(dst_ref[0, 0, r])], ssem.at[s]).start()
            return c
        lax.fori_loop(0, rows // DMA_UNROLL, body, 0)

    def wait_gather(s):
        pltpu.make_async_copy(xbuf.at[s], xbuf.at[s], gsem.at[s]).wait()

    def wait_scatter(s):
        pltpu.make_async_copy(ybuf.at[s], ybuf.at[s], ssem.at[s]).wait()

    @pl.when(i == 0)
    def _():
        for s in range(2):
            ybuf[s] = jnp.zeros(ybuf.shape[1:], F32)
        fills = [pltpu.make_async_copy(
            ybuf.at[s], yk_ref.at[pl.ds(yk_ref.shape[0] - (2 - s) * ybuf.shape[1], ybuf.shape[1])],
            ssem.at[s]) for s in range(2)]
        for f in fills:
            f.start()
        for f in fills:
            f.wait()

    @pl.when((i == 0) & (n_used > 0))
    def _():
        gather(tok_ref, 0)

    @pl.when(i < n_used)
    def _():
        wait_gather(slot)

        @pl.when(i + 1 < n_used)
        def _():
            gather(tok_next_ref, 1 - slot)

        @pl.when(i >= 2)
        def _():
            wait_scatter(slot)

        x = _load_token_tiles(xbuf.at[slot], rows).astype(BF16)
        _store_token_tiles(ybuf.at[slot], _expert_ffn(x, wgu_ref, bgu_ref, wd_ref, bd_ref))
        scatter(slot)

    @pl.when(i == pl.num_programs(0) - 1)
    def _():
        @pl.when(n_used >= 1)
        def _():
            wait_scatter((n_used - 1) % 2)

        @pl.when(n_used >= 2)
        def _():
            wait_scatter(n_used % 2)


def _experts(blk_e, n_used, tok3, dst3, ht, wgu, bgu, wd, bd, n_out_rows):
    n_blk, _, rows = tok3.shape
    ff, d = wd.shape[1:]
    assert d == TOKEN_ROWS * LANES
    emap = lambda i, be, nu: (be[i], 0, 0)
    smem = lambda imap: pl.BlockSpec((1, 1, rows), imap, memory_space=pltpu.SMEM)
    grid_spec = pltpu.PrefetchScalarGridSpec(
        num_scalar_prefetch=2, grid=(n_blk,),
        in_specs=[smem(lambda i, be, nu: (i, 0, 0)),
                  smem(lambda i, be, nu: (jnp.minimum(i + 1, n_blk - 1), 0, 0)),
                  smem(lambda i, be, nu: (i, 0, 0)),
                  pl.BlockSpec(memory_space=pl.ANY),
                  pl.BlockSpec((None, d, 2 * ff), emap), pl.BlockSpec((None, 1, 2 * ff), emap),
                  pl.BlockSpec((None, ff, d), emap), pl.BlockSpec((None, 1, d), emap)],
        out_specs=pl.BlockSpec(memory_space=pl.ANY),
        scratch_shapes=[pltpu.VMEM((2, rows * TOKEN_ROWS, LANES), F32),
                        pltpu.VMEM((2, rows * TOKEN_ROWS, LANES), F32),
                        pltpu.SemaphoreType.DMA((2,)), pltpu.SemaphoreType.DMA((2,))])
    return pl.pallas_call(
        _experts_kernel, grid_spec=grid_spec,
        out_shape=jax.ShapeDtypeStruct((n_out_rows * TOKEN_ROWS, LANES), F32),
        compiler_params=_cparams("arbitrary"), name="moe_experts",
    )(blk_e, n_used, tok3, tok3, dst3, ht, wgu, bgu, wd, bd)


def _combine_kernel(route_ref, h_ref, g_ref, b_ref, y0_ref, y1_ref, y2_ref, y3_ref, o_ref, *, alpha):
    z = alpha * h_ref[...]
    for k, y_ref in enumerate((y0_ref, y1_ref, y2_ref, y3_ref)):
        z = z + route_ref[:, TOP_K + k:TOP_K + k + 1] * _load_token_tiles(y_ref, h_ref.shape[0])
    o_ref[...] = _layer_norm(z, g_ref[...], b_ref[...])


def _combine(route, h, g, b, yk, *, alpha):
    m, d = h.shape
    tm = min(ROW_TILE, m)
    nt = m // tm
    row = lambda i: (i, 0)
    const = lambda i: (0, 0)
    choice = lambda k: pl.BlockSpec((tm * TOKEN_ROWS, LANES), lambda i: (k * nt + i, 0))
    return pl.pallas_call(
        functools.partial(_combine_kernel, alpha=alpha), grid=(nt,),
        in_specs=[pl.BlockSpec((tm, LANES), row), pl.BlockSpec((tm, d), row),
                  pl.BlockSpec(g.shape, const), pl.BlockSpec(b.shape, const)]
                 + [choice(k) for k in range(TOP_K)],
        out_specs=pl.BlockSpec((tm, d), row),
        out_shape=jax.ShapeDtypeStruct((m, d), F32),
        compiler_params=_cparams("parallel"), name="moe_combine")(route, h, g, b, yk, yk, yk, yk)


def _meta_moe_kernel(hm_ref, gm_ref, wgu_ref, bgu_ref, wd_ref, bd_ref, g_ref, b_ref, o_ref, acc_s,
                     *, alpha):
    e = pl.program_id(0)

    @pl.when(e == 0)
    def _():
        acc_s[...] = jnp.zeros_like(acc_s)

    y = _expert_ffn(hm_ref[...].astype(BF16), wgu_ref, bgu_ref, wd_ref, bd_ref)
    lane = lax.broadcasted_iota(jnp.int32, gm_ref.shape, 1)
    ge = jnp.sum(jnp.where(lane == e, gm_ref[...], 0.0), axis=-1, keepdims=True)
    acc_s[...] = acc_s[...] + ge * y

    @pl.when(e == pl.num_programs(0) - 1)
    def _():
        o_ref[...] = _layer_norm(alpha * hm_ref[...] + acc_s[...], g_ref[...], b_ref[...])


def _meta_moe(hm, gm, wgu, bgu, wd, bd, g, b, *, alpha):
    m, d = hm.shape
    ff = wd.shape[1]
    const = lambda e: (0, 0)
    emap = lambda e: (e, 0, 0)
    return pl.pallas_call(
        functools.partial(_meta_moe_kernel, alpha=alpha), grid=(wgu.shape[0],),
        in_specs=[pl.BlockSpec((m, d), const), pl.BlockSpec((m, LANES), const),
                  pl.BlockSpec((None, d, 2 * ff), emap), pl.BlockSpec((None, 1, 2 * ff), emap),
                  pl.BlockSpec((None, ff, d), emap), pl.BlockSpec((None, 1, d), emap),
                  pl.BlockSpec(g.shape, const), pl.BlockSpec(b.shape, const)],
        out_specs=pl.BlockSpec((m, d), const),
        out_shape=jax.ShapeDtypeStruct((m, d), F32),
        scratch_shapes=[pltpu.VMEM((m, d), F32)],
        compiler_params=_cparams("arbitrary"), name="meta_experts")(hm, gm, wgu, bgu, wd, bd, g, b)


def _moe(h1, h1t, route, cnt, hm1, route_m, ep, g, b, *, alpha):
    m, d = h1.shape
    n_blk = -(-m * TOP_K // MOE_ROWS) + N_EXPERTS
    e = route[:, :TOP_K].astype(jnp.int32)
    rank = route[:, 2 * TOP_K:3 * TOP_K].astype(jnp.int32)
    counts = cnt[0, :N_EXPERTS].astype(jnp.int32)
    padded = (counts + MOE_ROWS - 1) // MOE_ROWS * MOE_ROWS
    pend = jnp.cumsum(padded)
    dest = (pend - padded)[e] + rank
    blk_start = jnp.arange(n_blk, dtype=jnp.int32) * MOE_ROWS
    blk_e = jnp.minimum(jnp.sum((pend[None, :] <= blk_start[:, None]).astype(jnp.int32), axis=1),
                        N_EXPERTS - 1)
    n_used = (pend[-1:] // MOE_ROWS).astype(jnp.int32)
    n_sorted = n_blk * MOE_ROWS
    inv = jnp.zeros((n_sorted,), jnp.int32).at[dest.reshape(-1)].set(
        jnp.arange(1, m * TOP_K + 1, dtype=jnp.int32))
    pair = jnp.maximum(inv - 1, 0)
    pos = jnp.arange(n_sorted, dtype=jnp.int32)
    pad_row = TOP_K * m + (pos // MOE_ROWS % 2) * MOE_ROWS + pos % MOE_ROWS
    tok3 = (pair // TOP_K).reshape(n_blk, 1, MOE_ROWS)
    dst3 = jnp.where(inv > 0, (pair % TOP_K) * m + pair // TOP_K, pad_row).reshape(n_blk, 1, MOE_ROWS)
    yk = _experts(blk_e, n_used, tok3, dst3, h1t, ep["wgu"], ep["bgu"], ep["wd"], ep["bd"],
                  TOP_K * m + 2 * MOE_ROWS)
    h2 = _combine(route, h1, g, b, yk, alpha=alpha)
    em = route_m[:, :TOP_K].astype(jnp.int32)
    gm = jnp.sum(jax.nn.one_hot(em, LANES, dtype=F32) * route_m[:, TOP_K:2 * TOP_K, None], axis=1)
    hm2 = _meta_moe(hm1, gm, ep["wgu"], ep["bgu"], ep["wd"], ep["bd"], g, b, alpha=alpha)
    return h2, hm2


def _split_in_weights(w, b, dsa, g=None, bn=None):
    cuts = np.cumsum([Q_COLS, KV_COLS, KV_COLS, IDX_HEADS * IDX_DIM, IDX_DIM]).tolist()
    wb = w.astype(BF16)
    row = lambda v: v.reshape(1, -1)
    col = lambda v: v.reshape(-1, 1)
    wp = {"wq": wb[:, :cuts[0]], "bq": row(b[:cuts[0]]),
          "wk": wb[:, cuts[0]:cuts[1]], "bk": row(b[cuts[0]:cuts[1]]),
          "wvt": wb[:, cuts[1]:cuts[2]].T, "bvt": col(b[cuts[1]:cuts[2]])}
    if dsa:
        pad = LANES - IDX_DIM
        wp.update({
            "wqi": wb[:, cuts[2]:cuts[3]], "bqi": row(b[cuts[2]:cuts[3]]),
            "wki": jnp.pad(wb[:, cuts[3]:cuts[4]], ((0, 0), (0, pad))),
            "bki": row(jnp.pad(b[cuts[3]:cuts[4]], (0, pad))),
            "wwt": wb[:, cuts[4]:].T, "bwt": col(b[cuts[4]:]),
            "g": row(jnp.pad(g, (0, pad))), "bn": row(jnp.pad(bn, (0, pad)))})
    return wp


def kernel(x, meta_tokens, w_in_a, b_in_a, sinks_a, w_out_a, w_in_b, b_in_b, idx_k_norm_g, idx_k_norm_b, w_out_b, ln_mix_g, ln_mix_b, w_router, b_router, w_gate_up, b_gate_up, w_down, b_down, ln_ffn_g, ln_ffn_b):
    batch, seq, d = x.shape
    depth = ln_mix_g.shape[0]
    alpha = float((2 * depth) ** 0.25)
    row = lambda v: v.reshape(1, -1)
    nb = seq // BLOCK

    h = x.reshape(batch * seq, d)
    hm = jnp.zeros((BLOCK, d), F32).at[:N_META].set(meta_tokens.astype(F32))
    tab_q = _rope_tables(jnp.arange(N_META, N_META + seq), ROT_DIM)
    tab_i = _rope_tables(jnp.arange(N_META, N_META + seq), IDX_ROT_DIM)
    tab_qm = _rope_tables(jnp.arange(BLOCK), ROT_DIM)
    tab_im = _rope_tables(jnp.arange(BLOCK), IDX_ROT_DIM)
    no_sinks = jnp.full((N_HEADS,), NEG, F32)

    for i in range(depth):
        j = i // 2
        dsa = i % 2 == 1
        if dsa:
            wp = _split_in_weights(w_in_b[j], b_in_b[j], True, idx_k_norm_g[j], idx_k_norm_b[j])
            w_out, sinks = w_out_b[j], no_sinks
        else:
            wp = _split_in_weights(w_in_a[j], b_in_a[j], False)
            w_out, sinks = w_out_a[j], sinks_a[j].astype(F32)
        pr = _inproj(h, wp, tab_q, tab_i, dsa=dsa, seq=seq)
        pm = _inproj(hm, wp, tab_qm, tab_im, dsa=dsa, seq=BLOCK)
        q, k, vt3 = pr[:3]
        qm, km, vtm = pm[:3]
        if dsa:
            qi, ki, wt = pr[3:]
            att = _dsa_attn(q, qi, wt, k.reshape(batch * nb, BLOCK, KV_COLS), vt3,
                            ki.reshape(batch * nb, BLOCK, IDX_DIM), km, vtm, batch=batch)
        else:
            att = _local_attn(sinks, q, k, vt3, km, vtm, banded=True, batch=batch)
        att_m = _local_attn(sinks, qm, km, vtm, km, vtm, banded=False, batch=1)

        w_out = w_out.astype(BF16)
        wr = jnp.pad(w_router[i].astype(BF16), ((0, 0), (0, LANES - N_EXPERTS)))
        br = row(jnp.pad(b_router[i].astype(F32), (0, LANES - N_EXPERTS), constant_values=NEG))
        g1, b1 = row(ln_mix_g[i]), row(ln_mix_b[i])
        h1, h1t, route, cnt = _outproj(att, w_out, h, g1, b1, wr, br, alpha=alpha)
        hm1, _, route_m, _ = _outproj(att_m, w_out, hm, g1, b1, wr, br, alpha=alpha)

        ep = {"wgu": w_gate_up[i].astype(BF16), "bgu": b_gate_up[i][:, None, :],
              "wd": w_down[i].astype(BF16), "bd": b_down[i][:, None, :]}
        h, hm = _moe(h1, h1t, route, cnt, hm1, route_m, ep, row(ln_ffn_g[i]), row(ln_ffn_b[i]),
                     alpha=alpha)
    return h.reshape(batch, seq, d)
```

```python
import functools

import jax
import jax.numpy as jnp
import numpy as np
from jax import lax
from jax.experimental import pallas as pl
from jax.experimental.pallas import tpu as pltpu

N_META = 16
HEAD_DIM = 64
N_HEADS = 16
N_KV_HEADS = 2
GROUP = N_HEADS // N_KV_HEADS
Q_COLS = N_HEADS * HEAD_DIM
KV_COLS = N_KV_HEADS * HEAD_DIM
ROT_DIM = HEAD_DIM // 4
ROPE_THETA = 500000.0
ATTN_SCALE = HEAD_DIM ** -0.5
BLOCK = 128
IDX_HEADS = 8
IDX_DIM = 64
IDX_ROT_DIM = IDX_DIM // 2
IDX_SCALE = IDX_DIM ** -0.5
IDX_W_SCALE = IDX_HEADS ** -0.5
TOPK_MAX = 256
N_EXPERTS = 32
TOP_K = 4
SWIGLU_LIMIT = 7.0
SWIGLU_ALPHA = 1.702
LN_EPS = 1e-5
NEG = -1e30
INT_MIN = -2147483648

LANES = 128
TOKEN_ROWS = 8
ROW_TILE = 512
MOE_ROWS = 512
FF_CHUNK = 256
DMA_UNROLL = 8
IDX_CHUNK_BLOCKS = 4
ATT_CHUNK_BLOCKS = 2
V_ROWS = HEAD_DIM + 16
COL_TILE = 256
HEADS_PER_TILE = COL_TILE // BLOCK
N_COL_TILES = GROUP // HEADS_PER_TILE
PIPE = 8
VMEM_LIMIT = 48 * 1024 * 1024

F32 = jnp.float32
BF16 = jnp.bfloat16
I16 = jnp.int16
HALF = 1 << 15


def _cparams(*sem):
    return pltpu.CompilerParams(dimension_semantics=sem, vmem_limit_bytes=VMEM_LIMIT)


def _dot(a, b):
    return jnp.dot(a, b, preferred_element_type=F32)


def _dot_nt(a, b):
    return lax.dot_general(a, b, (((1,), (1,)), ((), ())), preferred_element_type=F32)


def _dot_tn(a, b):
    return lax.dot_general(a, b, (((0,), (0,)), ((), ())), preferred_element_type=F32)


def _layer_norm(z, g, b):
    mu = jnp.mean(z, axis=-1, keepdims=True)
    d = z - mu
    var = jnp.mean(d * d, axis=-1, keepdims=True)
    return d * lax.rsqrt(var + LN_EPS) * g + b


def _store_token_tiles(ref, x):
    for j in range(TOKEN_ROWS):
        ref[pl.ds(j, x.shape[0], stride=TOKEN_ROWS), :] = x[:, j * LANES:(j + 1) * LANES]


def _load_token_tiles(ref, rows):
    return jnp.concatenate([ref[pl.ds(j, rows, stride=TOKEN_ROWS), :] for j in range(TOKEN_ROWS)], axis=1)


def _rope128(blk, tab_ref, half):
    return (blk * tab_ref[0] + pltpu.roll(blk, half, 1) * tab_ref[1]
            + pltpu.roll(blk, LANES - half, 1) * tab_ref[2])


def _inproj_kernel(*refs, dsa):
    if dsa:
        (h_ref, wq_ref, wk_ref, wvt_ref, wqi_ref, wki_ref, wwt_ref, bq_ref, bk_ref, bvt_ref,
         bqi_ref, bki_ref, bwt_ref, tabq_ref, tabi_ref, g_ref, bn_ref,
         q_ref, k_ref, vt_ref, qi_ref, ki_ref, wt_ref) = refs
    else:
        h_ref, wq_ref, wk_ref, wvt_ref, bq_ref, bk_ref, bvt_ref, tabq_ref, q_ref, k_ref, vt_ref = refs
    x = h_ref[...].astype(BF16)
    q = _dot(x, wq_ref[...]) + bq_ref[...]
    for j in range(Q_COLS // LANES):
        sl = slice(j * LANES, (j + 1) * LANES)
        q_ref[:, sl] = _rope128(q[:, sl], tabq_ref, ROT_DIM // 2).astype(BF16)
    k = _dot(x, wk_ref[...]) + bk_ref[...]
    k_ref[...] = _rope128(k, tabq_ref, ROT_DIM // 2).astype(BF16)
    vt = _dot_nt(wvt_ref[...], x) + bvt_ref[...]
    for j in range(vt_ref.shape[0]):
        vt_ref[j] = vt[:, j * BLOCK:(j + 1) * BLOCK].astype(BF16)
    if not dsa:
        return
    qi = _dot(x, wqi_ref[...]) + bqi_ref[...]
    for j in range(IDX_HEADS * IDX_DIM // LANES):
        sl = slice(j * LANES, (j + 1) * LANES)
        qi_ref[:, sl] = _rope128(qi[:, sl], tabi_ref, IDX_ROT_DIM // 2).astype(BF16)
    ki = _dot(x, wki_ref[...]) + bki_ref[...]
    real = lax.broadcasted_iota(jnp.int32, ki.shape, 1) < IDX_DIM
    mu = jnp.sum(jnp.where(real, ki, 0.0), axis=-1, keepdims=True) * (1.0 / IDX_DIM)
    d = jnp.where(real, ki - mu, 0.0)
    var = jnp.sum(d * d, axis=-1, keepdims=True) * (1.0 / IDX_DIM)
    kn = d * lax.rsqrt(var + LN_EPS) * g_ref[...] + bn_ref[...]
    ki_ref[...] = _rope128(kn, tabi_ref, IDX_ROT_DIM // 2)[:, :IDX_DIM].astype(BF16)
    wt_ref[...] = (_dot_nt(wwt_ref[...], x) + bwt_ref[...]) * IDX_W_SCALE


def _rope_tables(pos, rot_dim):
    half = rot_dim // 2
    inv_freq = ROPE_THETA ** (-jnp.arange(half, dtype=F32) / half)
    ang = pos.astype(F32)[:, None] * inv_freq[None, :]
    cos, sin = jnp.cos(ang), jnp.sin(ang)
    n = pos.shape[0]
    rest = HEAD_DIM - rot_dim
    c = jnp.concatenate([cos, cos, jnp.ones((n, rest), F32)], 1)
    s_up = jnp.concatenate([jnp.zeros((n, half), F32), sin, jnp.zeros((n, rest), F32)], 1)
    s_lo = jnp.concatenate([-sin, jnp.zeros((n, HEAD_DIM - half), F32)], 1)
    return jnp.stack([jnp.tile(t, (1, LANES // HEAD_DIM)) for t in (c, s_up, s_lo)])


def _inproj(h, wp, tabq, tabi, *, dsa, seq):
    m, d = h.shape
    tm = min(ROW_TILE, m)
    nsb = seq // tm
    row = lambda i: (i, 0)
    const = lambda i: (0, 0)
    tab_map = lambda i: (0, i % nsb, 0)
    full = lambda a: pl.BlockSpec(a.shape, const)
    names = ("wq", "wk", "wvt", "wqi", "wki", "wwt", "bq", "bk", "bvt", "bqi", "bki", "bwt") if dsa \
        else ("wq", "wk", "wvt", "bq", "bk", "bvt")
    ins = [h] + [wp[k] for k in names] + [tabq]
    in_specs = [pl.BlockSpec((tm, d), row)] + [full(wp[k]) for k in names] \
        + [pl.BlockSpec((3, tm, LANES), tab_map)]
    out_shape = [jax.ShapeDtypeStruct((m, Q_COLS), BF16), jax.ShapeDtypeStruct((m, KV_COLS), BF16),
                 jax.ShapeDtypeStruct((m // BLOCK, KV_COLS, BLOCK), BF16)]
    out_specs = [pl.BlockSpec((tm, Q_COLS), row), pl.BlockSpec((tm, KV_COLS), row),
                 pl.BlockSpec((tm // BLOCK, KV_COLS, BLOCK), lambda i: (i, 0, 0))]
    if dsa:
        ins += [tabi, wp["g"], wp["bn"]]
        in_specs += [pl.BlockSpec((3, tm, LANES), tab_map), full(wp["g"]), full(wp["bn"])]
        out_shape += [jax.ShapeDtypeStruct((m, IDX_HEADS * IDX_DIM), BF16),
                      jax.ShapeDtypeStruct((m, IDX_DIM), BF16),
                      jax.ShapeDtypeStruct((IDX_HEADS, m), F32)]
        out_specs += [pl.BlockSpec((tm, IDX_HEADS * IDX_DIM), row),
                      pl.BlockSpec((tm, IDX_DIM), row),
                      pl.BlockSpec((IDX_HEADS, tm), lambda i: (0, i))]
    return pl.pallas_call(
        functools.partial(_inproj_kernel, dsa=dsa), grid=(m // tm,), in_specs=in_specs,
        out_specs=out_specs, out_shape=out_shape, compiler_params=_cparams("parallel"),
        name="inproj_dsa" if dsa else "inproj_swa")(*ins)


def _stack_heads(ref, first, count, width=HEAD_DIM, scale=None):
    parts = [ref[:, (first + g) * width:(first + g + 1) * width] for g in range(count)]
    x = jnp.concatenate(parts, axis=0)
    return x if scale is None else x * scale


def _store_tile_t(o_ref, first_head, acc):
    out_t = acc[:HEAD_DIM] / acc[HEAD_DIM:HEAD_DIM + 1]
    for g in range(HEADS_PER_TILE):
        h = first_head + g
        o_ref[h * HEAD_DIM:(h + 1) * HEAD_DIM, :] = out_t[:, g * BLOCK:(g + 1) * BLOCK].astype(BF16)


def _values_with_ones(vt, kh):
    hs = slice(kh * HEAD_DIM, (kh + 1) * HEAD_DIM)
    return jnp.concatenate([vt[hs, :], jnp.ones((V_ROWS - HEAD_DIM, vt.shape[1]), BF16)], axis=0)


def _mask_bias(sel):
    bias = jnp.where(sel, 0.0, NEG).astype(BF16)
    return jnp.concatenate([bias] * HEADS_PER_TILE, axis=1)


def _local_attn_kernel(sink_ref, q_ref, kc_ref, kp_ref, km_ref, vtc_ref, vtp_ref, vtm_ref, o_ref,
                       *, banded):
    n = pl.program_id(1)
    r = lax.broadcasted_iota(jnp.int32, (BLOCK, BLOCK), 0)
    c = lax.broadcasted_iota(jnp.int32, (BLOCK, BLOCK), 1)
    blocks = [(kc_ref, vtc_ref, _mask_bias(r <= c))]
    if banded:
        blocks.append((kp_ref, vtp_ref, _mask_bias((r > c) & (n > 0))))
        blocks.append((km_ref, vtm_ref, _mask_bias(r < N_META)))
    sink_row = lax.broadcasted_iota(jnp.int32, (V_ROWS, COL_TILE), 0) == HEAD_DIM
    tiles = [(kh, kh * GROUP + ct * HEADS_PER_TILE)
             for kh in range(N_KV_HEADS) for ct in range(N_COL_TILES)]
    ks = [[k_ref[:, kh * HEAD_DIM:(kh + 1) * HEAD_DIM] for k_ref, _, _ in blocks]
          for kh in range(N_KV_HEADS)]
    vs = [[_values_with_ones(vt_ref[...], kh) for _, vt_ref, _ in blocks] for kh in range(N_KV_HEADS)]

    def logits(i):
        kh, first = tiles[i]
        qs = _stack_heads(q_ref, first, HEADS_PER_TILE, scale=ATTN_SCALE)
        return [_dot_nt(k, qs) for k in ks[kh]]

    sss = [logits(i) for i in range(PIPE)]
    for i, (kh, first) in enumerate(tiles):
        sink = jnp.concatenate(
            [jnp.full((1, BLOCK), sink_ref[first + g], F32) for g in range(HEADS_PER_TILE)], axis=1)
        ss = [s.astype(BF16) + bias for s, (_, _, bias) in zip(sss[i], blocks)]
        m = sink
        for s in ss:
            m = jnp.maximum(m, jnp.max(s, axis=0, keepdims=True).astype(F32))
        mb = m.astype(BF16)
        ps = [jnp.exp(s - mb) for s in ss]
        if i + PIPE < len(tiles):
            sss.append(logits(i + PIPE))
        acc = jnp.where(sink_row, jnp.exp(sink - mb.astype(F32)), 0.0)
        for p, v in zip(ps, vs[kh]):
            acc = acc + _dot(v, p)
        _store_tile_t(o_ref, first, acc)


def _local_attn(sinks, q, k, vt3, km, vtm, *, banded, batch):
    m = q.shape[0]
    nb = m // BLOCK // batch
    cur = lambda b, n, s: (b * nb + n, 0)
    prev = lambda b, n, s: (b * nb + jnp.maximum(n - 1, 0), 0)
    cur3 = lambda b, n, s: (b * nb + n, 0, 0)
    prev3 = lambda b, n, s: (b * nb + jnp.maximum(n - 1, 0), 0, 0)
    grid_spec = pltpu.PrefetchScalarGridSpec(
        num_scalar_prefetch=1, grid=(batch, nb),
        in_specs=[pl.BlockSpec((BLOCK, Q_COLS), cur), pl.BlockSpec((BLOCK, KV_COLS), cur),
                  pl.BlockSpec((BLOCK, KV_COLS), prev),
                  pl.BlockSpec((BLOCK, KV_COLS), lambda b, n, s: (0, 0)),
                  pl.BlockSpec((None, KV_COLS, BLOCK), cur3), pl.BlockSpec((None, KV_COLS, BLOCK), prev3),
                  pl.BlockSpec((None, KV_COLS, BLOCK), lambda b, n, s: (0, 0, 0))],
        out_specs=pl.BlockSpec((None, Q_COLS, BLOCK), cur3))
    return pl.pallas_call(
        functools.partial(_local_attn_kernel, banded=banded), grid_spec=grid_spec,
        out_shape=jax.ShapeDtypeStruct((m // BLOCK, Q_COLS, BLOCK), BF16),
        compiler_params=_cparams("parallel", "parallel"),
        name="swa_attn" if banded else "meta_attn")(sinks, q, k, k, km, vt3, vt3, vtm)


def _sortable(x):
    bits = lax.bitcast_convert_type(x, jnp.int32)
    return jnp.where(bits < 0, bits ^ jnp.int32(0x7FFFFFFF), bits)


def _dsa_kernel(q_ref, qi_ref, wt_ref, k3_ref, vt3_ref, ki3_ref, km_ref, vtm_ref, o_ref,
                key_s, hi_s, lo_s, acc_s, *, ksel, idx_bits):
    n = pl.program_id(1)
    cb = IDX_CHUNK_BLOCKS
    ck = cb * BLOCK
    n_chunks = n // cb + 1
    lane = lax.broadcasted_iota(jnp.int32, (1, BLOCK), 1)
    t_row = n * BLOCK + lane

    qis = _stack_heads(qi_ref, 0, IDX_HEADS, IDX_DIM)
    wts = [wt_ref[h:h + 1, :] for h in range(IDX_HEADS)]

    def score_chunk(c, carry):
        kic = ki3_ref[pl.ds(c * cb, cb)].reshape(ck, IDX_DIM)
        d = _dot_nt(kic, qis)
        sc = jnp.zeros((ck, BLOCK), F32)
        for h in range(IDX_HEADS):
            sc = sc + jnp.maximum(d[:, h * BLOCK:(h + 1) * BLOCK], 0.0) * wts[h]
        sc = sc * IDX_SCALE
        s_idx = c * ck + lax.broadcasted_iota(jnp.int32, (ck, BLOCK), 0)
        key = jnp.where(s_idx <= t_row, _sortable(sc), INT_MIN)
        key_s[pl.ds(c * cb, cb)] = key.reshape(cb, BLOCK, BLOCK)
        hi = lax.shift_right_arithmetic(key, 16).astype(I16)
        lo = ((key & 0xFFFF) - HALF).astype(I16)
        hi_s[pl.ds(c * cb, cb)] = hi.reshape(cb, BLOCK, BLOCK)
        lo_s[pl.ds(c * cb, cb)] = lo.reshape(cb, BLOCK, BLOCK)
        return carry

    lax.fori_loop(0, n_chunks, score_chunk, 0)

    def count16(ref, pred):
        def body(c, acc):
            v = ref[pl.ds(c * cb, cb)].reshape(ck, BLOCK)
            hit = jnp.where(pred(v), jnp.int16(1), jnp.int16(0))
            parts = [hit[j * 16:(j + 1) * 16] for j in range(ck // 16)]
            while len(parts) > 1:
                parts = [a + b for a, b in zip(parts[::2], parts[1::2])]
            return acc + parts[0]
        acc = lax.fori_loop(0, n_chunks, body, jnp.zeros((16, BLOCK), I16))
        return jnp.sum(acc.astype(jnp.int32), axis=0, keepdims=True)

    def bisect16(ref, base):
        def bit(i, tu):
            cand_u = tu | lax.shift_left(jnp.int32(1), 15 - i)
            cand = (cand_u - HALF).astype(I16)
            return jnp.where(base + count16(ref, lambda v: v >= cand) >= ksel, cand_u, tu)
        return lax.fori_loop(0, 16, bit, jnp.zeros((1, BLOCK), jnp.int32))

    thr_hi = bisect16(hi_s, 0) - HALF
    thr_hi16 = thr_hi.astype(I16)
    n_gt_hi = count16(hi_s, lambda v: v > thr_hi16)

    def keep_lo(c, carry):
        sl = pl.ds(c * cb, cb)
        lo_s[sl] = jnp.where(hi_s[sl] == thr_hi16, lo_s[sl], jnp.int16(-HALF))
        return carry

    lax.fori_loop(0, n_chunks, keep_lo, 0)
    thr = lax.shift_left(thr_hi, 16) | bisect16(lo_s, n_gt_hi)

    def count(pred):
        def body(c, acc):
            keys = key_s[pl.ds(c * cb, cb)].reshape(ck, BLOCK)
            s_idx = c * ck + lax.broadcasted_iota(jnp.int32, (ck, BLOCK), 0)
            hit = jnp.where(pred(keys, s_idx), 1, 0).astype(jnp.int32)
            return acc + jnp.sum(hit.reshape(ck // 8, 8, BLOCK), axis=0)
        acc = lax.fori_loop(0, n_chunks, body, jnp.zeros((8, BLOCK), jnp.int32))
        return jnp.sum(acc, axis=0, keepdims=True)

    n_gt = count(lambda k, s: k > thr)
    n_ge = count(lambda k, s: k >= thr)
    need = ksel - n_gt

    def tie_bit(i, x):
        cand = x | lax.shift_left(jnp.int32(1), idx_bits - 1 - i)
        return jnp.where(count(lambda k, s: (k == thr) & (s < cand)) <= need - 1, cand, x)

    def tie_search():
        return lax.fori_loop(0, idx_bits, tie_bit, jnp.zeros((1, BLOCK), jnp.int32))

    excess = jnp.max(jnp.where((n_ge > ksel) & (thr != INT_MIN), 1, 0))
    tie_x = lax.cond(excess > 0, tie_search,
                     lambda: jnp.full((1, BLOCK), (1 << idx_bits) - 1, jnp.int32))

    tiles = [(kh, kh * GROUP + ct * HEADS_PER_TILE)
             for kh in range(N_KV_HEADS) for ct in range(N_COL_TILES)]
    qs = [_stack_heads(q_ref, first, HEADS_PER_TILE, scale=ATTN_SCALE) for _, first in tiles]
    krow = lax.broadcasted_iota(jnp.int32, (BLOCK, BLOCK), 0)
    bias_m = _mask_bias(krow < N_META)
    m0 = []
    for i, (kh, _) in enumerate(tiles):
        hs = slice(kh * HEAD_DIM, (kh + 1) * HEAD_DIM)
        s = _dot_nt(km_ref[:, hs], qs[i]).astype(BF16) + bias_m
        m = jnp.max(s, axis=0, keepdims=True)
        m0.append(m)
        acc_s[i] = _dot(_values_with_ones(vtm_ref[...], kh), jnp.exp(s - m))

    ab = ATT_CHUNK_BLOCKS
    ak = ab * BLOCK
    arow = lax.broadcasted_iota(jnp.int32, (ak, BLOCK), 0)

    def attend(c, ms):
        keys = key_s[pl.ds(c * ab, ab)].reshape(ak, BLOCK)
        s_idx = c * ak + arow
        sel = ((keys > thr) | ((keys == thr) & (s_idx <= tie_x))) & (s_idx <= t_row)
        bias = _mask_bias(sel)
        kc = k3_ref[pl.ds(c * ab, ab)].reshape(ak, KV_COLS)
        vtc = jnp.concatenate([vt3_ref[c * ab + g] for g in range(ab)], axis=1)
        vs = [_values_with_ones(vtc, kh) for kh in range(N_KV_HEADS)]
        ks = [kc[:, kh * HEAD_DIM:(kh + 1) * HEAD_DIM] for kh in range(N_KV_HEADS)]
        logits = lambda i: _dot_nt(ks[tiles[i][0]], qs[i])
        ss = [logits(i) for i in range(PIPE)]
        new_m = []
        for i, (kh, _) in enumerate(tiles):
            s = ss[i].astype(BF16) + bias
            m = jnp.maximum(ms[i], jnp.max(s, axis=0, keepdims=True))
            alpha = jnp.exp(ms[i].astype(F32) - m.astype(F32))
            p = jnp.exp(s - m)
            new_m.append(m)
            if i + PIPE < len(tiles):
                ss.append(logits(i + PIPE))
            acc_s[i] = acc_s[i] * alpha + _dot(vs[kh], p)
        return tuple(new_m)

    lax.fori_loop(0, n // ab + 1, attend, tuple(m0))
    for i, (_, first) in enumerate(tiles):
        _store_tile_t(o_ref, first, acc_s[i])


def _dsa_attn(q, qi, wt, k3, vt3, ki3, km, vtm, *, batch):
    m = q.shape[0]
    nb = m // BLOCK // batch
    seq = nb * BLOCK
    assert nb % IDX_CHUNK_BLOCKS == 0 and IDX_CHUNK_BLOCKS % ATT_CHUNK_BLOCKS == 0
    ksel = min(TOPK_MAX, seq // 4)
    idx_bits = max(1, int(np.ceil(np.log2(seq))))
    qmap = lambda b, n: (b * nb + n, 0)
    bmap = lambda b, n: (b, 0, 0)
    return pl.pallas_call(
        functools.partial(_dsa_kernel, ksel=ksel, idx_bits=idx_bits), grid=(batch, nb),
        in_specs=[pl.BlockSpec((BLOCK, Q_COLS), qmap),
                  pl.BlockSpec((BLOCK, IDX_HEADS * IDX_DIM), qmap),
                  pl.BlockSpec((IDX_HEADS, BLOCK), lambda b, n: (0, b * nb + n)),
                  pl.BlockSpec((nb, BLOCK, KV_COLS), bmap),
                  pl.BlockSpec((nb, KV_COLS, BLOCK), bmap),
                  pl.BlockSpec((nb, BLOCK, IDX_DIM), bmap),
                  pl.BlockSpec((BLOCK, KV_COLS), lambda b, n: (0, 0)),
                  pl.BlockSpec((None, KV_COLS, BLOCK), lambda b, n: (0, 0, 0))],
        out_specs=pl.BlockSpec((None, Q_COLS, BLOCK), lambda b, n: (b * nb + n, 0, 0)),
        out_shape=jax.ShapeDtypeStruct((m // BLOCK, Q_COLS, BLOCK), BF16),
        scratch_shapes=[pltpu.VMEM((nb, BLOCK, BLOCK), jnp.int32),
                        pltpu.VMEM((nb, BLOCK, BLOCK), I16), pltpu.VMEM((nb, BLOCK, BLOCK), I16),
                        pltpu.VMEM((N_KV_HEADS * N_COL_TILES, V_ROWS, COL_TILE), F32)],
        compiler_params=_cparams("parallel", "arbitrary"), name="dsa_attn",
    )(q, qi, wt, k3, vt3, ki3, km, vtm)


def _outproj_kernel(a_ref, w_ref, h_ref, g_ref, b_ref, wr_ref, br_ref, tri_ref,
                    h1_ref, h1t_ref, route_ref, cnt_ref, carry_s, *, alpha):
    i = pl.program_id(0)

    @pl.when(i == 0)
    def _():
        carry_s[...] = jnp.zeros_like(carry_s)

    y = jnp.concatenate([_dot_tn(a_ref[j], w_ref[...]) for j in range(a_ref.shape[0])], axis=0)
    h1 = _layer_norm(alpha * h_ref[...] + y, g_ref[...], b_ref[...])
    h1_ref[...] = h1
    _store_token_tiles(h1t_ref, h1)
    logits = _dot(h1.astype(BF16), wr_ref[...]) + br_ref[...]
    lane = lax.broadcasted_iota(jnp.int32, logits.shape, 1)
    idxs, vals = [], []
    rem = logits
    for _ in range(TOP_K):
        mx = jnp.max(rem, axis=-1, keepdims=True)
        ix = jnp.min(jnp.where(rem == mx, lane, LANES), axis=-1, keepdims=True)
        idxs.append(ix)
        vals.append(mx)
        rem = jnp.where(lane == ix, -jnp.inf, rem)
    exps = [jnp.exp(v - vals[0]) for v in vals]
    den = exps[0] + exps[1] + exps[2] + exps[3]
    sel = jnp.where(rem == -jnp.inf, 1.0, 0.0)
    before = _dot(tri_ref[...], sel.astype(BF16)) + carry_s[...]
    route = jnp.zeros(logits.shape, F32)
    for k in range(TOP_K):
        rank = jnp.sum(jnp.where(lane == idxs[k], before, 0.0), axis=-1, keepdims=True)
        route = jnp.where(lane == k, idxs[k].astype(F32), route)
        route = jnp.where(lane == TOP_K + k, exps[k] / den, route)
        route = jnp.where(lane == 2 * TOP_K + k, rank, route)
    route_ref[...] = route
    carry_s[...] = carry_s[...] + jnp.sum(sel, axis=0, keepdims=True)
    cnt_ref[...] = carry_s[...]


def _outproj(a_t, w, h, g, b, wr, br, *, alpha):
    m, d = h.shape
    tm = min(ROW_TILE, m)
    tri = jnp.tril(jnp.ones((tm, tm), BF16), -1)
    row = lambda i: (i, 0)
    const = lambda i: (0, 0)
    full = lambda x: pl.BlockSpec(x.shape, const)
    return pl.pallas_call(
        functools.partial(_outproj_kernel, alpha=alpha), grid=(m // tm,),
        in_specs=[pl.BlockSpec((tm // BLOCK, Q_COLS, BLOCK), lambda i: (i, 0, 0)), full(w),
                  pl.BlockSpec((tm, d), row), full(g), full(b), full(wr), full(br), full(tri)],
        out_specs=[pl.BlockSpec((tm, d), row), pl.BlockSpec((tm * TOKEN_ROWS, LANES), row),
                   pl.BlockSpec((tm, LANES), row), pl.BlockSpec((1, LANES), const)],
        out_shape=[jax.ShapeDtypeStruct((m, d), F32), jax.ShapeDtypeStruct((m * TOKEN_ROWS, LANES), F32),
                   jax.ShapeDtypeStruct((m, LANES), F32), jax.ShapeDtypeStruct((1, LANES), F32)],
        scratch_shapes=[pltpu.VMEM((1, LANES), F32)],
        compiler_params=_cparams("arbitrary"), name="outproj_router")(a_t, w, h, g, b, wr, br, tri)


def _ffn_chunk(x, wg, wu, bg, bu, wd):
    gate = jnp.minimum(_dot(x, wg) + bg, SWIGLU_LIMIT)
    up = jnp.clip(_dot(x, wu) + bu, -SWIGLU_LIMIT, SWIGLU_LIMIT)
    act = gate * jax.nn.sigmoid(SWIGLU_ALPHA * gate) * (up + 1.0)
    return _dot(act.astype(BF16), wd)


def _experts_kernel(be_ref, nu_ref, tok_ref, tok_next_ref, dst_prev_ref, dst_ref, h_ref,
                    wg_ref, wu_ref, bg_ref, bu_ref, wd_ref, bd_ref, yk_ref,
                    xbuf, ybuf, gsem, ssem):
    del be_ref
    i = pl.program_id(0)
    n_used = nu_ref[0]
    slot = i % 2
    other = 1 - slot
    rows = xbuf.shape[1] // TOKEN_ROWS
    n_chunks = wg_ref.shape[0]
    chunk_rows = rows // n_chunks
    pad_base = yk_ref.shape[0] // TOKEN_ROWS - 2 * rows
    tile = lambda t: pl.ds(pl.multiple_of(t * TOKEN_ROWS, TOKEN_ROWS), TOKEN_ROWS)

    def gather_row(src_ref, s, r):
        pltpu.make_async_copy(h_ref.at[tile(src_ref[0, 0, r])], xbuf.at[s, tile(r)], gsem.at[s]).start()

    def scatter_row(d, s, r):
        pltpu.make_async_copy(ybuf.at[s, tile(r)], yk_ref.at[tile(d)], ssem.at[s]).start()

    def gather(src_ref, s):
        def body(r8, c):
            for u in range(DMA_UNROLL):
                gather_row(src_ref, s, r8 * DMA_UNROLL + u)
            return c
        lax.fori_loop(0, rows // DMA_UNROLL, body, 0)

    def scatter(s):
        def body(r8, c):
            for u in range(DMA_UNROLL):
                r = r8 * DMA_UNROLL + u
                scatter_row(dst_ref[0, 0, r], s, r)
            return c
        lax.fori_loop(0, rows // DMA_UNROLL, body, 0)

    def wait_gather(s):
        pltpu.make_async_copy(xbuf.at[s], xbuf.at[s], gsem.at[s]).wait()

    def wait_scatter(s):
        pltpu.make_async_copy(ybuf.at[s], ybuf.at[s], ssem.at[s]).wait()

    @pl.when(i == 0)
    def _():
        for s in range(2):
            ybuf[s] = jnp.zeros(ybuf.shape[1:], F32)
        fills = [pltpu.make_async_copy(
            ybuf.at[s], yk_ref.at[pl.ds(yk_ref.shape[0] - (2 - s) * ybuf.shape[1], ybuf.shape[1])],
            ssem.at[s]) for s in range(2)]
        for f in fills:
            f.start()
        for f in fills:
            f.wait()

    @pl.when((i == 0) & (n_used > 0))
    def _():
        gather(tok_ref, 0)

    @pl.when(i < n_used)
    def _():
        wait_gather(slot)

        @pl.when(i >= 1)
        def _():
            wait_scatter(slot)

        x = _load_token_tiles(xbuf.at[slot], rows).astype(BF16)
        y = jnp.broadcast_to(bd_ref[...], (rows, bd_ref.shape[1]))
        for c in range(n_chunks):
            for r in range(c * chunk_rows, (c + 1) * chunk_rows):
                gather_row(tok_next_ref, other, r)
                scatter_row(jnp.where(i > 0, dst_prev_ref[0, 0, r], pad_base + other * rows + r),
                            other, r)
            y = y + _ffn_chunk(x, wg_ref[c], wu_ref[c], bg_ref[c], bu_ref[c], wd_ref[c])
        _store_token_tiles(ybuf.at[slot], y)

        @pl.when(i == n_used - 1)
        def _():
            scatter(slot)
            wait_scatter(other)
            wait_gather(other)
            wait_scatter(slot)


def _experts(blk_e, n_used, tok3, dst3, ht, ep, n_out_rows):
    n_blk, _, rows = tok3.shape
    n_chunks, ffc, d = ep["wd"].shape[1:]
    assert d == TOKEN_ROWS * LANES and rows % n_chunks == 0
    emap = lambda i, be, nu: (be[i], 0, 0, 0)
    smem = lambda imap: pl.BlockSpec((1, 1, rows), imap, memory_space=pltpu.SMEM)
    grid_spec = pltpu.PrefetchScalarGridSpec(
        num_scalar_prefetch=2, grid=(n_blk,),
        in_specs=[smem(lambda i, be, nu: (i, 0, 0)),
                  smem(lambda i, be, nu: (jnp.minimum(i + 1, n_blk - 1), 0, 0)),
                  smem(lambda i, be, nu: (jnp.maximum(i - 1, 0), 0, 0)),
                  smem(lambda i, be, nu: (i, 0, 0)),
                  pl.BlockSpec(memory_space=pl.ANY),
                  pl.BlockSpec((None, n_chunks, d, ffc), emap), pl.BlockSpec((None, n_chunks, d, ffc), emap),
                  pl.BlockSpec((None, n_chunks, 1, ffc), emap), pl.BlockSpec((None, n_chunks, 1, ffc), emap),
                  pl.BlockSpec((None, n_chunks, ffc, d), emap),
                  pl.BlockSpec((None, 1, d), lambda i, be, nu: (be[i], 0, 0))],
        out_specs=pl.BlockSpec(memory_space=pl.ANY),
        scratch_shapes=[pltpu.VMEM((2, rows * TOKEN_ROWS, LANES), F32),
                        pltpu.VMEM((2, rows * TOKEN_ROWS, LANES), F32),
                        pltpu.SemaphoreType.DMA((2,)), pltpu.SemaphoreType.DMA((2,))])
    return pl.pallas_call(
        _experts_kernel, grid_spec=grid_spec,
        out_shape=jax.ShapeDtypeStruct((n_out_rows * TOKEN_ROWS, LANES), F32),
        compiler_params=_cparams("arbitrary"), name="moe_experts",
    )(blk_e, n_used, tok3, tok3, dst3, dst3, ht, ep["wg"], ep["wu"], ep["bg"], ep["bu"], ep["wd"], ep["bd"])


def _combine_kernel(route_ref, h_ref, g_ref, b_ref, y0_ref, y1_ref, y2_ref, y3_ref, o_ref, *, alpha):
    z = alpha * h_ref[...]
    for k, y_ref in enumerate((y0_ref, y1_ref, y2_ref, y3_ref)):
        z = z + route_ref[:, TOP_K + k:TOP_K + k + 1] * _load_token_tiles(y_ref, h_ref.shape[0])
    o_ref[...] = _layer_norm(z, g_ref[...], b_ref[...])


def _combine(route, h, g, b, yk, *, alpha):
    m, d = h.shape
    tm = min(ROW_TILE, m)
    nt = m // tm
    row = lambda i: (i, 0)
    const = lambda i: (0, 0)
    choice = lambda k: pl.BlockSpec((tm * TOKEN_ROWS, LANES), lambda i: (k * nt + i, 0))
    return pl.pallas_call(
        functools.partial(_combine_kernel, alpha=alpha), grid=(nt,),
        in_specs=[pl.BlockSpec((tm, LANES), row), pl.BlockSpec((tm, d), row),
                  pl.BlockSpec(g.shape, const), pl.BlockSpec(b.shape, const)]
                 + [choice(k) for k in range(TOP_K)],
        out_specs=pl.BlockSpec((tm, d), row),
        out_shape=jax.ShapeDtypeStruct((m, d), F32),
        compiler_params=_cparams("parallel"), name="moe_combine")(route, h, g, b, yk, yk, yk, yk)


def _meta_moe_kernel(hm_ref, gm_ref, wg_ref, wu_ref, bg_ref, bu_ref, wd_ref, bd_ref, g_ref, b_ref,
                     o_ref, acc_s, *, alpha):
    e = pl.program_id(0)

    @pl.when(e == 0)
    def _():
        acc_s[...] = jnp.zeros_like(acc_s)

    x = hm_ref[...].astype(BF16)
    y = bd_ref[...]
    for c in range(wg_ref.shape[0]):
        y = y + _ffn_chunk(x, wg_ref[c], wu_ref[c], bg_ref[c], bu_ref[c], wd_ref[c])
    lane = lax.broadcasted_iota(jnp.int32, gm_ref.shape, 1)
    ge = jnp.sum(jnp.where(lane == e, gm_ref[...], 0.0), axis=-1, keepdims=True)
    acc_s[...] = acc_s[...] + ge * y

    @pl.when(e == pl.num_programs(0) - 1)
    def _():
        o_ref[...] = _layer_norm(alpha * hm_ref[...] + acc_s[...], g_ref[...], b_ref[...])


def _meta_moe(hm, gm, ep, g, b, *, alpha):
    m, d = hm.shape
    n_exp, n_chunks, ffc = ep["wd"].shape[:3]
    const = lambda e: (0, 0)
    emap = lambda e: (e, 0, 0, 0)
    return pl.pallas_call(
        functools.partial(_meta_moe_kernel, alpha=alpha), grid=(n_exp,),
        in_specs=[pl.BlockSpec((m, d), const), pl.BlockSpec((m, LANES), const),
                  pl.BlockSpec((None, n_chunks, d, ffc), emap), pl.BlockSpec((None, n_chunks, d, ffc), emap),
                  pl.BlockSpec((None, n_chunks, 1, ffc), emap), pl.BlockSpec((None, n_chunks, 1, ffc), emap),
                  pl.BlockSpec((None, n_chunks, ffc, d), emap),
                  pl.BlockSpec((None, 1, d), lambda e: (e, 0, 0)),
                  pl.BlockSpec(g.shape, const), pl.BlockSpec(b.shape, const)],
        out_specs=pl.BlockSpec((m, d), const),
        out_shape=jax.ShapeDtypeStruct((m, d), F32),
        scratch_shapes=[pltpu.VMEM((m, d), F32)],
        compiler_params=_cparams("arbitrary"), name="meta_experts",
    )(hm, gm, ep["wg"], ep["wu"], ep["bg"], ep["bu"], ep["wd"], ep["bd"], g, b)


def _moe(h1, h1t, route, cnt, hm1, route_m, ep, g, b, *, alpha):
    m, d = h1.shape
    n_blk = -(-m * TOP_K // MOE_ROWS) + N_EXPERTS
    e = route[:, :TOP_K].astype(jnp.int32)
    rank = route[:, 2 * TOP_K:3 * TOP_K].astype(jnp.int32)
    counts = cnt[0, :N_EXPERTS].astype(jnp.int32)
    padded = (counts + MOE_ROWS - 1) // MOE_ROWS * MOE_ROWS
    pend = jnp.cumsum(padded)
    dest = (pend - padded)[e] + rank
    blk_start = jnp.arange(n_blk, dtype=jnp.int32) * MOE_ROWS
    blk_e = jnp.minimum(jnp.sum((pend[None, :] <= blk_start[:, None]).astype(jnp.int32), axis=1),
                        N_EXPERTS - 1)
    n_used = (pend[-1:] // MOE_ROWS).astype(jnp.int32)
    n_sorted = n_blk * MOE_ROWS
    inv = jnp.zeros((n_sorted,), jnp.int32).at[dest.reshape(-1)].set(
        jnp.arange(1, m * TOP_K + 1, dtype=jnp.int32), unique_indices=True, mode="promise_in_bounds")
    pair = jnp.maximum(inv - 1, 0)
    pos = jnp.arange(n_sorted, dtype=jnp.int32)
    pad_row = TOP_K * m + (pos // MOE_ROWS % 2) * MOE_ROWS + pos % MOE_ROWS
    tok3 = (pair // TOP_K).reshape(n_blk, 1, MOE_ROWS)
    dst3 = jnp.where(inv > 0, (pair % TOP_K) * m + pair // TOP_K, pad_row).reshape(n_blk, 1, MOE_ROWS)
    yk = _experts(blk_e, n_used, tok3, dst3, h1t, ep, TOP_K * m + 2 * MOE_ROWS)
    h2 = _combine(route, h1, g, b, yk, alpha=alpha)
    em = route_m[:, :TOP_K].astype(jnp.int32)
    gm = jnp.sum(jax.nn.one_hot(em, LANES, dtype=F32) * route_m[:, TOP_K:2 * TOP_K, None], axis=1)
    hm2 = _meta_moe(hm1, gm, ep, g, b, alpha=alpha)
    return h2, hm2


def _expert_weights(w_gu, b_gu, w_d, b_d):
    n_exp, d, ff2 = w_gu.shape
    ff = ff2 // 2
    nc = ff // FF_CHUNK
    cols = lambda w: w.reshape(n_exp, d, nc, FF_CHUNK).transpose(0, 2, 1, 3).astype(BF16)
    bias = lambda v: v.reshape(n_exp, nc, 1, FF_CHUNK)
    return {"wg": cols(w_gu[:, :, :ff]), "wu": cols(w_gu[:, :, ff:]),
            "bg": bias(b_gu[:, :ff]), "bu": bias(b_gu[:, ff:]),
            "wd": w_d.astype(BF16).reshape(n_exp, nc, FF_CHUNK, d), "bd": b_d[:, None, :]}


def _split_in_weights(w, b, dsa, g=None, bn=None):
    cuts = np.cumsum([Q_COLS, KV_COLS, KV_COLS, IDX_HEADS * IDX_DIM, IDX_DIM]).tolist()
    wb = w.astype(BF16)
    row = lambda v: v.reshape(1, -1)
    col = lambda v: v.reshape(-1, 1)
    wp = {"wq": wb[:, :cuts[0]], "bq": row(b[:cuts[0]]),
          "wk": wb[:, cuts[0]:cuts[1]], "bk": row(b[cuts[0]:cuts[1]]),
          "wvt": wb[:, cuts[1]:cuts[2]].T, "bvt": col(b[cuts[1]:cuts[2]])}
    if dsa:
        pad = LANES - IDX_DIM
        wp.update({
            "wqi": wb[:, cuts[2]:cuts[3]], "bqi": row(b[cuts[2]:cuts[3]]),
            "wki": jnp.pad(wb[:, cuts[3]:cuts[4]], ((0, 0), (0, pad))),
            "bki": row(jnp.pad(b[cuts[3]:cuts[4]], (0, pad))),
            "wwt": wb[:, cuts[4]:].T, "bwt": col(b[cuts[4]:]),
            "g": row(jnp.pad(g, (0, pad))), "bn": row(jnp.pad(bn, (0, pad)))})
    return wp


def kernel(x, meta_tokens, w_in_a, b_in_a, sinks_a, w_out_a, w_in_b, b_in_b, idx_k_norm_g, idx_k_norm_b, w_out_b, ln_mix_g, ln_mix_b, w_router, b_router, w_gate_up, b_gate_up, w_down, b_down, ln_ffn_g, ln_ffn_b):
    batch, seq, d = x.shape
    depth = ln_mix_g.shape[0]
    alpha = float((2 * depth) ** 0.25)
    row = lambda v: v.reshape(1, -1)
    nb = seq // BLOCK

    h = x.reshape(batch * seq, d)
    hm = jnp.zeros((BLOCK, d), F32).at[:N_META].set(meta_tokens.astype(F32))
    tab_q = _rope_tables(jnp.arange(N_META, N_META + seq), ROT_DIM)
    tab_i = _rope_tables(jnp.arange(N_META, N_META + seq), IDX_ROT_DIM)
    tab_qm = _rope_tables(jnp.arange(BLOCK), ROT_DIM)
    tab_im = _rope_tables(jnp.arange(BLOCK), IDX_ROT_DIM)
    no_sinks = jnp.full((N_HEADS,), NEG, F32)

    for i in range(depth):
        j = i // 2
        dsa = i % 2 == 1
        if dsa:
            wp = _split_in_weights(w_in_b[j], b_in_b[j], True, idx_k_norm_g[j], idx_k_norm_b[j])
            w_out, sinks = w_out_b[j], no_sinks
        else:
            wp = _split_in_weights(w_in_a[j], b_in_a[j], False)
            w_out, sinks = w_out_a[j], sinks_a[j].astype(F32)
        pr = _inproj(h, wp, tab_q, tab_i, dsa=dsa, seq=seq)
        pm = _inproj(hm, wp, tab_qm, tab_im, dsa=dsa, seq=BLOCK)
        q, k, vt3 = pr[:3]
        qm, km, vtm = pm[:3]
        if dsa:
            qi, ki, wt = pr[3:]
            att = _dsa_attn(q, qi, wt, k.reshape(batch * nb, BLOCK, KV_COLS), vt3,
                            ki.reshape(batch * nb, BLOCK, IDX_DIM), km, vtm, batch=batch)
        else:
            att = _local_attn(sinks, q, k, vt3, km, vtm, banded=True, batch=batch)
        att_m = _local_attn(sinks, qm, km, vtm, km, vtm, banded=False, batch=1)

        w_out = w_out.astype(BF16)
        wr = jnp.pad(w_router[i].astype(BF16), ((0, 0), (0, LANES - N_EXPERTS)))
        br = row(jnp.pad(b_router[i].astype(F32), (0, LANES - N_EXPERTS), constant_values=NEG))
        g1, b1 = row(ln_mix_g[i]), row(ln_mix_b[i])
        h1, h1t, route, cnt = _outproj(att, w_out, h, g1, b1, wr, br, alpha=alpha)
        hm1, _, route_m, _ = _outproj(att_m, w_out, hm, g1, b1, wr, br, alpha=alpha)

        ep = _expert_weights(w_gate_up[i], b_gate_up[i], w_down[i], b_down[i])
        h, hm = _moe(h1, h1t, route, cnt, hm1, route_m, ep, row(ln_ffn_g[i]), row(ln_ffn_b[i]),
                     alpha=alpha)
    return h.reshape(batch, seq, d)
```

```python
import functools

import jax
import jax.numpy as jnp
import numpy as np
from jax import lax
from jax.experimental import pallas as pl
from jax.experimental.pallas import tpu as pltpu

N_META = 16
HEAD_DIM = 64
N_HEADS = 16
N_KV_HEADS = 2
GROUP = N_HEADS // N_KV_HEADS
Q_COLS = N_HEADS * HEAD_DIM
KV_COLS = N_KV_HEADS * HEAD_DIM
ROT_DIM = HEAD_DIM // 4
ROPE_THETA = 500000.0
ATTN_SCALE = HEAD_DIM ** -0.5
BLOCK = 128
IDX_HEADS = 8
IDX_DIM = 64
IDX_ROT_DIM = IDX_DIM // 2
IDX_SCALE = IDX_DIM ** -0.5
IDX_W_SCALE = IDX_HEADS ** -0.5
TOPK_MAX = 256
N_EXPERTS = 32
TOP_K = 4
SWIGLU_LIMIT = 7.0
SWIGLU_ALPHA = 1.702
LN_EPS = 1e-5
NEG = -1e30
INT_MIN = -2147483648

LANES = 128
TOKEN_ROWS = 8
ROW_TILE = 512
MOE_ROWS = 512
FF_CHUNK = 256
DMA_UNROLL = 8
IDX_CHUNK_BLOCKS = 4
ATT_CHUNK_BLOCKS = 4
V_ROWS = HEAD_DIM + 16
COL_TILE = 256
HEADS_PER_TILE = COL_TILE // BLOCK
N_COL_TILES = GROUP // HEADS_PER_TILE
PIPE = 8
VMEM_LIMIT = 48 * 1024 * 1024

F32 = jnp.float32
BF16 = jnp.bfloat16


def _cparams(*sem):
    return pltpu.CompilerParams(dimension_semantics=sem, vmem_limit_bytes=VMEM_LIMIT)


def _dot(a, b):
    return jnp.dot(a, b, preferred_element_type=F32)


def _dot_nt(a, b):
    return lax.dot_general(a, b, (((1,), (1,)), ((), ())), preferred_element_type=F32)


def _dot_tn(a, b):
    return lax.dot_general(a, b, (((0,), (0,)), ((), ())), preferred_element_type=F32)


def _layer_norm(z, g, b):
    mu = jnp.mean(z, axis=-1, keepdims=True)
    d = z - mu
    var = jnp.mean(d * d, axis=-1, keepdims=True)
    return d * lax.rsqrt(var + LN_EPS) * g + b


def _store_token_tiles(ref, x):
    for j in range(TOKEN_ROWS):
        ref[pl.ds(j, x.shape[0], stride=TOKEN_ROWS), :] = x[:, j * LANES:(j + 1) * LANES]


def _load_token_tiles(ref, rows):
    return jnp.concatenate([ref[pl.ds(j, rows, stride=TOKEN_ROWS), :] for j in range(TOKEN_ROWS)], axis=1)


def _rope128(blk, tab_ref, half):
    return (blk * tab_ref[0] + pltpu.roll(blk, half, 1) * tab_ref[1]
            + pltpu.roll(blk, LANES - half, 1) * tab_ref[2])


def _inproj_kernel(*refs, dsa):
    if dsa:
        (h_ref, wq_ref, wk_ref, wvt_ref, wqi_ref, wki_ref, wwt_ref, bq_ref, bk_ref, bvt_ref,
         bqi_ref, bki_ref, bwt_ref, tabq_ref, tabi_ref, g_ref, bn_ref,
         q_ref, k_ref, vt_ref, qi_ref, ki_ref, wt_ref) = refs
    else:
        h_ref, wq_ref, wk_ref, wvt_ref, bq_ref, bk_ref, bvt_ref, tabq_ref, q_ref, k_ref, vt_ref = refs
    x = h_ref[...].astype(BF16)
    q = _dot(x, wq_ref[...]) + bq_ref[...]
    for j in range(Q_COLS // LANES):
        sl = slice(j * LANES, (j + 1) * LANES)
        q_ref[:, sl] = _rope128(q[:, sl], tabq_ref, ROT_DIM // 2).astype(BF16)
    k = _dot(x, wk_ref[...]) + bk_ref[...]
    k_ref[...] = _rope128(k, tabq_ref, ROT_DIM // 2).astype(BF16)
    vt = _dot_nt(wvt_ref[...], x) + bvt_ref[...]
    for j in range(vt_ref.shape[0]):
        vt_ref[j] = vt[:, j * BLOCK:(j + 1) * BLOCK].astype(BF16)
    if not dsa:
        return
    qi = _dot(x, wqi_ref[...]) + bqi_ref[...]
    for j in range(IDX_HEADS * IDX_DIM // LANES):
        sl = slice(j * LANES, (j + 1) * LANES)
        qi_ref[:, sl] = _rope128(qi[:, sl], tabi_ref, IDX_ROT_DIM // 2).astype(BF16)
    ki = _dot(x, wki_ref[...]) + bki_ref[...]
    real = lax.broadcasted_iota(jnp.int32, ki.shape, 1) < IDX_DIM
    mu = jnp.sum(jnp.where(real, ki, 0.0), axis=-1, keepdims=True) * (1.0 / IDX_DIM)
    d = jnp.where(real, ki - mu, 0.0)
    var = jnp.sum(d * d, axis=-1, keepdims=True) * (1.0 / IDX_DIM)
    kn = d * lax.rsqrt(var + LN_EPS) * g_ref[...] + bn_ref[...]
    ki_ref[...] = _rope128(kn, tabi_ref, IDX_ROT_DIM // 2)[:, :IDX_DIM].astype(BF16)
    wt_ref[...] = (_dot_nt(wwt_ref[...], x) + bwt_ref[...]) * IDX_W_SCALE


def _rope_tables(pos, rot_dim):
    half = rot_dim // 2
    inv_freq = ROPE_THETA ** (-jnp.arange(half, dtype=F32) / half)
    ang = pos.astype(F32)[:, None] * inv_freq[None, :]
    cos, sin = jnp.cos(ang), jnp.sin(ang)
    n = pos.shape[0]
    rest = HEAD_DIM - rot_dim
    c = jnp.concatenate([cos, cos, jnp.ones((n, rest), F32)], 1)
    s_up = jnp.concatenate([jnp.zeros((n, half), F32), sin, jnp.zeros((n, rest), F32)], 1)
    s_lo = jnp.concatenate([-sin, jnp.zeros((n, HEAD_DIM - half), F32)], 1)
    return jnp.stack([jnp.tile(t, (1, LANES // HEAD_DIM)) for t in (c, s_up, s_lo)])


def _inproj(h, wp, tabq, tabi, *, dsa, seq):
    m, d = h.shape
    tm = min(ROW_TILE, m)
    nsb = seq // tm
    row = lambda i: (i, 0)
    const = lambda i: (0, 0)
    tab_map = lambda i: (0, i % nsb, 0)
    full = lambda a: pl.BlockSpec(a.shape, const)
    names = ("wq", "wk", "wvt", "wqi", "wki", "wwt", "bq", "bk", "bvt", "bqi", "bki", "bwt") if dsa \
        else ("wq", "wk", "wvt", "bq", "bk", "bvt")
    ins = [h] + [wp[k] for k in names] + [tabq]
    in_specs = [pl.BlockSpec((tm, d), row)] + [full(wp[k]) for k in names] \
        + [pl.BlockSpec((3, tm, LANES), tab_map)]
    out_shape = [jax.ShapeDtypeStruct((m, Q_COLS), BF16), jax.ShapeDtypeStruct((m, KV_COLS), BF16),
                 jax.ShapeDtypeStruct((m // BLOCK, KV_COLS, BLOCK), BF16)]
    out_specs = [pl.BlockSpec((tm, Q_COLS), row), pl.BlockSpec((tm, KV_COLS), row),
                 pl.BlockSpec((tm // BLOCK, KV_COLS, BLOCK), lambda i: (i, 0, 0))]
    if dsa:
        ins += [tabi, wp["g"], wp["bn"]]
        in_specs += [pl.BlockSpec((3, tm, LANES), tab_map), full(wp["g"]), full(wp["bn"])]
        out_shape += [jax.ShapeDtypeStruct((m, IDX_HEADS * IDX_DIM), BF16),
                      jax.ShapeDtypeStruct((m, IDX_DIM), BF16),
                      jax.ShapeDtypeStruct((IDX_HEADS, m), F32)]
        out_specs += [pl.BlockSpec((tm, IDX_HEADS * IDX_DIM), row),
                      pl.BlockSpec((tm, IDX_DIM), row),
                      pl.BlockSpec((IDX_HEADS, tm), lambda i: (0, i))]
    return pl.pallas_call(
        functools.partial(_inproj_kernel, dsa=dsa), grid=(m // tm,), in_specs=in_specs,
        out_specs=out_specs, out_shape=out_shape, compiler_params=_cparams("parallel"),
        name="inproj_dsa" if dsa else "inproj_swa")(*ins)


def _stack_heads(ref, first, count, width=HEAD_DIM, scale=None):
    parts = [ref[:, (first + g) * width:(first + g + 1) * width] for g in range(count)]
    x = jnp.concatenate(parts, axis=0)
    return x if scale is None else x * scale


def _store_tile_t(o_ref, first_head, acc):
    out_t = acc[:HEAD_DIM] / acc[HEAD_DIM:HEAD_DIM + 1]
    for g in range(HEADS_PER_TILE):
        h = first_head + g
        o_ref[h * HEAD_DIM:(h + 1) * HEAD_DIM, :] = out_t[:, g * BLOCK:(g + 1) * BLOCK].astype(BF16)


def _values_with_ones(vt, kh):
    hs = slice(kh * HEAD_DIM, (kh + 1) * HEAD_DIM)
    return jnp.concatenate([vt[hs, :], jnp.ones((V_ROWS - HEAD_DIM, vt.shape[1]), BF16)], axis=0)


def _mask_bias(sel):
    bias = jnp.where(sel, 0.0, NEG).astype(BF16)
    return jnp.concatenate([bias] * HEADS_PER_TILE, axis=1)


def _local_attn_kernel(sink_ref, q_ref, kc_ref, kp_ref, km_ref, vtc_ref, vtp_ref, vtm_ref, o_ref,
                       *, banded):
    n = pl.program_id(1)
    r = lax.broadcasted_iota(jnp.int32, (BLOCK, BLOCK), 0)
    c = lax.broadcasted_iota(jnp.int32, (BLOCK, BLOCK), 1)
    blocks = [(kc_ref, vtc_ref, _mask_bias(r <= c))]
    if banded:
        blocks.append((kp_ref, vtp_ref, _mask_bias((r > c) & (n > 0))))
        blocks.append((km_ref, vtm_ref, _mask_bias(r < N_META)))
    sink_row = lax.broadcasted_iota(jnp.int32, (V_ROWS, COL_TILE), 0) == HEAD_DIM
    tiles = [(kh, kh * GROUP + ct * HEADS_PER_TILE)
             for kh in range(N_KV_HEADS) for ct in range(N_COL_TILES)]
    ks = [[k_ref[:, kh * HEAD_DIM:(kh + 1) * HEAD_DIM] for k_ref, _, _ in blocks]
          for kh in range(N_KV_HEADS)]
    vs = [[_values_with_ones(vt_ref[...], kh) for _, vt_ref, _ in blocks] for kh in range(N_KV_HEADS)]

    def logits(i):
        kh, first = tiles[i]
        qs = _stack_heads(q_ref, first, HEADS_PER_TILE, scale=ATTN_SCALE)
        return [_dot_nt(k, qs) for k in ks[kh]]

    sss = [logits(i) for i in range(PIPE)]
    for i, (kh, first) in enumerate(tiles):
        sink = jnp.concatenate(
            [jnp.full((1, BLOCK), sink_ref[first + g], F32) for g in range(HEADS_PER_TILE)], axis=1)
        ss = [s.astype(BF16) + bias for s, (_, _, bias) in zip(sss[i], blocks)]
        m = sink
        for s in ss:
            m = jnp.maximum(m, jnp.max(s, axis=0, keepdims=True).astype(F32))
        mb = m.astype(BF16)
        ps = [jnp.exp(s - mb) for s in ss]
        if i + PIPE < len(tiles):
            sss.append(logits(i + PIPE))
        acc = jnp.where(sink_row, jnp.exp(sink - mb.astype(F32)), 0.0)
        for p, v in zip(ps, vs[kh]):
            acc = acc + _dot(v, p)
        _store_tile_t(o_ref, first, acc)


def _local_attn(sinks, q, k, vt3, km, vtm, *, banded, batch):
    m = q.shape[0]
    nb = m // BLOCK // batch
    cur = lambda b, n, s: (b * nb + n, 0)
    prev = lambda b, n, s: (b * nb + jnp.maximum(n - 1, 0), 0)
    cur3 = lambda b, n, s: (b * nb + n, 0, 0)
    prev3 = lambda b, n, s: (b * nb + jnp.maximum(n - 1, 0), 0, 0)
    grid_spec = pltpu.PrefetchScalarGridSpec(
        num_scalar_prefetch=1, grid=(batch, nb),
        in_specs=[pl.BlockSpec((BLOCK, Q_COLS), cur), pl.BlockSpec((BLOCK, KV_COLS), cur),
                  pl.BlockSpec((BLOCK, KV_COLS), prev),
                  pl.BlockSpec((BLOCK, KV_COLS), lambda b, n, s: (0, 0)),
                  pl.BlockSpec((None, KV_COLS, BLOCK), cur3), pl.BlockSpec((None, KV_COLS, BLOCK), prev3),
                  pl.BlockSpec((None, KV_COLS, BLOCK), lambda b, n, s: (0, 0, 0))],
        out_specs=pl.BlockSpec((None, Q_COLS, BLOCK), cur3))
    return pl.pallas_call(
        functools.partial(_local_attn_kernel, banded=banded), grid_spec=grid_spec,
        out_shape=jax.ShapeDtypeStruct((m // BLOCK, Q_COLS, BLOCK), BF16),
        compiler_params=_cparams("parallel", "parallel"),
        name="swa_attn" if banded else "meta_attn")(sinks, q, k, k, km, vt3, vt3, vtm)


def _sortable(x):
    bits = lax.bitcast_convert_type(x, jnp.int32)
    return jnp.where(bits < 0, bits ^ jnp.int32(0x7FFFFFFF), bits)


def _dsa_kernel(q_ref, qi_ref, wt_ref, k3_ref, vt3_ref, ki3_ref, km_ref, vtm_ref, o_ref,
                key_s, acc_s, *, ksel, idx_bits):
    n = pl.program_id(1)
    cb = IDX_CHUNK_BLOCKS
    ck = cb * BLOCK
    n_chunks = n // cb + 1
    lane = lax.broadcasted_iota(jnp.int32, (1, BLOCK), 1)
    t_row = n * BLOCK + lane

    qis = _stack_heads(qi_ref, 0, IDX_HEADS, IDX_DIM)
    wts = [wt_ref[h:h + 1, :] for h in range(IDX_HEADS)]

    def score_chunk(c, carry):
        kic = ki3_ref[pl.ds(c * cb, cb)].reshape(ck, IDX_DIM)
        d = _dot_nt(kic, qis)
        sc = jnp.zeros((ck, BLOCK), F32)
        for h in range(IDX_HEADS):
            sc = sc + jnp.maximum(d[:, h * BLOCK:(h + 1) * BLOCK], 0.0) * wts[h]
        sc = sc * IDX_SCALE
        s_idx = c * ck + lax.broadcasted_iota(jnp.int32, (ck, BLOCK), 0)
        key = jnp.where(s_idx <= t_row, _sortable(sc), INT_MIN)
        key_s[pl.ds(c * cb, cb)] = key.reshape(cb, BLOCK, BLOCK)
        return carry

    lax.fori_loop(0, n_chunks, score_chunk, 0)

    def count(pred):
        def body(c, acc):
            keys = key_s[pl.ds(c * cb, cb)].reshape(ck, BLOCK)
            s_idx = c * ck + lax.broadcasted_iota(jnp.int32, (ck, BLOCK), 0)
            hit = jnp.where(pred(keys, s_idx), 1, 0).astype(jnp.int32)
            return acc + jnp.sum(hit.reshape(ck // 8, 8, BLOCK), axis=0)
        acc = lax.fori_loop(0, n_chunks, body, jnp.zeros((8, BLOCK), jnp.int32))
        return jnp.sum(acc, axis=0, keepdims=True)

    def thr_bit(i, tu):
        cand_u = tu | lax.shift_left(jnp.int32(1), 31 - i)
        cand = cand_u ^ jnp.int32(INT_MIN)
        return jnp.where(count(lambda k, s: k >= cand) >= ksel, cand_u, tu)

    thr_u = lax.fori_loop(0, 32, thr_bit, jnp.zeros((1, BLOCK), jnp.int32))
    thr = thr_u ^ jnp.int32(INT_MIN)
    n_gt = count(lambda k, s: k > thr)
    n_ge = count(lambda k, s: k >= thr)
    need = ksel - n_gt

    def tie_bit(i, x):
        cand = x | lax.shift_left(jnp.int32(1), idx_bits - 1 - i)
        return jnp.where(count(lambda k, s: (k == thr) & (s < cand)) <= need - 1, cand, x)

    def tie_search():
        return lax.fori_loop(0, idx_bits, tie_bit, jnp.zeros((1, BLOCK), jnp.int32))

    excess = jnp.max(jnp.where((n_ge > ksel) & (thr_u != 0), 1, 0))
    tie_x = lax.cond(excess > 0, tie_search,
                     lambda: jnp.full((1, BLOCK), (1 << idx_bits) - 1, jnp.int32))

    tiles = [(kh, kh * GROUP + ct * HEADS_PER_TILE)
             for kh in range(N_KV_HEADS) for ct in range(N_COL_TILES)]
    qs = [_stack_heads(q_ref, first, HEADS_PER_TILE, scale=ATTN_SCALE) for _, first in tiles]
    krow = lax.broadcasted_iota(jnp.int32, (BLOCK, BLOCK), 0)
    bias_m = _mask_bias(krow < N_META)
    sm = [_dot_nt(km_ref[:, kh * HEAD_DIM:(kh + 1) * HEAD_DIM], qs[i]) for i, (kh, _) in enumerate(tiles)]
    vm = [_values_with_ones(vtm_ref[...], kh) for kh in range(N_KV_HEADS)]
    m0 = []
    for i, (kh, _) in enumerate(tiles):
        s = sm[i].astype(BF16) + bias_m
        m = jnp.max(s, axis=0, keepdims=True)
        m0.append(m)
        acc_s[i] = _dot(vm[kh], jnp.exp(s - m))

    ab = ATT_CHUNK_BLOCKS
    ak = ab * BLOCK
    arow = lax.broadcasted_iota(jnp.int32, (ak, BLOCK), 0)

    def attend(c, ms):
        keys = key_s[pl.ds(c * ab, ab)].reshape(ak, BLOCK)
        s_idx = c * ak + arow
        sel = ((keys > thr) | ((keys == thr) & (s_idx <= tie_x))) & (s_idx <= t_row)
        bias = _mask_bias(sel)
        kc = k3_ref[pl.ds(c * ab, ab)].reshape(ak, KV_COLS)
        vtc = jnp.concatenate([vt3_ref[c * ab + g] for g in range(ab)], axis=1)
        vs = [_values_with_ones(vtc, kh) for kh in range(N_KV_HEADS)]
        ks = [kc[:, kh * HEAD_DIM:(kh + 1) * HEAD_DIM] for kh in range(N_KV_HEADS)]
        logits = lambda i: _dot_nt(ks[tiles[i][0]], qs[i])
        ss = [logits(i) for i in range(PIPE)]
        new_m = []
        for i, (kh, _) in enumerate(tiles):
            s = ss[i].astype(BF16) + bias
            m = jnp.maximum(ms[i], jnp.max(s, axis=0, keepdims=True))
            alpha = jnp.exp(ms[i].astype(F32) - m.astype(F32))
            p = jnp.exp(s - m)
            new_m.append(m)
            if i + PIPE < len(tiles):
                ss.append(logits(i + PIPE))
            acc_s[i] = acc_s[i] * alpha + _dot(vs[kh], p)
        return tuple(new_m)

    lax.fori_loop(0, n // ab + 1, attend, tuple(m0))
    for i, (_, first) in enumerate(tiles):
        _store_tile_t(o_ref, first, acc_s[i])


def _dsa_attn(q, qi, wt, k3, vt3, ki3, km, vtm, *, batch):
    m = q.shape[0]
    nb = m // BLOCK // batch
    seq = nb * BLOCK
    assert nb % IDX_CHUNK_BLOCKS == 0 and IDX_CHUNK_BLOCKS % ATT_CHUNK_BLOCKS == 0
    ksel = min(TOPK_MAX, seq // 4)
    idx_bits = max(1, int(np.ceil(np.log2(seq))))
    qmap = lambda b, n: (b * nb + n, 0)
    bmap = lambda b, n: (b, 0, 0)
    return pl.pallas_call(
        functools.partial(_dsa_kernel, ksel=ksel, idx_bits=idx_bits), grid=(batch, nb),
        in_specs=[pl.BlockSpec((BLOCK, Q_COLS), qmap),
                  pl.BlockSpec((BLOCK, IDX_HEADS * IDX_DIM), qmap),
                  pl.BlockSpec((IDX_HEADS, BLOCK), lambda b, n: (0, b * nb + n)),
                  pl.BlockSpec((nb, BLOCK, KV_COLS), bmap),
                  pl.BlockSpec((nb, KV_COLS, BLOCK), bmap),
                  pl.BlockSpec((nb, BLOCK, IDX_DIM), bmap),
                  pl.BlockSpec((BLOCK, KV_COLS), lambda b, n: (0, 0)),
                  pl.BlockSpec((None, KV_COLS, BLOCK), lambda b, n: (0, 0, 0))],
        out_specs=pl.BlockSpec((None, Q_COLS, BLOCK), lambda b, n: (b * nb + n, 0, 0)),
        out_shape=jax.ShapeDtypeStruct((m // BLOCK, Q_COLS, BLOCK), BF16),
        scratch_shapes=[pltpu.VMEM((nb, BLOCK, BLOCK), jnp.int32),
                        pltpu.VMEM((N_KV_HEADS * N_COL_TILES, V_ROWS, COL_TILE), F32)],
        compiler_params=_cparams("parallel", "arbitrary"), name="dsa_attn",
    )(q, qi, wt, k3, vt3, ki3, km, vtm)


def _outproj_kernel(a_ref, w_ref, h_ref, g_ref, b_ref, wr_ref, br_ref, tri_ref,
                    h1_ref, h1t_ref, route_ref, cnt_ref, carry_s, *, alpha):
    i = pl.program_id(0)

    @pl.when(i == 0)
    def _():
        carry_s[...] = jnp.zeros_like(carry_s)

    y = jnp.concatenate([_dot_tn(a_ref[j], w_ref[...]) for j in range(a_ref.shape[0])], axis=0)
    h1 = _layer_norm(alpha * h_ref[...] + y, g_ref[...], b_ref[...])
    h1_ref[...] = h1
    _store_token_tiles(h1t_ref, h1)
    logits = _dot(h1.astype(BF16), wr_ref[...]) + br_ref[...]
    lane = lax.broadcasted_iota(jnp.int32, logits.shape, 1)
    idxs, vals = [], []
    rem = logits
    for _ in range(TOP_K):
        mx = jnp.max(rem, axis=-1, keepdims=True)
        ix = jnp.min(jnp.where(rem == mx, lane, LANES), axis=-1, keepdims=True)
        idxs.append(ix)
        vals.append(mx)
        rem = jnp.where(lane == ix, -jnp.inf, rem)
    exps = [jnp.exp(v - vals[0]) for v in vals]
    den = exps[0] + exps[1] + exps[2] + exps[3]
    sel = jnp.where(rem == -jnp.inf, 1.0, 0.0)
    before = _dot(tri_ref[...], sel.astype(BF16)) + carry_s[...]
    route = jnp.zeros(logits.shape, F32)
    for k in range(TOP_K):
        rank = jnp.sum(jnp.where(lane == idxs[k], before, 0.0), axis=-1, keepdims=True)
        route = jnp.where(lane == k, idxs[k].astype(F32), route)
        route = jnp.where(lane == TOP_K + k, exps[k] / den, route)
        route = jnp.where(lane == 2 * TOP_K + k, rank, route)
    route_ref[...] = route
    carry_s[...] = carry_s[...] + jnp.sum(sel, axis=0, keepdims=True)
    cnt_ref[...] = carry_s[...]


def _outproj(a_t, w, h, g, b, wr, br, *, alpha):
    m, d = h.shape
    tm = min(ROW_TILE, m)
    tri = jnp.tril(jnp.ones((tm, tm), BF16), -1)
    row = lambda i: (i, 0)
    const = lambda i: (0, 0)
    full = lambda x: pl.BlockSpec(x.shape, const)
    return pl.pallas_call(
        functools.partial(_outproj_kernel, alpha=alpha), grid=(m // tm,),
        in_specs=[pl.BlockSpec((tm // BLOCK, Q_COLS, BLOCK), lambda i: (i, 0, 0)), full(w),
                  pl.BlockSpec((tm, d), row), full(g), full(b), full(wr), full(br), full(tri)],
        out_specs=[pl.BlockSpec((tm, d), row), pl.BlockSpec((tm * TOKEN_ROWS, LANES), row),
                   pl.BlockSpec((tm, LANES), row), pl.BlockSpec((1, LANES), const)],
        out_shape=[jax.ShapeDtypeStruct((m, d), F32), jax.ShapeDtypeStruct((m * TOKEN_ROWS, LANES), F32),
                   jax.ShapeDtypeStruct((m, LANES), F32), jax.ShapeDtypeStruct((1, LANES), F32)],
        scratch_shapes=[pltpu.VMEM((1, LANES), F32)],
        compiler_params=_cparams("arbitrary"), name="outproj_router")(a_t, w, h, g, b, wr, br, tri)


def _expert_ffn(x, wgu_ref, bgu_ref, wd_ref, bd_ref):
    ff = wd_ref.shape[0]
    acc = jnp.zeros((x.shape[0], wd_ref.shape[1]), F32)
    for c in range(ff // FF_CHUNK):
        gs = slice(c * FF_CHUNK, (c + 1) * FF_CHUNK)
        us = slice(ff + c * FF_CHUNK, ff + (c + 1) * FF_CHUNK)
        gate = jnp.minimum(_dot(x, wgu_ref[:, gs]) + bgu_ref[:, gs], SWIGLU_LIMIT)
        up = jnp.clip(_dot(x, wgu_ref[:, us]) + bgu_ref[:, us], -SWIGLU_LIMIT, SWIGLU_LIMIT)
        act = gate * jax.nn.sigmoid(SWIGLU_ALPHA * gate) * (up + 1.0)
        acc = acc + _dot(act.astype(BF16), wd_ref[gs, :])
    return acc + bd_ref[...]


def _experts_kernel(be_ref, nu_ref, tok_ref, tok_next_ref, dst_ref, h_ref, wgu_ref, bgu_ref, wd_ref,
                    bd_ref, yk_ref, xbuf, ybuf, gsem, ssem):
    del be_ref
    i = pl.program_id(0)
    n_used = nu_ref[0]
    slot = i % 2
    rows = xbuf.shape[1] // TOKEN_ROWS
    tile = lambda t: pl.ds(pl.multiple_of(t * TOKEN_ROWS, TOKEN_ROWS), TOKEN_ROWS)

    def gather(src_ref, s):
        def body(r8, c):
            for u in range(DMA_UNROLL):
                r = r8 * DMA_UNROLL + u
                pltpu.make_async_copy(h_ref.at[tile(src_ref[0, 0, r])],
                                      xbuf.at[s, tile(r)], gsem.at[s]).start()
            return c
        lax.fori_loop(0, rows // DMA_UNROLL, body, 0)

    def scatter(s):
        def body(r8, c):
            for u in range(DMA_UNROLL):
                r = r8 * DMA_UNROLL + u
                pltpu.make_async_copy(ybuf.at[s, tile(r)],
                                      yk_ref.at[tile(dst_ref[0, 0, r])], ssem.at[s]).start()
            return c
        lax.fori_loop(0, rows // DMA_UNROLL, body, 0)

    def wait_gather(s):
        pltpu.make_async_copy(xbuf.at[s], xbuf.at[s], gsem.at[s]).wait()

    def wait_scatter(s):
        pltpu.make_async_copy(ybuf.at[s], ybuf.at[s], ssem.at[s]).wait()

    @pl.when(i == 0)
    def _():
        for s in range(2):
            ybuf[s] = jnp.zeros(ybuf.shape[1:], F32)
        fills = [pltpu.make_async_copy(
            ybuf.at[s], yk_ref.at[pl.ds(yk_ref.shape[0] - (2 - s) * ybuf.shape[1], ybuf.shape[1])],
            ssem.at[s]) for s in range(2)]
        for f in fills:
            f.start()
        for f in fills:
            f.wait()

    @pl.when((i == 0) & (n_used > 0))
    def _():
        gather(tok_ref, 0)

    @pl.when(i < n_used)
    def _():
        wait_gather(slot)

        @pl.when(i + 1 < n_used)
        def _():
            gather(tok_next_ref, 1 - slot)

        @pl.when(i >= 2)
        def _():
            wait_scatter(slot)

        x = _load_token_tiles(xbuf.at[slot], rows).astype(BF16)
        _store_token_tiles(ybuf.at[slot], _expert_ffn(x, wgu_ref, bgu_ref, wd_ref, bd_ref))
        scatter(slot)

    @pl.when(i == pl.num_programs(0) - 1)
    def _():
        @pl.when(n_used >= 1)
        def _():
            wait_scatter((n_used - 1) % 2)

        @pl.when(n_used >= 2)
        def _():
            wait_scatter(n_used % 2)


def _experts(blk_e, n_used, tok3, dst3, ht, wgu, bgu, wd, bd, n_out_rows):
    n_blk, _, rows = tok3.shape
    ff, d = wd.shape[1:]
    assert d == TOKEN_ROWS * LANES
    emap = lambda i, be, nu: (be[i], 0, 0)
    smem = lambda imap: pl.BlockSpec((1, 1, rows), imap, memory_space=pltpu.SMEM)
    grid_spec = pltpu.PrefetchScalarGridSpec(
        num_scalar_prefetch=2, grid=(n_blk,),
        in_specs=[smem(lambda i, be, nu: (i, 0, 0)),
                  smem(lambda i, be, nu: (jnp.minimum(i + 1, n_blk - 1), 0, 0)),
                  smem(lambda i, be, nu: (i, 0, 0)),
                  pl.BlockSpec(memory_space=pl.ANY),
                  pl.BlockSpec((None, d, 2 * ff), emap), pl.BlockSpec((None, 1, 2 * ff), emap),
                  pl.BlockSpec((None, ff, d), emap), pl.BlockSpec((None, 1, d), emap)],
        out_specs=pl.BlockSpec(memory_space=pl.ANY),
        scratch_shapes=[pltpu.VMEM((2, rows * TOKEN_ROWS, LANES), F32),
                        pltpu.VMEM((2, rows * TOKEN_ROWS, LANES), F32),
                        pltpu.SemaphoreType.DMA((2,)), pltpu.SemaphoreType.DMA((2,))])
    return pl.pallas_call(
        _experts_kernel, grid_spec=grid_spec,
        out_shape=jax.ShapeDtypeStruct((n_out_rows * TOKEN_ROWS, LANES), F32),
        compiler_params=_cparams("arbitrary"), name="moe_experts",
    )(blk_e, n_used, tok3, tok3, dst3, ht, wgu, bgu, wd, bd)


def _combine_kernel(route_ref, h_ref, g_ref, b_ref, y0_ref, y1_ref, y2_ref, y3_ref, o_ref, *, alpha):
    z = alpha * h_ref[...]
    for k, y_ref in enumerate((y0_ref, y1_ref, y2_ref, y3_ref)):
        z = z + route_ref[:, TOP_K + k:TOP_K + k + 1] * _load_token_tiles(y_ref, h_ref.shape[0])
    o_ref[...] = _layer_norm(z, g_ref[...], b_ref[...])


def _combine(route, h, g, b, yk, *, alpha):
    m, d = h.shape
    tm = min(ROW_TILE, m)
    nt = m // tm
    row = lambda i: (i, 0)
    const = lambda i: (0, 0)
    choice = lambda k: pl.BlockSpec((tm * TOKEN_ROWS, LANES), lambda i: (k * nt + i, 0))
    return pl.pallas_call(
        functools.partial(_combine_kernel, alpha=alpha), grid=(nt,),
        in_specs=[pl.BlockSpec((tm, LANES), row), pl.BlockSpec((tm, d), row),
                  pl.BlockSpec(g.shape, const), pl.BlockSpec(b.shape, const)]
                 + [choice(k) for k in range(TOP_K)],
        out_specs=pl.BlockSpec((tm, d), row),
        out_shape=jax.ShapeDtypeStruct((m, d), F32),
        compiler_params=_cparams("parallel"), name="moe_combine")(route, h, g, b, yk, yk, yk, yk)


def _meta_moe_kernel(hm_ref, gm_ref, wgu_ref, bgu_ref, wd_ref, bd_ref, g_ref, b_ref, o_ref, acc_s,
                     *, alpha):
    e = pl.program_id(0)

    @pl.when(e == 0)
    def _():
        acc_s[...] = jnp.zeros_like(acc_s)

    y = _expert_ffn(hm_ref[...].astype(BF16), wgu_ref, bgu_ref, wd_ref, bd_ref)
    lane = lax.broadcasted_iota(jnp.int32, gm_ref.shape, 1)
    ge = jnp.sum(jnp.where(lane == e, gm_ref[...], 0.0), axis=-1, keepdims=True)
    acc_s[...] = acc_s[...] + ge * y

    @pl.when(e == pl.num_programs(0) - 1)
    def _():
        o_ref[...] = _layer_norm(alpha * hm_ref[...] + acc_s[...], g_ref[...], b_ref[...])


def _meta_moe(hm, gm, wgu, bgu, wd, bd, g, b, *, alpha):
    m, d = hm.shape
    ff = wd.shape[1]
    const = lambda e: (0, 0)
    emap = lambda e: (e, 0, 0)
    return pl.pallas_call(
        functools.partial(_meta_moe_kernel, alpha=alpha), grid=(wgu.shape[0],),
        in_specs=[pl.BlockSpec((m, d), const), pl.BlockSpec((m, LANES), const),
                  pl.BlockSpec((None, d, 2 * ff), emap), pl.BlockSpec((None, 1, 2 * ff), emap),
                  pl.BlockSpec((None, ff, d), emap), pl.BlockSpec((None, 1, d), emap),
                  pl.BlockSpec(g.shape, const), pl.BlockSpec(b.shape, const)],
        out_specs=pl.BlockSpec((m, d), const),
        out_shape=jax.ShapeDtypeStruct((m, d), F32),
        scratch_shapes=[pltpu.VMEM((m, d), F32)],
        compiler_params=_cparams("arbitrary"), name="meta_experts")(hm, gm, wgu, bgu, wd, bd, g, b)


def _moe(h1, h1t, route, cnt, hm1, route_m, ep, g, b, *, alpha):
    m, d = h1.shape
    n_blk = -(-m * TOP_K // MOE_ROWS) + N_EXPERTS
    e = route[:, :TOP_K].astype(jnp.int32)
    rank = route[:, 2 * TOP_K:3 * TOP_K].astype(jnp.int32)
    counts = cnt[0, :N_EXPERTS].astype(jnp.int32)
    padded = (counts + MOE_ROWS - 1) // MOE_ROWS * MOE_ROWS
    pend = jnp.cumsum(padded)
    dest = (pend - padded)[e] + rank
    blk_start = jnp.arange(n_blk, dtype=jnp.int32) * MOE_ROWS
    blk_e = jnp.minimum(jnp.sum((pend[None, :] <= blk_start[:, None]).astype(jnp.int32), axis=1),
                        N_EXPERTS - 1)
    n_used = (pend[-1:] // MOE_ROWS).astype(jnp.int32)
    n_sorted = n_blk * MOE_ROWS
    inv = jnp.zeros((n_sorted,), jnp.int32).at[dest.reshape(-1)].set(
        jnp.arange(1, m * TOP_K + 1, dtype=jnp.int32))
    pair = jnp.maximum(inv - 1, 0)
    pos = jnp.arange(n_sorted, dtype=jnp.int32)
    pad_row = TOP_K * m + (pos // MOE_ROWS % 2) * MOE_ROWS + pos % MOE_ROWS
    tok3 = (pair // TOP_K).reshape(n_blk, 1, MOE_ROWS)
    dst3 = jnp.where(inv > 0, (pair % TOP_K) * m + pair // TOP_K, pad_row).reshape(n_blk, 1, MOE_ROWS)
    yk = _experts(blk_e, n_used, tok3, dst3, h1t, ep["wgu"], ep["bgu"], ep["wd"], ep["bd"],
                  TOP_K * m + 2 * MOE_ROWS)
    h2 = _combine(route, h1, g, b, yk, alpha=alpha)
    em = route_m[:, :TOP_K].astype(jnp.int32)
    gm = jnp.sum(jax.nn.one_hot(em, LANES, dtype=F32) * route_m[:, TOP_K:2 * TOP_K, None], axis=1)
    hm2 = _meta_moe(hm1, gm, ep["wgu"], ep["bgu"], ep["wd"], ep["bd"], g, b, alpha=alpha)
    return h2, hm2


def _split_in_weights(w, b, dsa, g=None, bn=None):
    cuts = np.cumsum([Q_COLS, KV_COLS, KV_COLS, IDX_HEADS * IDX_DIM, IDX_DIM]).tolist()
    wb = w.astype(BF16)
    row = lambda v: v.reshape(1, -1)
    col = lambda v: v.reshape(-1, 1)
    wp = {"wq": wb[:, :cuts[0]], "bq": row(b[:cuts[0]]),
          "wk": wb[:, cuts[0]:cuts[1]], "bk": row(b[cuts[0]:cuts[1]]),
          "wvt": wb[:, cuts[1]:cuts[2]].T, "bvt": col(b[cuts[1]:cuts[2]])}
    if dsa:
        pad = LANES - IDX_DIM
        wp.update({
            "wqi": wb[:, cuts[2]:cuts[3]], "bqi": row(b[cuts[2]:cuts[3]]),
            "wki": jnp.pad(wb[:, cuts[3]:cuts[4]], ((0, 0), (0, pad))),
            "bki": row(jnp.pad(b[cuts[3]:cuts[4]], (0, pad))),
            "wwt": wb[:, cuts[4]:].T, "bwt": col(b[cuts[4]:]),
            "g": row(jnp.pad(g, (0, pad))), "bn": row(jnp.pad(bn, (0, pad)))})
    return wp


def kernel(x, meta_tokens, w_in_a, b_in_a, sinks_a, w_out_a, w_in_b, b_in_b, idx_k_norm_g, idx_k_norm_b, w_out_b, ln_mix_g, ln_mix_b, w_router, b_router, w_gate_up, b_gate_up, w_down, b_down, ln_ffn_g, ln_ffn_b):
    batch, seq, d = x.shape
    depth = ln_mix_g.shape[0]
    alpha = float((2 * depth) ** 0.25)
    row = lambda v: v.reshape(1, -1)
    nb = seq // BLOCK

    h = x.reshape(batch * seq, d)
    hm = jnp.zeros((BLOCK, d), F32).at[:N_META].set(meta_tokens.astype(F32))
    tab_q = _rope_tables(jnp.arange(N_META, N_META + seq), ROT_DIM)
    tab_i = _rope_tables(jnp.arange(N_META, N_META + seq), IDX_ROT_DIM)
    tab_qm = _rope_tables(jnp.arange(BLOCK), ROT_DIM)
    tab_im = _rope_tables(jnp.arange(BLOCK), IDX_ROT_DIM)
    no_sinks = jnp.full((N_HEADS,), NEG, F32)

    for i in range(depth):
        j = i // 2
        dsa = i % 2 == 1
        if dsa:
            wp = _split_in_weights(w_in_b[j], b_in_b[j], True, idx_k_norm_g[j], idx_k_norm_b[j])
            w_out, sinks = w_out_b[j], no_sinks
        else:
            wp = _split_in_weights(w_in_a[j], b_in_a[j], False)
            w_out, sinks = w_out_a[j], sinks_a[j].astype(F32)
        pr = _inproj(h, wp, tab_q, tab_i, dsa=dsa, seq=seq)
        pm = _inproj(hm, wp, tab_qm, tab_im, dsa=dsa, seq=BLOCK)
        q, k, vt3 = pr[:3]
        qm, km, vtm = pm[:3]
        if dsa:
            qi, ki, wt = pr[3:]
            att = _dsa_attn(q, qi, wt, k.reshape(batch * nb, BLOCK, KV_COLS), vt3,
                            ki.reshape(batch * nb, BLOCK, IDX_DIM), km, vtm, batch=batch)
        else:
            att = _local_attn(sinks, q, k, vt3, km, vtm, banded=True, batch=batch)
        att_m = _local_attn(sinks, qm, km, vtm, km, vtm, banded=False, batch=1)

        w_out = w_out.astype(BF16)
        wr = jnp.pad(w_router[i].astype(BF16), ((0, 0), (0, LANES - N_EXPERTS)))
        br = row(jnp.pad(b_router[i].astype(F32), (0, LANES - N_EXPERTS), constant_values=NEG))
        g1, b1 = row(ln_mix_g[i]), row(ln_mix_b[i])
        h1, h1t, route, cnt = _outproj(att, w_out, h, g1, b1, wr, br, alpha=alpha)
        hm1, _, route_m, _ = _outproj(att_m, w_out, hm, g1, b1, wr, br, alpha=alpha)

        ep = {"wgu": w_gate_up[i].astype(BF16), "bgu": b_gate_up[i][:, None, :],
              "wd": w_down[i].astype(BF16), "bd": b_down[i][:, None, :]}
        h, hm = _moe(h1, h1t, route, cnt, hm1, route_m, ep, row(ln_ffn_g[i]), row(ln_ffn_b[i]),
                     alpha=alpha)
    return h.reshape(batch, seq, d)
```

```python
import functools

import jax
import jax.numpy as jnp
import numpy as np
from jax import lax
from jax.experimental import pallas as pl
from jax.experimental.pallas import tpu as pltpu

N_META = 16
HEAD_DIM = 64
N_HEADS = 16
N_KV_HEADS = 2
GROUP = N_HEADS // N_KV_HEADS
Q_COLS = N_HEADS * HEAD_DIM
KV_COLS = N_KV_HEADS * HEAD_DIM
ROT_DIM = HEAD_DIM // 4
ROPE_THETA = 500000.0
ATTN_SCALE = HEAD_DIM ** -0.5
BLOCK = 128
IDX_HEADS = 8
IDX_DIM = 64
IDX_ROT_DIM = IDX_DIM // 2
IDX_SCALE = IDX_DIM ** -0.5
IDX_W_SCALE = IDX_HEADS ** -0.5
TOPK_MAX = 256
N_EXPERTS = 32
TOP_K = 4
SWIGLU_LIMIT = 7.0
SWIGLU_ALPHA = 1.702
LN_EPS = 1e-5
NEG = -1e30
INT_MIN = -2147483648

LANES = 128
TOKEN_ROWS = 8
ROW_TILE = 512
MOE_ROWS = 512
FF_CHUNK = 256
DMA_UNROLL = 8
OUTPROJ_GROUPS = 2
IDX_CHUNK_BLOCKS = 4
ATT_CHUNK_BLOCKS = 4
V_ROWS = HEAD_DIM + 16
COL_TILE = 256
HEADS_PER_TILE = COL_TILE // BLOCK
N_COL_TILES = GROUP // HEADS_PER_TILE
PIPE = 8
VMEM_LIMIT = 48 * 1024 * 1024

F32 = jnp.float32
BF16 = jnp.bfloat16


def _cparams(*sem):
    return pltpu.CompilerParams(dimension_semantics=sem, vmem_limit_bytes=VMEM_LIMIT)


def _dot(a, b):
    return jnp.dot(a, b, preferred_element_type=F32)


def _dot_nt(a, b):
    return lax.dot_general(a, b, (((1,), (1,)), ((), ())), preferred_element_type=F32)


def _dot_tn(a, b):
    return lax.dot_general(a, b, (((0,), (0,)), ((), ())), preferred_element_type=F32)


def _layer_norm(z, g, b):
    mu = jnp.mean(z, axis=-1, keepdims=True)
    d = z - mu
    var = jnp.mean(d * d, axis=-1, keepdims=True)
    return d * lax.rsqrt(var + LN_EPS) * g + b


def _store_token_tiles(ref, x):
    for j in range(TOKEN_ROWS):
        ref[pl.ds(j, x.shape[0], stride=TOKEN_ROWS), :] = x[:, j * LANES:(j + 1) * LANES]


def _load_token_tiles(ref, rows):
    return jnp.concatenate([ref[pl.ds(j, rows, stride=TOKEN_ROWS), :] for j in range(TOKEN_ROWS)], axis=1)


def _rope128(blk, tab_ref, half):
    return (blk * tab_ref[0] + pltpu.roll(blk, half, 1) * tab_ref[1]
            + pltpu.roll(blk, LANES - half, 1) * tab_ref[2])


def _inproj_kernel(*refs, dsa):
    if dsa:
        (h_ref, wq_ref, wk_ref, wvt_ref, wqi_ref, wki_ref, wwt_ref, bq_ref, bk_ref, bvt_ref,
         bqi_ref, bki_ref, bwt_ref, tabq_ref, tabi_ref, g_ref, bn_ref,
         q_ref, k_ref, vt_ref, qi_ref, ki_ref, wt_ref) = refs
    else:
        h_ref, wq_ref, wk_ref, wvt_ref, bq_ref, bk_ref, bvt_ref, tabq_ref, q_ref, k_ref, vt_ref = refs
    x = h_ref[...].astype(BF16)
    q = _dot(x, wq_ref[...]) + bq_ref[...]
    for j in range(Q_COLS // LANES):
        sl = slice(j * LANES, (j + 1) * LANES)
        q_ref[:, sl] = _rope128(q[:, sl], tabq_ref, ROT_DIM // 2).astype(BF16)
    k = _dot(x, wk_ref[...]) + bk_ref[...]
    k_ref[...] = _rope128(k, tabq_ref, ROT_DIM // 2).astype(BF16)
    vt = _dot_nt(wvt_ref[...], x) + bvt_ref[...]
    for j in range(vt_ref.shape[0]):
        vt_ref[j] = vt[:, j * BLOCK:(j + 1) * BLOCK].astype(BF16)
    if not dsa:
        return
    qi = _dot(x, wqi_ref[...]) + bqi_ref[...]
    for j in range(IDX_HEADS * IDX_DIM // LANES):
        sl = slice(j * LANES, (j + 1) * LANES)
        qi_ref[:, sl] = _rope128(qi[:, sl], tabi_ref, IDX_ROT_DIM // 2).astype(BF16)
    ki = _dot(x, wki_ref[...]) + bki_ref[...]
    real = lax.broadcasted_iota(jnp.int32, ki.shape, 1) < IDX_DIM
    mu = jnp.sum(jnp.where(real, ki, 0.0), axis=-1, keepdims=True) * (1.0 / IDX_DIM)
    d = jnp.where(real, ki - mu, 0.0)
    var = jnp.sum(d * d, axis=-1, keepdims=True) * (1.0 / IDX_DIM)
    kn = d * lax.rsqrt(var + LN_EPS) * g_ref[...] + bn_ref[...]
    ki_ref[...] = _rope128(kn, tabi_ref, IDX_ROT_DIM // 2)[:, :IDX_DIM].astype(BF16)
    wt_ref[...] = (_dot_nt(wwt_ref[...], x) + bwt_ref[...]) * IDX_W_SCALE


def _rope_tables(pos, rot_dim):
    half = rot_dim // 2
    inv_freq = ROPE_THETA ** (-jnp.arange(half, dtype=F32) / half)
    ang = pos.astype(F32)[:, None] * inv_freq[None, :]
    cos, sin = jnp.cos(ang), jnp.sin(ang)
    n = pos.shape[0]
    rest = HEAD_DIM - rot_dim
    c = jnp.concatenate([cos, cos, jnp.ones((n, rest), F32)], 1)
    s_up = jnp.concatenate([jnp.zeros((n, half), F32), sin, jnp.zeros((n, rest), F32)], 1)
    s_lo = jnp.concatenate([-sin, jnp.zeros((n, HEAD_DIM - half), F32)], 1)
    return jnp.stack([jnp.tile(t, (1, LANES // HEAD_DIM)) for t in (c, s_up, s_lo)])


def _inproj(h, wp, tabq, tabi, *, dsa, seq):
    m, d = h.shape
    tm = min(ROW_TILE, m)
    nsb = seq // tm
    row = lambda i: (i, 0)
    const = lambda i: (0, 0)
    tab_map = lambda i: (0, i % nsb, 0)
    full = lambda a: pl.BlockSpec(a.shape, const)
    names = ("wq", "wk", "wvt", "wqi", "wki", "wwt", "bq", "bk", "bvt", "bqi", "bki", "bwt") if dsa \
        else ("wq", "wk", "wvt", "bq", "bk", "bvt")
    ins = [h] + [wp[k] for k in names] + [tabq]
    in_specs = [pl.BlockSpec((tm, d), row)] + [full(wp[k]) for k in names] \
        + [pl.BlockSpec((3, tm, LANES), tab_map)]
    out_shape = [jax.ShapeDtypeStruct((m, Q_COLS), BF16), jax.ShapeDtypeStruct((m, KV_COLS), BF16),
                 jax.ShapeDtypeStruct((m // BLOCK, KV_COLS, BLOCK), BF16)]
    out_specs = [pl.BlockSpec((tm, Q_COLS), row), pl.BlockSpec((tm, KV_COLS), row),
                 pl.BlockSpec((tm // BLOCK, KV_COLS, BLOCK), lambda i: (i, 0, 0))]
    if dsa:
        ins += [tabi, wp["g"], wp["bn"]]
        in_specs += [pl.BlockSpec((3, tm, LANES), tab_map), full(wp["g"]), full(wp["bn"])]
        out_shape += [jax.ShapeDtypeStruct((m, IDX_HEADS * IDX_DIM), BF16),
                      jax.ShapeDtypeStruct((m, IDX_DIM), BF16),
                      jax.ShapeDtypeStruct((IDX_HEADS, m), F32)]
        out_specs += [pl.BlockSpec((tm, IDX_HEADS * IDX_DIM), row),
                      pl.BlockSpec((tm, IDX_DIM), row),
                      pl.BlockSpec((IDX_HEADS, tm), lambda i: (0, i))]
    return pl.pallas_call(
        functools.partial(_inproj_kernel, dsa=dsa), grid=(m // tm,), in_specs=in_specs,
        out_specs=out_specs, out_shape=out_shape, compiler_params=_cparams("parallel"),
        name="inproj_dsa" if dsa else "inproj_swa")(*ins)


def _stack_heads(ref, first, count, width=HEAD_DIM, scale=None):
    parts = [ref[:, (first + g) * width:(first + g + 1) * width] for g in range(count)]
    x = jnp.concatenate(parts, axis=0)
    return x if scale is None else x * scale


def _store_tile_t(o_ref, first_head, acc):
    out_t = acc[:HEAD_DIM] / acc[HEAD_DIM:HEAD_DIM + 1]
    for g in range(HEADS_PER_TILE):
        h = first_head + g
        o_ref[h * HEAD_DIM:(h + 1) * HEAD_DIM, :] = out_t[:, g * BLOCK:(g + 1) * BLOCK].astype(BF16)


def _values_with_ones(vt, kh):
    hs = slice(kh * HEAD_DIM, (kh + 1) * HEAD_DIM)
    return jnp.concatenate([vt[hs, :], jnp.ones((V_ROWS - HEAD_DIM, vt.shape[1]), BF16)], axis=0)


def _mask_bias(sel):
    bias = jnp.where(sel, 0.0, NEG).astype(BF16)
    return jnp.concatenate([bias] * HEADS_PER_TILE, axis=1)


def _local_attn_kernel(sink_ref, q_ref, kc_ref, kp_ref, km_ref, vtc_ref, vtp_ref, vtm_ref, o_ref,
                       *, banded):
    n = pl.program_id(1)
    r = lax.broadcasted_iota(jnp.int32, (BLOCK, BLOCK), 0)
    c = lax.broadcasted_iota(jnp.int32, (BLOCK, BLOCK), 1)
    blocks = [(kc_ref, vtc_ref, _mask_bias(r <= c))]
    if banded:
        blocks.append((kp_ref, vtp_ref, _mask_bias((r > c) & (n > 0))))
        blocks.append((km_ref, vtm_ref, _mask_bias(r < N_META)))
    sink_row = lax.broadcasted_iota(jnp.int32, (V_ROWS, COL_TILE), 0) == HEAD_DIM
    tiles = [(kh, kh * GROUP + ct * HEADS_PER_TILE)
             for kh in range(N_KV_HEADS) for ct in range(N_COL_TILES)]
    ks = [[k_ref[:, kh * HEAD_DIM:(kh + 1) * HEAD_DIM] for k_ref, _, _ in blocks]
          for kh in range(N_KV_HEADS)]
    vs = [[_values_with_ones(vt_ref[...], kh) for _, vt_ref, _ in blocks] for kh in range(N_KV_HEADS)]

    def logits(i):
        kh, first = tiles[i]
        qs = _stack_heads(q_ref, first, HEADS_PER_TILE, scale=ATTN_SCALE)
        return [_dot_nt(k, qs) for k in ks[kh]]

    sss = [logits(i) for i in range(PIPE)]
    for i, (kh, first) in enumerate(tiles):
        sink = jnp.concatenate(
            [jnp.full((1, BLOCK), sink_ref[first + g], F32) for g in range(HEADS_PER_TILE)], axis=1)
        ss = [s.astype(BF16) + bias for s, (_, _, bias) in zip(sss[i], blocks)]
        m = sink
        for s in ss:
            m = jnp.maximum(m, jnp.max(s, axis=0, keepdims=True).astype(F32))
        mb = m.astype(BF16)
        ps = [jnp.exp(s - mb) for s in ss]
        if i + PIPE < len(tiles):
            sss.append(logits(i + PIPE))
        acc = jnp.where(sink_row, jnp.exp(sink - mb.astype(F32)), 0.0)
        for p, v in zip(ps, vs[kh]):
            acc = acc + _dot(v, p)
        _store_tile_t(o_ref, first, acc)


def _local_attn(sinks, q, k, vt3, km, vtm, *, banded, batch):
    m = q.shape[0]
    nb = m // BLOCK // batch
    cur = lambda b, n, s: (b * nb + n, 0)
    prev = lambda b, n, s: (b * nb + jnp.maximum(n - 1, 0), 0)
    cur3 = lambda b, n, s: (b * nb + n, 0, 0)
    prev3 = lambda b, n, s: (b * nb + jnp.maximum(n - 1, 0), 0, 0)
    grid_spec = pltpu.PrefetchScalarGridSpec(
        num_scalar_prefetch=1, grid=(batch, nb),
        in_specs=[pl.BlockSpec((BLOCK, Q_COLS), cur), pl.BlockSpec((BLOCK, KV_COLS), cur),
                  pl.BlockSpec((BLOCK, KV_COLS), prev),
                  pl.BlockSpec((BLOCK, KV_COLS), lambda b, n, s: (0, 0)),
                  pl.BlockSpec((None, KV_COLS, BLOCK), cur3), pl.BlockSpec((None, KV_COLS, BLOCK), prev3),
                  pl.BlockSpec((None, KV_COLS, BLOCK), lambda b, n, s: (0, 0, 0))],
        out_specs=pl.BlockSpec((None, Q_COLS, BLOCK), cur3))
    return pl.pallas_call(
        functools.partial(_local_attn_kernel, banded=banded), grid_spec=grid_spec,
        out_shape=jax.ShapeDtypeStruct((m // BLOCK, Q_COLS, BLOCK), BF16),
        compiler_params=_cparams("parallel", "parallel"),
        name="swa_attn" if banded else "meta_attn")(sinks, q, k, k, km, vt3, vt3, vtm)


def _sortable(x):
    bits = lax.bitcast_convert_type(x, jnp.int32)
    return jnp.where(bits < 0, bits ^ jnp.int32(0x7FFFFFFF), bits)


def _dsa_kernel(q_ref, qi_ref, wt_ref, k3_ref, vt3_ref, ki3_ref, km_ref, vtm_ref, o_ref,
                key_s, acc_s, *, ksel, idx_bits):
    n = pl.program_id(1)
    cb = IDX_CHUNK_BLOCKS
    ck = cb * BLOCK
    n_chunks = n // cb + 1
    lane = lax.broadcasted_iota(jnp.int32, (1, BLOCK), 1)
    t_row = n * BLOCK + lane

    qis = _stack_heads(qi_ref, 0, IDX_HEADS, IDX_DIM)
    wts = [wt_ref[h:h + 1, :] for h in range(IDX_HEADS)]

    def score_chunk(c, carry):
        kic = ki3_ref[pl.ds(c * cb, cb)].reshape(ck, IDX_DIM)
        d = _dot_nt(kic, qis)
        sc = jnp.zeros((ck, BLOCK), F32)
        for h in range(IDX_HEADS):
            sc = sc + jnp.maximum(d[:, h * BLOCK:(h + 1) * BLOCK], 0.0) * wts[h]
        sc = sc * IDX_SCALE
        s_idx = c * ck + lax.broadcasted_iota(jnp.int32, (ck, BLOCK), 0)
        key = jnp.where(s_idx <= t_row, _sortable(sc), INT_MIN)
        key_s[pl.ds(c * cb, cb)] = key.reshape(cb, BLOCK, BLOCK)
        return carry

    lax.fori_loop(0, n_chunks, score_chunk, 0)

    def count(pred):
        def body(c, acc):
            keys = key_s[pl.ds(c * cb, cb)].reshape(ck, BLOCK)
            s_idx = c * ck + lax.broadcasted_iota(jnp.int32, (ck, BLOCK), 0)
            hit = jnp.where(pred(keys, s_idx), 1, 0).astype(jnp.int32)
            return acc + jnp.sum(hit.reshape(ck // 8, 8, BLOCK), axis=0)
        acc = lax.fori_loop(0, n_chunks, body, jnp.zeros((8, BLOCK), jnp.int32))
        return jnp.sum(acc, axis=0, keepdims=True)

    def thr_bit(i, tu):
        cand_u = tu | lax.shift_left(jnp.int32(1), 31 - i)
        cand = cand_u ^ jnp.int32(INT_MIN)
        return jnp.where(count(lambda k, s: k >= cand) >= ksel, cand_u, tu)

    thr_u = lax.fori_loop(0, 32, thr_bit, jnp.zeros((1, BLOCK), jnp.int32))
    thr = thr_u ^ jnp.int32(INT_MIN)
    n_gt = count(lambda k, s: k > thr)
    n_ge = count(lambda k, s: k >= thr)
    need = ksel - n_gt

    def tie_bit(i, x):
        cand = x | lax.shift_left(jnp.int32(1), idx_bits - 1 - i)
        return jnp.where(count(lambda k, s: (k == thr) & (s < cand)) <= need - 1, cand, x)

    def tie_search():
        return lax.fori_loop(0, idx_bits, tie_bit, jnp.zeros((1, BLOCK), jnp.int32))

    excess = jnp.max(jnp.where((n_ge > ksel) & (thr_u != 0), 1, 0))
    tie_x = lax.cond(excess > 0, tie_search,
                     lambda: jnp.full((1, BLOCK), (1 << idx_bits) - 1, jnp.int32))

    tiles = [(kh, kh * GROUP + ct * HEADS_PER_TILE)
             for kh in range(N_KV_HEADS) for ct in range(N_COL_TILES)]
    qs = [_stack_heads(q_ref, first, HEADS_PER_TILE, scale=ATTN_SCALE) for _, first in tiles]
    krow = lax.broadcasted_iota(jnp.int32, (BLOCK, BLOCK), 0)
    bias_m = _mask_bias(krow < N_META)
    sm = [_dot_nt(km_ref[:, kh * HEAD_DIM:(kh + 1) * HEAD_DIM], qs[i]) for i, (kh, _) in enumerate(tiles)]
    vm = [_values_with_ones(vtm_ref[...], kh) for kh in range(N_KV_HEADS)]
    m0 = []
    for i, (kh, _) in enumerate(tiles):
        s = sm[i].astype(BF16) + bias_m
        m = jnp.max(s, axis=0, keepdims=True)
        m0.append(m)
        acc_s[i] = _dot(vm[kh], jnp.exp(s - m))

    ab = ATT_CHUNK_BLOCKS
    ak = ab * BLOCK
    arow = lax.broadcasted_iota(jnp.int32, (ak, BLOCK), 0)

    def attend(c, ms):
        keys = key_s[pl.ds(c * ab, ab)].reshape(ak, BLOCK)
        s_idx = c * ak + arow
        sel = ((keys > thr) | ((keys == thr) & (s_idx <= tie_x))) & (s_idx <= t_row)
        bias = _mask_bias(sel)
        kc = k3_ref[pl.ds(c * ab, ab)].reshape(ak, KV_COLS)
        vtc = jnp.concatenate([vt3_ref[c * ab + g] for g in range(ab)], axis=1)
        vs = [_values_with_ones(vtc, kh) for kh in range(N_KV_HEADS)]
        ks = [kc[:, kh * HEAD_DIM:(kh + 1) * HEAD_DIM] for kh in range(N_KV_HEADS)]
        logits = lambda i: _dot_nt(ks[tiles[i][0]], qs[i])
        ss = [logits(i) for i in range(PIPE)]
        new_m = []
        for i, (kh, _) in enumerate(tiles):
            s = ss[i].astype(BF16) + bias
            m = jnp.maximum(ms[i], jnp.max(s, axis=0, keepdims=True))
            alpha = jnp.exp(ms[i].astype(F32) - m.astype(F32))
            p = jnp.exp(s - m)
            new_m.append(m)
            if i + PIPE < len(tiles):
                ss.append(logits(i + PIPE))
            acc_s[i] = acc_s[i] * alpha + _dot(vs[kh], p)
        return tuple(new_m)

    lax.fori_loop(0, n // ab + 1, attend, tuple(m0))
    for i, (_, first) in enumerate(tiles):
        _store_tile_t(o_ref, first, acc_s[i])


def _dsa_attn(q, qi, wt, k3, vt3, ki3, km, vtm, *, batch):
    m = q.shape[0]
    nb = m // BLOCK // batch
    seq = nb * BLOCK
    assert nb % IDX_CHUNK_BLOCKS == 0 and IDX_CHUNK_BLOCKS % ATT_CHUNK_BLOCKS == 0
    ksel = min(TOPK_MAX, seq // 4)
    idx_bits = max(1, int(np.ceil(np.log2(seq))))
    qmap = lambda b, n: (b * nb + n, 0)
    bmap = lambda b, n: (b, 0, 0)
    return pl.pallas_call(
        functools.partial(_dsa_kernel, ksel=ksel, idx_bits=idx_bits), grid=(batch, nb),
        in_specs=[pl.BlockSpec((BLOCK, Q_COLS), qmap),
                  pl.BlockSpec((BLOCK, IDX_HEADS * IDX_DIM), qmap),
                  pl.BlockSpec((IDX_HEADS, BLOCK), lambda b, n: (0, b * nb + n)),
                  pl.BlockSpec((nb, BLOCK, KV_COLS), bmap),
                  pl.BlockSpec((nb, KV_COLS, BLOCK), bmap),
                  pl.BlockSpec((nb, BLOCK, IDX_DIM), bmap),
                  pl.BlockSpec((BLOCK, KV_COLS), lambda b, n: (0, 0)),
                  pl.BlockSpec((None, KV_COLS, BLOCK), lambda b, n: (0, 0, 0))],
        out_specs=pl.BlockSpec((None, Q_COLS, BLOCK), lambda b, n: (b * nb + n, 0, 0)),
        out_shape=jax.ShapeDtypeStruct((m // BLOCK, Q_COLS, BLOCK), BF16),
        scratch_shapes=[pltpu.VMEM((nb, BLOCK, BLOCK), jnp.int32),
                        pltpu.VMEM((N_KV_HEADS * N_COL_TILES, V_ROWS, COL_TILE), F32)],
        compiler_params=_cparams("parallel", "arbitrary"), name="dsa_attn",
    )(q, qi, wt, k3, vt3, ki3, km, vtm)


def _outproj_kernel(a_ref, w_ref, h_ref, g_ref, b_ref, wr_ref, br_ref, tri_ref,
                    h1_ref, h1t_ref, route_ref, cnt_ref, carry_s, *, alpha):
    i = pl.program_id(0)

    @pl.when(i == 0)
    def _():
        carry_s[...] = jnp.zeros_like(carry_s)

    nblk = a_ref.shape[0]
    gb = max(1, nblk // OUTPROJ_GROUPS)
    gr = gb * BLOCK
    n_grp = nblk // gb
    ys = [jnp.concatenate([_dot_tn(a_ref[g * gb + j], w_ref[...]) for j in range(gb)], axis=0)
          for g in range(n_grp)]
    lane = lax.broadcasted_iota(jnp.int32, (gr, LANES), 1)
    picks = []
    for g in range(n_grp):
        rs = pl.ds(g * gr, gr)
        h1 = _layer_norm(alpha * h_ref[rs, :] + ys[g], g_ref[...], b_ref[...])
        h1_ref[rs, :] = h1
        _store_token_tiles(h1t_ref.at[pl.ds(g * gr * TOKEN_ROWS, gr * TOKEN_ROWS)], h1)
        rem = _dot(h1.astype(BF16), wr_ref[...]) + br_ref[...]
        idxs, vals = [], []
        for _ in range(TOP_K):
            mx = jnp.max(rem, axis=-1, keepdims=True)
            ix = jnp.min(jnp.where(rem == mx, lane, LANES), axis=-1, keepdims=True)
            idxs.append(ix)
            vals.append(mx)
            rem = jnp.where(lane == ix, -jnp.inf, rem)
        exps = [jnp.exp(v - vals[0]) for v in vals]
        den = exps[0] + exps[1] + exps[2] + exps[3]
        picks.append((idxs, [e / den for e in exps], jnp.where(rem == -jnp.inf, 1.0, 0.0)))
    sel = jnp.concatenate([p[2] for p in picks], axis=0)
    before = _dot(tri_ref[...], sel.astype(BF16)) + carry_s[...]
    for g, (idxs, gates, _) in enumerate(picks):
        bg = before[g * gr:(g + 1) * gr]
        route = jnp.zeros((gr, LANES), F32)
        for k in range(TOP_K):
            rank = jnp.sum(jnp.where(lane == idxs[k], bg, 0.0), axis=-1, keepdims=True)
            route = jnp.where(lane == k, idxs[k].astype(F32), route)
            route = jnp.where(lane == TOP_K + k, gates[k], route)
            route = jnp.where(lane == 2 * TOP_K + k, rank, route)
        route_ref[pl.ds(g * gr, gr), :] = route
    carry_s[...] = carry_s[...] + jnp.sum(sel, axis=0, keepdims=True)
    cnt_ref[...] = carry_s[...]


def _outproj(a_t, w, h, g, b, wr, br, *, alpha):
    m, d = h.shape
    tm = min(ROW_TILE, m)
    tri = jnp.tril(jnp.ones((tm, tm), BF16), -1)
    row = lambda i: (i, 0)
    const = lambda i: (0, 0)
    full = lambda x: pl.BlockSpec(x.shape, const)
    return pl.pallas_call(
        functools.partial(_outproj_kernel, alpha=alpha), grid=(m // tm,),
        in_specs=[pl.BlockSpec((tm // BLOCK, Q_COLS, BLOCK), lambda i: (i, 0, 0)), full(w),
                  pl.BlockSpec((tm, d), row), full(g), full(b), full(wr), full(br), full(tri)],
        out_specs=[pl.BlockSpec((tm, d), row), pl.BlockSpec((tm * TOKEN_ROWS, LANES), row),
                   pl.BlockSpec((tm, LANES), row), pl.BlockSpec((1, LANES), const)],
        out_shape=[jax.ShapeDtypeStruct((m, d), F32), jax.ShapeDtypeStruct((m * TOKEN_ROWS, LANES), F32),
                   jax.ShapeDtypeStruct((m, LANES), F32), jax.ShapeDtypeStruct((1, LANES), F32)],
        scratch_shapes=[pltpu.VMEM((1, LANES), F32)],
        compiler_params=_cparams("arbitrary"), name="outproj_router")(a_t, w, h, g, b, wr, br, tri)


def _expert_ffn(x, wgu_ref, bgu_ref, wd_ref, bd_ref):
    ff = wd_ref.shape[0]
    acc = jnp.zeros((x.shape[0], wd_ref.shape[1]), F32)
    for c in range(ff // FF_CHUNK):
        gs = slice(c * FF_CHUNK, (c + 1) * FF_CHUNK)
        us = slice(ff + c * FF_CHUNK, ff + (c + 1) * FF_CHUNK)
        gate = jnp.minimum(_dot(x, wgu_ref[:, gs]) + bgu_ref[:, gs], SWIGLU_LIMIT)
        up = jnp.clip(_dot(x, wgu_ref[:, us]) + bgu_ref[:, us], -SWIGLU_LIMIT, SWIGLU_LIMIT)
        act = gate * jax.nn.sigmoid(SWIGLU_ALPHA * gate) * (up + 1.0)
        acc = acc + _dot(act.astype(BF16), wd_ref[gs, :])
    return acc + bd_ref[...]


def _experts_kernel(be_ref, nu_ref, tok_ref, tok_next_ref, dst_ref, h_ref, wgu_ref, bgu_ref, wd_ref,
                    bd_ref, yk_ref, xbuf, ybuf, gsem, ssem):
    del be_ref
    i = pl.program_id(0)
    n_used = nu_ref[0]
    slot = i % 2
    rows = xbuf.shape[1] // TOKEN_ROWS
    tile = lambda t: pl.ds(pl.multiple_of(t * TOKEN_ROWS, TOKEN_ROWS), TOKEN_ROWS)

    def gather(src_ref, s):
        def body(r8, c):
            for u in range(DMA_UNROLL):
                r = r8 * DMA_UNROLL + u
                pltpu.make_async_copy(h_ref.at[tile(src_ref[0, 0, r])],
                                      xbuf.at[s, tile(r)], gsem.at[s]).start()
            return c
        lax.fori_loop(0, rows // DMA_UNROLL, body, 0)

    def scatter(s):
        def body(r8, c):
            for u in range(DMA_UNROLL):
                r = r8 * DMA_UNROLL + u
                pltpu.make_async_copy(ybuf.at[s, tile(r)],
                                      yk_ref.at[tile(dst_ref[0, 0, r])], ssem.at[s]).start()
            return c
        lax.fori_loop(0, rows // DMA_UNROLL, body, 0)

    def wait_gather(s):
        pltpu.make_async_copy(xbuf.at[s], xbuf.at[s], gsem.at[s]).wait()

    def wait_scatter(s):
        pltpu.make_async_copy(ybuf.at[s], ybuf.at[s], ssem.at[s]).wait()

    @pl.when(i == 0)
    def _():
        for s in range(2):
            ybuf[s] = jnp.zeros(ybuf.shape[1:], F32)
        fills = [pltpu.make_async_copy(
            ybuf.at[s], yk_ref.at[pl.ds(yk_ref.shape[0] - (2 - s) * ybuf.shape[1], ybuf.shape[1])],
            ssem.at[s]) for s in range(2)]
        for f in fills:
            f.start()
        for f in fills:
            f.wait()

    @pl.when((i == 0) & (n_used > 0))
    def _():
        gather(tok_ref, 0)

    @pl.when(i < n_used)
    def _():
        wait_gather(slot)

        @pl.when(i + 1 < n_used)
        def _():
            gather(tok_next_ref, 1 - slot)

        @pl.when(i >= 2)
        def _():
            wait_scatter(slot)

        x = _load_token_tiles(xbuf.at[slot], rows).astype(BF16)
        _store_token_tiles(ybuf.at[slot], _expert_ffn(x, wgu_ref, bgu_ref, wd_ref, bd_ref))
        scatter(slot)

    @pl.when(i == pl.num_programs(0) - 1)
    def _():
        @pl.when(n_used >= 1)
        def _():
            wait_scatter((n_used - 1) % 2)

        @pl.when(n_used >= 2)
        def _():
            wait_scatter(n_used % 2)


def _experts(blk_e, n_used, tok3, dst3, ht, wgu, bgu, wd, bd, n_out_rows):
    n_blk, _, rows = tok3.shape
    ff, d = wd.shape[1:]
    assert d == TOKEN_ROWS * LANES
    emap = lambda i, be, nu: (be[i], 0, 0)
    smem = lambda imap: pl.BlockSpec((1, 1, rows), imap, memory_space=pltpu.SMEM)
    grid_spec = pltpu.PrefetchScalarGridSpec(
        num_scalar_prefetch=2, grid=(n_blk,),
        in_specs=[smem(lambda i, be, nu: (i, 0, 0)),
                  smem(lambda i, be, nu: (jnp.minimum(i + 1, n_blk - 1), 0, 0)),
                  smem(lambda i, be, nu: (i, 0, 0)),
                  pl.BlockSpec(memory_space=pl.ANY),
                  pl.BlockSpec((None, d, 2 * ff), emap), pl.BlockSpec((None, 1, 2 * ff), emap),
                  pl.BlockSpec((None, ff, d), emap), pl.BlockSpec((None, 1, d), emap)],
        out_specs=pl.BlockSpec(memory_space=pl.ANY),
        scratch_shapes=[pltpu.VMEM((2, rows * TOKEN_ROWS, LANES), F32),
                        pltpu.VMEM((2, rows * TOKEN_ROWS, LANES), F32),
                        pltpu.SemaphoreType.DMA((2,)), pltpu.SemaphoreType.DMA((2,))])
    return pl.pallas_call(
        _experts_kernel, grid_spec=grid_spec,
        out_shape=jax.ShapeDtypeStruct((n_out_rows * TOKEN_ROWS, LANES), F32),
        compiler_params=_cparams("arbitrary"), name="moe_experts",
    )(blk_e, n_used, tok3, tok3, dst3, ht, wgu, bgu, wd, bd)


def _combine_kernel(route_ref, h_ref, g_ref, b_ref, y0_ref, y1_ref, y2_ref, y3_ref, o_ref, *, alpha):
    z = alpha * h_ref[...]
    for k, y_ref in enumerate((y0_ref, y1_ref, y2_ref, y3_ref)):
        z = z + route_ref[:, TOP_K + k:TOP_K + k + 1] * _load_token_tiles(y_ref, h_ref.shape[0])
    o_ref[...] = _layer_norm(z, g_ref[...], b_ref[...])


def _combine(route, h, g, b, yk, *, alpha):
    m, d = h.shape
    tm = min(ROW_TILE, m)
    nt = m // tm
    row = lambda i: (i, 0)
    const = lambda i: (0, 0)
    choice = lambda k: pl.BlockSpec((tm * TOKEN_ROWS, LANES), lambda i: (k * nt + i, 0))
    return pl.pallas_call(
        functools.partial(_combine_kernel, alpha=alpha), grid=(nt,),
        in_specs=[pl.BlockSpec((tm, LANES), row), pl.BlockSpec((tm, d), row),
                  pl.BlockSpec(g.shape, const), pl.BlockSpec(b.shape, const)]
                 + [choice(k) for k in range(TOP_K)],
        out_specs=pl.BlockSpec((tm, d), row),
        out_shape=jax.ShapeDtypeStruct((m, d), F32),
        compiler_params=_cparams("parallel"), name="moe_combine")(route, h, g, b, yk, yk, yk, yk)


def _meta_moe_kernel(hm_ref, gm_ref, wgu_ref, bgu_ref, wd_ref, bd_ref, g_ref, b_ref, o_ref, acc_s,
                     *, alpha):
    e = pl.program_id(0)

    @pl.when(e == 0)
    def _():
        acc_s[...] = jnp.zeros_like(acc_s)

    y = _expert_ffn(hm_ref[...].astype(BF16), wgu_ref, bgu_ref, wd_ref, bd_ref)
    lane = lax.broadcasted_iota(jnp.int32, gm_ref.shape, 1)
    ge = jnp.sum(jnp.where(lane == e, gm_ref[...], 0.0), axis=-1, keepdims=True)
    acc_s[...] = acc_s[...] + ge * y

    @pl.when(e == pl.num_programs(0) - 1)
    def _():
        o_ref[...] = _layer_norm(alpha * hm_ref[...] + acc_s[...], g_ref[...], b_ref[...])


def _meta_moe(hm, gm, wgu, bgu, wd, bd, g, b, *, alpha):
    m, d = hm.shape
    ff = wd.shape[1]
    const = lambda e: (0, 0)
    emap = lambda e: (e, 0, 0)
    return pl.pallas_call(
        functools.partial(_meta_moe_kernel, alpha=alpha), grid=(wgu.shape[0],),
        in_specs=[pl.BlockSpec((m, d), const), pl.BlockSpec((m, LANES), const),
                  pl.BlockSpec((None, d, 2 * ff), emap), pl.BlockSpec((None, 1, 2 * ff), emap),
                  pl.BlockSpec((None, ff, d), emap), pl.BlockSpec((None, 1, d), emap),
                  pl.BlockSpec(g.shape, const), pl.BlockSpec(b.shape, const)],
        out_specs=pl.BlockSpec((m, d), const),
        out_shape=jax.ShapeDtypeStruct((m, d), F32),
        scratch_shapes=[pltpu.VMEM((m, d), F32)],
        compiler_params=_cparams("arbitrary"), name="meta_experts")(hm, gm, wgu, bgu, wd, bd, g, b)


def _moe(h1, h1t, route, cnt, hm1, route_m, ep, g, b, *, alpha):
    m, d = h1.shape
    n_blk = -(-m * TOP_K // MOE_ROWS) + N_EXPERTS
    e = route[:, :TOP_K].astype(jnp.int32)
    rank = route[:, 2 * TOP_K:3 * TOP_K].astype(jnp.int32)
    counts = cnt[0, :N_EXPERTS].astype(jnp.int32)
    padded = (counts + MOE_ROWS - 1) // MOE_ROWS * MOE_ROWS
    pend = jnp.cumsum(padded)
    dest = (pend - padded)[e] + rank
    blk_start = jnp.arange(n_blk, dtype=jnp.int32) * MOE_ROWS
    blk_e = jnp.minimum(jnp.sum((pend[None, :] <= blk_start[:, None]).astype(jnp.int32), axis=1),
                        N_EXPERTS - 1)
    n_used = (pend[-1:] // MOE_ROWS).astype(jnp.int32)
    n_sorted = n_blk * MOE_ROWS
    inv = jnp.zeros((n_sorted,), jnp.int32).at[dest.reshape(-1)].set(
        jnp.arange(1, m * TOP_K + 1, dtype=jnp.int32))
    pair = jnp.maximum(inv - 1, 0)
    pos = jnp.arange(n_sorted, dtype=jnp.int32)
    pad_row = TOP_K * m + (pos // MOE_ROWS % 2) * MOE_ROWS + pos % MOE_ROWS
    tok3 = (pair // TOP_K).reshape(n_blk, 1, MOE_ROWS)
    dst3 = jnp.where(inv > 0, (pair % TOP_K) * m + pair // TOP_K, pad_row).reshape(n_blk, 1, MOE_ROWS)
    yk = _experts(blk_e, n_used, tok3, dst3, h1t, ep["wgu"], ep["bgu"], ep["wd"], ep["bd"],
                  TOP_K * m + 2 * MOE_ROWS)
    h2 = _combine(route, h1, g, b, yk, alpha=alpha)
    em = route_m[:, :TOP_K].astype(jnp.int32)
    gm = jnp.sum(jax.nn.one_hot(em, LANES, dtype=F32) * route_m[:, TOP_K:2 * TOP_K, None], axis=1)
    hm2 = _meta_moe(hm1, gm, ep["wgu"], ep["bgu"], ep["wd"], ep["bd"], g, b, alpha=alpha)
    return h2, hm2


def _split_in_weights(w, b, dsa, g=None, bn=None):
    cuts = np.cumsum([Q_COLS, KV_COLS, KV_COLS, IDX_HEADS * IDX_DIM, IDX_DIM]).tolist()
    wb = w.astype(BF16)
    row = lambda v: v.reshape(1, -1)
    col = lambda v: v.reshape(-1, 1)
    wp = {"wq": wb[:, :cuts[0]], "bq": row(b[:cuts[0]]),
          "wk": wb[:, cuts[0]:cuts[1]], "bk": row(b[cuts[0]:cuts[1]]),
          "wvt": wb[:, cuts[1]:cuts[2]].T, "bvt": col(b[cuts[1]:cuts[2]])}
    if dsa:
        pad = LANES - IDX_DIM
        wp.update({
            "wqi": wb[:, cuts[2]:cuts[3]], "bqi": row(b[cuts[2]:cuts[3]]),
            "wki": jnp.pad(wb[:, cuts[3]:cuts[4]], ((0, 0), (0, pad))),
            "bki": row(jnp.pad(b[cuts[3]:cuts[4]], (0, pad))),
            "wwt": wb[:, cuts[4]:].T, "bwt": col(b[cuts[4]:]),
            "g": row(jnp.pad(g, (0, pad))), "bn": row(jnp.pad(bn, (0, pad)))})
    return wp


def kernel(x, meta_tokens, w_in_a, b_in_a, sinks_a, w_out_a, w_in_b, b_in_b, idx_k_norm_g, idx_k_norm_b, w_out_b, ln_mix_g, ln_mix_b, w_router, b_router, w_gate_up, b_gate_up, w_down, b_down, ln_ffn_g, ln_ffn_b):
    batch, seq, d = x.shape
    depth = ln_mix_g.shape[0]
    alpha = float((2 * depth) ** 0.25)
    row = lambda v: v.reshape(1, -1)
    nb = seq // BLOCK

    h = x.reshape(batch * seq, d)
    hm = jnp.zeros((BLOCK, d), F32).at[:N_META].set(meta_tokens.astype(F32))
    tab_q = _rope_tables(jnp.arange(N_META, N_META + seq), ROT_DIM)
    tab_i = _rope_tables(jnp.arange(N_META, N_META + seq), IDX_ROT_DIM)
    tab_qm = _rope_tables(jnp.arange(BLOCK), ROT_DIM)
    tab_im = _rope_tables(jnp.arange(BLOCK), IDX_ROT_DIM)
    no_sinks = jnp.full((N_HEADS,), NEG, F32)

    for i in range(depth):
        j = i // 2
        dsa = i % 2 == 1
        if dsa:
            wp = _split_in_weights(w_in_b[j], b_in_b[j], True, idx_k_norm_g[j], idx_k_norm_b[j])
            w_out, sinks = w_out_b[j], no_sinks
        else:
            wp = _split_in_weights(w_in_a[j], b_in_a[j], False)
            w_out, sinks = w_out_a[j], sinks_a[j].astype(F32)
        pr = _inproj(h, wp, tab_q, tab_i, dsa=dsa, seq=seq)
        pm = _inproj(hm, wp, tab_qm, tab_im, dsa=dsa, seq=BLOCK)
        q, k, vt3 = pr[:3]
        qm, km, vtm = pm[:3]
        if dsa:
            qi, ki, wt = pr[3:]
            att = _dsa_attn(q, qi, wt, k.reshape(batch * nb, BLOCK, KV_COLS), vt3,
                            ki.reshape(batch * nb, BLOCK, IDX_DIM), km, vtm, batch=batch)
        else:
            att = _local_attn(sinks, q, k, vt3, km, vtm, banded=True, batch=batch)
        att_m = _local_attn(sinks, qm, km, vtm, km, vtm, banded=False, batch=1)

        w_out = w_out.astype(BF16)
        wr = jnp.pad(w_router[i].astype(BF16), ((0, 0), (0, LANES - N_EXPERTS)))
        br = row(jnp.pad(b_router[i].astype(F32), (0, LANES - N_EXPERTS), constant_values=NEG))
        g1, b1 = row(ln_mix_g[i]), row(ln_mix_b[i])
        h1, h1t, route, cnt = _outproj(att, w_out, h, g1, b1, wr, br, alpha=alpha)
        hm1, _, route_m, _ = _outproj(att_m, w_out, hm, g1, b1, wr, br, alpha=alpha)

        ep = {"wgu": w_gate_up[i].astype(BF16), "bgu": b_gate_up[i][:, None, :],
              "wd": w_down[i].astype(BF16), "bd": b_down[i][:, None, :]}
        h, hm = _moe(h1, h1t, route, cnt, hm1, route_m, ep, row(ln_ffn_g[i]), row(ln_ffn_b[i]),
                     alpha=alpha)
    return h.reshape(batch, seq, d)
```

```python
import functools

import jax
import jax.numpy as jnp
import numpy as np
from jax import lax
from jax.experimental import pallas as pl
from jax.experimental.pallas import tpu as pltpu

N_META = 16
HEAD_DIM = 64
N_HEADS = 16
N_KV_HEADS = 2
GROUP = N_HEADS // N_KV_HEADS
Q_COLS = N_HEADS * HEAD_DIM
KV_COLS = N_KV_HEADS * HEAD_DIM
ROT_DIM = HEAD_DIM // 4
ROPE_THETA = 500000.0
ATTN_SCALE = HEAD_DIM ** -0.5
BLOCK = 128
IDX_HEADS = 8
IDX_DIM = 64
IDX_ROT_DIM = IDX_DIM // 2
IDX_SCALE = IDX_DIM ** -0.5
IDX_W_SCALE = IDX_HEADS ** -0.5
TOPK_MAX = 256
N_EXPERTS = 32
TOP_K = 4
SWIGLU_LIMIT = 7.0
SWIGLU_ALPHA = 1.702
LN_EPS = 1e-5
NEG = -1e30
INT_MIN = -2147483648

LANES = 128
TOKEN_ROWS = 8
ROW_TILE = 512
MOE_ROWS = 512
FF_CHUNK = 256
DMA_UNROLL = 8
OUTPROJ_GROUPS = 2
IDX_CHUNK_BLOCKS = 4
ATT_CHUNK_BLOCKS = 4
V_ROWS = HEAD_DIM + 16
COL_TILE = 256
HEADS_PER_TILE = COL_TILE // BLOCK
N_COL_TILES = GROUP // HEADS_PER_TILE
PIPE = 8
VMEM_LIMIT = 48 * 1024 * 1024

F32 = jnp.float32
BF16 = jnp.bfloat16


def _cparams(*sem):
    return pltpu.CompilerParams(dimension_semantics=sem, vmem_limit_bytes=VMEM_LIMIT)


def _dot(a, b):
    return jnp.dot(a, b, preferred_element_type=F32)


def _dot_nt(a, b):
    return lax.dot_general(a, b, (((1,), (1,)), ((), ())), preferred_element_type=F32)


def _dot_tn(a, b):
    return lax.dot_general(a, b, (((0,), (0,)), ((), ())), preferred_element_type=F32)


def _layer_norm(z, g, b):
    mu = jnp.mean(z, axis=-1, keepdims=True)
    d = z - mu
    var = jnp.mean(d * d, axis=-1, keepdims=True)
    return d * lax.rsqrt(var + LN_EPS) * g + b


def _store_token_tiles(ref, x):
    for j in range(TOKEN_ROWS):
        ref[pl.ds(j, x.shape[0], stride=TOKEN_ROWS), :] = x[:, j * LANES:(j + 1) * LANES]


def _load_token_tiles(ref, rows):
    return jnp.concatenate([ref[pl.ds(j, rows, stride=TOKEN_ROWS), :] for j in range(TOKEN_ROWS)], axis=1)


def _rope128(blk, tab_ref, half):
    return (blk * tab_ref[0] + pltpu.roll(blk, half, 1) * tab_ref[1]
            + pltpu.roll(blk, LANES - half, 1) * tab_ref[2])


def _inproj_kernel(*refs, dsa):
    if dsa:
        (h_ref, wq_ref, wk_ref, wvt_ref, wqi_ref, bq_ref, bk_ref, bvt_ref, bqi_ref,
         tabq_ref, tabi_ref, g_ref, bn_ref, q_ref, k_ref, vt_ref, qi_ref, ki_ref, wt_ref) = refs
    else:
        h_ref, wq_ref, wk_ref, wvt_ref, bq_ref, bk_ref, bvt_ref, tabq_ref, q_ref, k_ref, vt_ref = refs
    x = h_ref[...].astype(BF16)
    q = _dot(x, wq_ref[...]) + bq_ref[...]
    for j in range(Q_COLS // LANES):
        sl = slice(j * LANES, (j + 1) * LANES)
        q_ref[:, sl] = _rope128(q[:, sl], tabq_ref, ROT_DIM // 2).astype(BF16)
    kk = _dot(x, wk_ref[...]) + bk_ref[...]
    k_ref[...] = _rope128(kk[:, :KV_COLS], tabq_ref, ROT_DIM // 2).astype(BF16)
    vw = _dot_nt(wvt_ref[...], x) + bvt_ref[...]
    for j in range(vt_ref.shape[0]):
        vt_ref[j] = vw[:KV_COLS, j * BLOCK:(j + 1) * BLOCK].astype(BF16)
    if not dsa:
        return
    qi = _dot(x, wqi_ref[...]) + bqi_ref[...]
    for j in range(IDX_HEADS * IDX_DIM // LANES):
        sl = slice(j * LANES, (j + 1) * LANES)
        qi_ref[:, sl] = _rope128(qi[:, sl], tabi_ref, IDX_ROT_DIM // 2).astype(BF16)
    ki = kk[:, KV_COLS:]
    real =lax.broadcasted_iota(jnp.int32, ki.shape, 1) < IDX_DIM
    mu = jnp.sum(jnp.where(real, ki, 0.0), axis=-1, keepdims=True) * (1.0 / IDX_DIM)
    d = jnp.where(real, ki - mu, 0.0)
    var = jnp.sum(d * d, axis=-1, keepdims=True) * (1.0 / IDX_DIM)
    kn = d * lax.rsqrt(var + LN_EPS) * g_ref[...] + bn_ref[...]
    ki_ref[...] = _rope128(kn, tabi_ref, IDX_ROT_DIM // 2)[:, :IDX_DIM].astype(BF16)
    wt_ref[...] = vw[KV_COLS:KV_COLS + IDX_HEADS] * IDX_W_SCALE


def _rope_tables(pos, rot_dim):
    half = rot_dim // 2
    inv_freq = ROPE_THETA ** (-jnp.arange(half, dtype=F32) / half)
    ang = pos.astype(F32)[:, None] * inv_freq[None, :]
    cos, sin = jnp.cos(ang), jnp.sin(ang)
    n = pos.shape[0]
    rest = HEAD_DIM - rot_dim
    c = jnp.concatenate([cos, cos, jnp.ones((n, rest), F32)], 1)
    s_up = jnp.concatenate([jnp.zeros((n, half), F32), sin, jnp.zeros((n, rest), F32)], 1)
    s_lo = jnp.concatenate([-sin, jnp.zeros((n, HEAD_DIM - half), F32)], 1)
    return jnp.stack([jnp.tile(t, (1, LANES // HEAD_DIM)) for t in (c, s_up, s_lo)])


def _inproj(h, wp, tabq, tabi, *, dsa, seq):
    m, d = h.shape
    tm = min(ROW_TILE, m)
    nsb = seq // tm
    row = lambda i: (i, 0)
    const = lambda i: (0, 0)
    tab_map = lambda i: (0, i % nsb, 0)
    full = lambda a: pl.BlockSpec(a.shape, const)
    names = ("wq", "wk", "wvt", "wqi", "bq", "bk", "bvt", "bqi") if dsa \
        else ("wq", "wk", "wvt", "bq", "bk", "bvt")
    ins = [h] + [wp[k] for k in names] + [tabq]
    in_specs = [pl.BlockSpec((tm, d), row)] + [full(wp[k]) for k in names] \
        + [pl.BlockSpec((3, tm, LANES), tab_map)]
    out_shape = [jax.ShapeDtypeStruct((m, Q_COLS), BF16), jax.ShapeDtypeStruct((m, KV_COLS), BF16),
                 jax.ShapeDtypeStruct((m // BLOCK, KV_COLS, BLOCK), BF16)]
    out_specs = [pl.BlockSpec((tm, Q_COLS), row), pl.BlockSpec((tm, KV_COLS), row),
                 pl.BlockSpec((tm // BLOCK, KV_COLS, BLOCK), lambda i: (i, 0, 0))]
    if dsa:
        ins += [tabi, wp["g"], wp["bn"]]
        in_specs += [pl.BlockSpec((3, tm, LANES), tab_map), full(wp["g"]), full(wp["bn"])]
        out_shape += [jax.ShapeDtypeStruct((m, IDX_HEADS * IDX_DIM), BF16),
                      jax.ShapeDtypeStruct((m, IDX_DIM), BF16),
                      jax.ShapeDtypeStruct((IDX_HEADS, m), F32)]
        out_specs += [pl.BlockSpec((tm, IDX_HEADS * IDX_DIM), row),
                      pl.BlockSpec((tm, IDX_DIM), row),
                      pl.BlockSpec((IDX_HEADS, tm), lambda i: (0, i))]
    return pl.pallas_call(
        functools.partial(_inproj_kernel, dsa=dsa), grid=(m // tm,), in_specs=in_specs,
        out_specs=out_specs, out_shape=out_shape, compiler_params=_cparams("parallel"),
        name="inproj_dsa" if dsa else "inproj_swa")(*ins)


def _stack_heads(ref, first, count, width=HEAD_DIM, scale=None):
    parts = [ref[:, (first + g) * width:(first + g + 1) * width] for g in range(count)]
    x = jnp.concatenate(parts, axis=0)
    return x if scale is None else x * scale


def _store_tile_t(o_ref, first_head, acc):
    out_t = acc[:HEAD_DIM] / acc[HEAD_DIM:HEAD_DIM + 1]
    for g in range(HEADS_PER_TILE):
        h = first_head + g
        o_ref[h * HEAD_DIM:(h + 1) * HEAD_DIM, :] = out_t[:, g * BLOCK:(g + 1) * BLOCK].astype(BF16)


def _values_with_ones(vt, kh):
    hs = slice(kh * HEAD_DIM, (kh + 1) * HEAD_DIM)
    return jnp.concatenate([vt[hs, :], jnp.ones((V_ROWS - HEAD_DIM, vt.shape[1]), BF16)], axis=0)


def _mask_bias(sel):
    bias = jnp.where(sel, 0.0, NEG).astype(BF16)
    return jnp.concatenate([bias] * HEADS_PER_TILE, axis=1)


def _local_attn_kernel(sink_ref, q_ref, kc_ref, kp_ref, km_ref, vtc_ref, vtp_ref, vtm_ref, o_ref,
                       *, banded):
    n = pl.program_id(1)
    r = lax.broadcasted_iota(jnp.int32, (BLOCK, BLOCK), 0)
    c = lax.broadcasted_iota(jnp.int32, (BLOCK, BLOCK), 1)
    blocks = [(kc_ref, vtc_ref, _mask_bias(r <= c))]
    if banded:
        blocks.append((kp_ref, vtp_ref, _mask_bias((r > c) & (n > 0))))
        blocks.append((km_ref, vtm_ref, _mask_bias(r < N_META)))
    sink_row = lax.broadcasted_iota(jnp.int32, (V_ROWS, COL_TILE), 0) == HEAD_DIM
    tiles = [(kh, kh * GROUP + ct * HEADS_PER_TILE)
             for kh in range(N_KV_HEADS) for ct in range(N_COL_TILES)]
    ks = [[k_ref[:, kh * HEAD_DIM:(kh + 1) * HEAD_DIM] for k_ref, _, _ in blocks]
          for kh in range(N_KV_HEADS)]
    vs = [[_values_with_ones(vt_ref[...], kh) for _, vt_ref, _ in blocks] for kh in range(N_KV_HEADS)]

    def logits(i):
        kh, first = tiles[i]
        qs = _stack_heads(q_ref, first, HEADS_PER_TILE, scale=ATTN_SCALE)
        return [_dot_nt(k, qs) for k in ks[kh]]

    sss = [logits(i) for i in range(PIPE)]
    for i, (kh, first) in enumerate(tiles):
        sink = jnp.concatenate(
            [jnp.full((1, BLOCK), sink_ref[first + g], F32) for g in range(HEADS_PER_TILE)], axis=1)
        ss = [s.astype(BF16) + bias for s, (_, _, bias) in zip(sss[i], blocks)]
        m = sink
        for s in ss:
            m = jnp.maximum(m, jnp.max(s, axis=0, keepdims=True).astype(F32))
        mb = m.astype(BF16)
        ps = [jnp.exp(s - mb) for s in ss]
        if i + PIPE < len(tiles):
            sss.append(logits(i + PIPE))
        acc = jnp.where(sink_row, jnp.exp(sink - mb.astype(F32)), 0.0)
        for p, v in zip(ps, vs[kh]):
            acc = acc + _dot(v, p)
        _store_tile_t(o_ref, first, acc)


def _local_attn(sinks, q, k, vt3, km, vtm, *, banded, batch):
    m = q.shape[0]
    nb = m // BLOCK // batch
    cur = lambda b, n, s: (b * nb + n, 0)
    prev = lambda b, n, s: (b * nb + jnp.maximum(n - 1, 0), 0)
    cur3 = lambda b, n, s: (b * nb + n, 0, 0)
    prev3 = lambda b, n, s: (b * nb + jnp.maximum(n - 1, 0), 0, 0)
    grid_spec = pltpu.PrefetchScalarGridSpec(
        num_scalar_prefetch=1, grid=(batch, nb),
        in_specs=[pl.BlockSpec((BLOCK, Q_COLS), cur), pl.BlockSpec((BLOCK, KV_COLS), cur),
                  pl.BlockSpec((BLOCK, KV_COLS), prev),
                  pl.BlockSpec((BLOCK, KV_COLS), lambda b, n, s: (0, 0)),
                  pl.BlockSpec((None, KV_COLS, BLOCK), cur3), pl.BlockSpec((None, KV_COLS, BLOCK), prev3),
                  pl.BlockSpec((None, KV_COLS, BLOCK), lambda b, n, s: (0, 0, 0))],
        out_specs=pl.BlockSpec((None, Q_COLS, BLOCK), cur3))
    return pl.pallas_call(
        functools.partial(_local_attn_kernel, banded=banded), grid_spec=grid_spec,
        out_shape=jax.ShapeDtypeStruct((m // BLOCK, Q_COLS, BLOCK), BF16),
        compiler_params=_cparams("parallel", "parallel"),
        name="swa_attn" if banded else "meta_attn")(sinks, q, k, k, km, vt3, vt3, vtm)


def _sortable(x):
    bits = lax.bitcast_convert_type(x, jnp.int32)
    return jnp.where(bits < 0, bits ^ jnp.int32(0x7FFFFFFF), bits)


def _dsa_kernel(q_ref, qi_ref, wt_ref, k3_ref, vt3_ref, ki3_ref, km_ref, vtm_ref, o_ref,
                key_s, acc_s, *, ksel, idx_bits):
    n = pl.program_id(1)
    cb = IDX_CHUNK_BLOCKS
    ck = cb * BLOCK
    n_chunks = n // cb + 1
    lane = lax.broadcasted_iota(jnp.int32, (1, BLOCK), 1)
    t_row = n * BLOCK + lane

    qis = _stack_heads(qi_ref, 0, IDX_HEADS, IDX_DIM)
    wts = [wt_ref[h:h + 1, :] for h in range(IDX_HEADS)]

    def score_chunk(c, carry):
        kic = ki3_ref[pl.ds(c * cb, cb)].reshape(ck, IDX_DIM)
        d = _dot_nt(kic, qis)
        sc = jnp.zeros((ck, BLOCK), F32)
        for h in range(IDX_HEADS):
            sc = sc + jnp.maximum(d[:, h * BLOCK:(h + 1) * BLOCK], 0.0) * wts[h]
        sc = sc * IDX_SCALE
        s_idx = c * ck + lax.broadcasted_iota(jnp.int32, (ck, BLOCK), 0)
        key = jnp.where(s_idx <= t_row, _sortable(sc), INT_MIN)
        key_s[pl.ds(c * cb, cb)] = key.reshape(cb, BLOCK, BLOCK)
        return carry

    lax.fori_loop(0, n_chunks, score_chunk, 0)

    def count(pred):
        def body(c, acc):
            keys = key_s[pl.ds(c * cb, cb)].reshape(ck, BLOCK)
            s_idx = c * ck + lax.broadcasted_iota(jnp.int32, (ck, BLOCK), 0)
            hit = jnp.where(pred(keys, s_idx), 1, 0).astype(jnp.int32)
            return acc + jnp.sum(hit.reshape(ck // 8, 8, BLOCK), axis=0)
        acc = lax.fori_loop(0, n_chunks, body, jnp.zeros((8, BLOCK), jnp.int32))
        return jnp.sum(acc, axis=0, keepdims=True)

    def thr_bit(i, tu):
        cand_u = tu | lax.shift_left(jnp.int32(1), 31 - i)
        cand = cand_u ^ jnp.int32(INT_MIN)
        return jnp.where(count(lambda k, s: k >= cand) >= ksel, cand_u, tu)

    thr_u = lax.fori_loop(0, 32, thr_bit, jnp.zeros((1, BLOCK), jnp.int32))
    thr = thr_u ^ jnp.int32(INT_MIN)
    n_gt = count(lambda k, s: k > thr)
    n_ge = count(lambda k, s: k >= thr)
    need = ksel - n_gt

    def tie_bit(i, x):
        cand = x | lax.shift_left(jnp.int32(1), idx_bits - 1 - i)
        return jnp.where(count(lambda k, s: (k == thr) & (s < cand)) <= need - 1, cand, x)

    def tie_search():
        return lax.fori_loop(0, idx_bits, tie_bit, jnp.zeros((1, BLOCK), jnp.int32))

    excess = jnp.max(jnp.where((n_ge > ksel) & (thr_u != 0), 1, 0))
    tie_x = lax.cond(excess > 0, tie_search,
                     lambda: jnp.full((1, BLOCK), (1 << idx_bits) - 1, jnp.int32))

    tiles = [(kh, kh * GROUP + ct * HEADS_PER_TILE)
             for kh in range(N_KV_HEADS) for ct in range(N_COL_TILES)]
    qs = [_stack_heads(q_ref, first, HEADS_PER_TILE, scale=ATTN_SCALE) for _, first in tiles]
    krow = lax.broadcasted_iota(jnp.int32, (BLOCK, BLOCK), 0)
    bias_m = _mask_bias(krow < N_META)
    sm = [_dot_nt(km_ref[:, kh * HEAD_DIM:(kh + 1) * HEAD_DIM], qs[i]) for i, (kh, _) in enumerate(tiles)]
    vm = [_values_with_ones(vtm_ref[...], kh) for kh in range(N_KV_HEADS)]
    m0 = []
    for i, (kh, _) in enumerate(tiles):
        s = sm[i].astype(BF16) + bias_m
        m = jnp.max(s, axis=0, keepdims=True)
        m0.append(m)
        acc_s[i] = _dot(vm[kh], jnp.exp(s - m))

    ab = ATT_CHUNK_BLOCKS
    ak = ab * BLOCK
    arow = lax.broadcasted_iota(jnp.int32, (ak, BLOCK), 0)

    def attend(c, ms):
        keys = key_s[pl.ds(c * ab, ab)].reshape(ak, BLOCK)
        s_idx = c * ak + arow
        sel = ((keys > thr) | ((keys == thr) & (s_idx <= tie_x))) & (s_idx <= t_row)
        bias = _mask_bias(sel)
        kc = k3_ref[pl.ds(c * ab, ab)].reshape(ak, KV_COLS)
        vtc = jnp.concatenate([vt3_ref[c * ab + g] for g in range(ab)], axis=1)
        vs = [_values_with_ones(vtc, kh) for kh in range(N_KV_HEADS)]
        ks = [kc[:, kh * HEAD_DIM:(kh + 1) * HEAD_DIM] for kh in range(N_KV_HEADS)]
        logits = lambda i: _dot_nt(ks[tiles[i][0]], qs[i])
        ss = [logits(i) for i in range(PIPE)]
        new_m = []
        for i, (kh, _) in enumerate(tiles):
            s = ss[i].astype(BF16) + bias
            m = jnp.maximum(ms[i], jnp.max(s, axis=0, keepdims=True))
            alpha = jnp.exp(ms[i].astype(F32) - m.astype(F32))
            p = jnp.exp(s - m)
            new_m.append(m)
            if i + PIPE < len(tiles):
                ss.append(logits(i + PIPE))
            acc_s[i] = acc_s[i] * alpha + _dot(vs[kh], p)
        return tuple(new_m)

    lax.fori_loop(0, n // ab + 1, attend, tuple(m0))
    for i, (_, first) in enumerate(tiles):
        _store_tile_t(o_ref, first, acc_s[i])


def _dsa_attn(q, qi, wt, k3, vt3, ki3, km, vtm, *, batch):
    m = q.shape[0]
    nb = m // BLOCK // batch
    seq = nb * BLOCK
    assert nb % IDX_CHUNK_BLOCKS == 0 and IDX_CHUNK_BLOCKS % ATT_CHUNK_BLOCKS == 0
    ksel = min(TOPK_MAX, seq // 4)
    idx_bits = max(1, int(np.ceil(np.log2(seq))))
    qmap = lambda b, n: (b * nb + n, 0)
    bmap = lambda b, n: (b, 0, 0)
    return pl.pallas_call(
        functools.partial(_dsa_kernel, ksel=ksel, idx_bits=idx_bits), grid=(batch, nb),
        in_specs=[pl.BlockSpec((BLOCK, Q_COLS), qmap),
                  pl.BlockSpec((BLOCK, IDX_HEADS * IDX_DIM), qmap),
                  pl.BlockSpec((IDX_HEADS, BLOCK), lambda b, n: (0, b * nb + n)),
                  pl.BlockSpec((nb, BLOCK, KV_COLS), bmap),
                  pl.BlockSpec((nb, KV_COLS, BLOCK), bmap),
                  pl.BlockSpec((nb, BLOCK, IDX_DIM), bmap),
                  pl.BlockSpec((BLOCK, KV_COLS), lambda b, n: (0, 0)),
                  pl.BlockSpec((None, KV_COLS, BLOCK), lambda b, n: (0, 0, 0))],
        out_specs=pl.BlockSpec((None, Q_COLS, BLOCK), lambda b, n: (b * nb + n, 0, 0)),
        out_shape=jax.ShapeDtypeStruct((m // BLOCK, Q_COLS, BLOCK), BF16),
        scratch_shapes=[pltpu.VMEM((nb, BLOCK, BLOCK), jnp.int32),
                        pltpu.VMEM((N_KV_HEADS * N_COL_TILES, V_ROWS, COL_TILE), F32)],
        compiler_params=_cparams("parallel", "arbitrary"), name="dsa_attn",
    )(q, qi, wt, k3, vt3, ki3, km, vtm)


def _outproj_kernel(a_ref, w_ref, h_ref, g_ref, b_ref, wr_ref, br_ref, tri_ref,
                    h1_ref, h1t_ref, route_ref, cnt_ref, carry_s, *, alpha):
    i = pl.program_id(0)

    @pl.when(i == 0)
    def _():
        carry_s[...] = jnp.zeros_like(carry_s)

    nblk = a_ref.shape[0]
    gb = max(1, nblk // OUTPROJ_GROUPS)
    gr = gb * BLOCK
    n_grp = nblk // gb
    ys = [jnp.concatenate([_dot_tn(a_ref[g * gb + j], w_ref[...]) for j in range(gb)], axis=0)
          for g in range(n_grp)]
    lane = lax.broadcasted_iota(jnp.int32, (gr, LANES), 1)
    picks = []
    for g in range(n_grp):
        rs = pl.ds(g * gr, gr)
        h1 = _layer_norm(alpha * h_ref[rs, :] + ys[g], g_ref[...], b_ref[...])
        h1_ref[rs, :] = h1
        _store_token_tiles(h1t_ref.at[pl.ds(g * gr * TOKEN_ROWS, gr * TOKEN_ROWS)], h1)
        rem = _dot(h1.astype(BF16), wr_ref[...]) + br_ref[...]
        idxs, vals = [], []
        for _ in range(TOP_K):
            mx = jnp.max(rem, axis=-1, keepdims=True)
            ix = jnp.min(jnp.where(rem == mx, lane, LANES), axis=-1, keepdims=True)
            idxs.append(ix)
            vals.append(mx)
            rem = jnp.where(lane == ix, -jnp.inf, rem)
        exps = [jnp.exp(v - vals[0]) for v in vals]
        den = exps[0] + exps[1] + exps[2] + exps[3]
        picks.append((idxs, [e / den for e in exps], jnp.where(rem == -jnp.inf, 1.0, 0.0)))
    sel = jnp.concatenate([p[2] for p in picks], axis=0)
    before = _dot(tri_ref[...], sel.astype(BF16)) + carry_s[...]
    for g, (idxs, gates, _) in enumerate(picks):
        bg = before[g * gr:(g + 1) * gr]
        route = jnp.zeros((gr, LANES), F32)
        for k in range(TOP_K):
            rank = jnp.sum(jnp.where(lane == idxs[k], bg, 0.0), axis=-1, keepdims=True)
            route = jnp.where(lane == k, idxs[k].astype(F32), route)
            route = jnp.where(lane == TOP_K + k, gates[k], route)
            route = jnp.where(lane == 2 * TOP_K + k, rank, route)
        route_ref[pl.ds(g * gr, gr), :] = route
    carry_s[...] = carry_s[...] + jnp.sum(sel, axis=0, keepdims=True)
    cnt_ref[...] = carry_s[...]


def _outproj(a_t, w, h, g, b, wr, br, *, alpha):
    m, d = h.shape
    tm = min(ROW_TILE, m)
    tri = jnp.tril(jnp.ones((tm, tm), BF16), -1)
    row = lambda i: (i, 0)
    const = lambda i: (0, 0)
    full = lambda x: pl.BlockSpec(x.shape, const)
    return pl.pallas_call(
        functools.partial(_outproj_kernel, alpha=alpha), grid=(m // tm,),
        in_specs=[pl.BlockSpec((tm // BLOCK, Q_COLS, BLOCK), lambda i: (i, 0, 0)), full(w),
                  pl.BlockSpec((tm, d), row), full(g), full(b), full(wr), full(br), full(tri)],
        out_specs=[pl.BlockSpec((tm, d), row), pl.BlockSpec((tm * TOKEN_ROWS, LANES), row),
                   pl.BlockSpec((tm, LANES), row), pl.BlockSpec((1, LANES), const)],
        out_shape=[jax.ShapeDtypeStruct((m, d), F32), jax.ShapeDtypeStruct((m * TOKEN_ROWS, LANES), F32),
                   jax.ShapeDtypeStruct((m, LANES), F32), jax.ShapeDtypeStruct((1, LANES), F32)],
        scratch_shapes=[pltpu.VMEM((1, LANES), F32)],
        compiler_params=_cparams("arbitrary"), name="outproj_router")(a_t, w, h, g, b, wr, br, tri)


def _expert_ffn(x, wgu_ref, bgu_ref, wd_ref, bd_ref):
    ff = wd_ref.shape[0]
    acc = jnp.zeros((x.shape[0], wd_ref.shape[1]), F32)
    for c in range(ff // FF_CHUNK):
        gs = slice(c * FF_CHUNK, (c + 1) * FF_CHUNK)
        us = slice(ff + c * FF_CHUNK, ff + (c + 1) * FF_CHUNK)
        gate = jnp.minimum(_dot(x, wgu_ref[:, gs]) + bgu_ref[:, gs], SWIGLU_LIMIT)
        up = jnp.clip(_dot(x, wgu_ref[:, us]) + bgu_ref[:, us], -SWIGLU_LIMIT, SWIGLU_LIMIT)
        act = gate * jax.nn.sigmoid(SWIGLU_ALPHA * gate) * (up + 1.0)
        acc = acc + _dot(act.astype(BF16), wd_ref[gs, :])
    return acc + bd_ref[...]


def _experts_kernel(be_ref, nu_ref, tok_ref, tok_next_ref, dst_ref, h_ref, wgu_ref, bgu_ref, wd_ref,
                    bd_ref, yk_ref, xbuf, ybuf, gsem, ssem):
    del be_ref
    i = pl.program_id(0)
    n_used = nu_ref[0]
    slot = i % 2
    rows = xbuf.shape[1] // TOKEN_ROWS
    tile = lambda t: pl.ds(pl.multiple_of(t * TOKEN_ROWS, TOKEN_ROWS), TOKEN_ROWS)

    def gather(src_ref, s):
        def body(r8, c):
            for u in range(DMA_UNROLL):
                r = r8 * DMA_UNROLL + u
                pltpu.make_async_copy(h_ref.at[tile(src_ref[0, 0, r])],
                                      xbuf.at[s, tile(r)], gsem.at[s]).start()
            return c
        lax.fori_loop(0, rows // DMA_UNROLL, body, 0)

    def scatter(s):
        def body(r8, c):
            for u in range(DMA_UNROLL):
                r = r8 * DMA_UNROLL + u
                pltpu.make_async_copy(ybuf.at[s, tile(r)],
                                      yk_ref.at[tile(dst_ref[0, 0, r])], ssem.at[s]).start()
            return c
        lax.fori_loop(0, rows // DMA_UNROLL, body, 0)

    def wait_gather(s):
        pltpu.make_async_copy(xbuf.at[s], xbuf.at[s], gsem.at[s]).wait()

    def wait_scatter(s):
        pltpu.make_async_copy(ybuf.at[s], ybuf.at[s], ssem.at[s]).wait()

    @pl.when(i == 0)
    def _():
        for s in range(2):
            ybuf[s] = jnp.zeros(ybuf.shape[1:], F32)
        fills = [pltpu.make_async_copy(
            ybuf.at[s], yk_ref.at[pl.ds(yk_ref.shape[0] - (2 - s) * ybuf.shape[1], ybuf.shape[1])],
            ssem.at[s]) for s in range(2)]
        for f in fills:
            f.start()
        for f in fills:
            f.wait()

    @pl.when((i == 0) & (n_used > 0))
    def _():
        gather(tok_ref, 0)

    @pl.when(i < n_used)
    def _():
        wait_gather(slot)

        @pl.when(i + 1 < n_used)
        def _():
            gather(tok_next_ref, 1 - slot)

        @pl.when(i >= 2)
        def _():
            wait_scatter(slot)

        x = _load_token_tiles(xbuf.at[slot], rows).astype(BF16)
        _store_token_tiles(ybuf.at[slot], _expert_ffn(x, wgu_ref, bgu_ref, wd_ref, bd_ref))
        scatter(slot)

    @pl.when(i == pl.num_programs(0) - 1)
    def _():
        @pl.when(n_used >= 1)
        def _():
            wait_scatter((n_used - 1) % 2)

        @pl.when(n_used >= 2)
        def _():
            wait_scatter(n_used % 2)


def _experts(blk_e, n_used, tok3, dst3, ht, wgu, bgu, wd, bd, n_out_rows):
    n_blk, _, rows = tok3.shape
    ff, d = wd.shape[1:]
    assert d == TOKEN_ROWS * LANES
    emap = lambda i, be, nu: (be[i], 0, 0)
    smem = lambda imap: pl.BlockSpec((1, 1, rows), imap, memory_space=pltpu.SMEM)
    grid_spec = pltpu.PrefetchScalarGridSpec(
        num_scalar_prefetch=2, grid=(n_blk,),
        in_specs=[smem(lambda i, be, nu: (i, 0, 0)),
                  smem(lambda i, be, nu: (jnp.minimum(i + 1, n_blk - 1), 0, 0)),
                  smem(lambda i, be, nu: (i, 0, 0)),
                  pl.BlockSpec(memory_space=pl.ANY),
                  pl.BlockSpec((None, d, 2 * ff), emap), pl.BlockSpec((None, 1, 2 * ff), emap),
                  pl.BlockSpec((None, ff, d), emap), pl.BlockSpec((None, 1, d), emap)],
        out_specs=pl.BlockSpec(memory_space=pl.ANY),
        scratch_shapes=[pltpu.VMEM((2, rows * TOKEN_ROWS, LANES), F32),
                        pltpu.VMEM((2, rows * TOKEN_ROWS, LANES), F32),
                        pltpu.SemaphoreType.DMA((2,)), pltpu.SemaphoreType.DMA((2,))])
    return pl.pallas_call(
        _experts_kernel, grid_spec=grid_spec,
        out_shape=jax.ShapeDtypeStruct((n_out_rows * TOKEN_ROWS, LANES), F32),
        compiler_params=_cparams("arbitrary"), name="moe_experts",
    )(blk_e, n_used, tok3, tok3, dst3, ht, wgu, bgu, wd, bd)


def _combine_kernel(route_ref, h_ref, g_ref, b_ref, y0_ref, y1_ref, y2_ref, y3_ref, o_ref, *, alpha):
    z = alpha * h_ref[...]
    for k, y_ref in enumerate((y0_ref, y1_ref, y2_ref, y3_ref)):
        z = z + route_ref[:, TOP_K + k:TOP_K + k + 1] * _load_token_tiles(y_ref, h_ref.shape[0])
    o_ref[...] = _layer_norm(z, g_ref[...], b_ref[...])


def _combine(route, h, g, b, yk, *, alpha):
    m, d = h.shape
    tm = min(ROW_TILE, m)
    nt = m // tm
    row = lambda i: (i, 0)
    const = lambda i: (0, 0)
    choice = lambda k: pl.BlockSpec((tm * TOKEN_ROWS, LANES), lambda i: (k * nt + i, 0))
    return pl.pallas_call(
        functools.partial(_combine_kernel, alpha=alpha), grid=(nt,),
        in_specs=[pl.BlockSpec((tm, LANES), row), pl.BlockSpec((tm, d), row),
                  pl.BlockSpec(g.shape, const), pl.BlockSpec(b.shape, const)]
                 + [choice(k) for k in range(TOP_K)],
        out_specs=pl.BlockSpec((tm, d), row),
        out_shape=jax.ShapeDtypeStruct((m, d), F32),
        compiler_params=_cparams("parallel"), name="moe_combine")(route, h, g, b, yk, yk, yk, yk)


def _meta_moe_kernel(hm_ref, gm_ref, wgu_ref, bgu_ref, wd_ref, bd_ref, g_ref, b_ref, o_ref, acc_s,
                     *, alpha):
    e = pl.program_id(0)

    @pl.when(e == 0)
    def _():
        acc_s[...] = jnp.zeros_like(acc_s)

    y = _expert_ffn(hm_ref[...].astype(BF16), wgu_ref, bgu_ref, wd_ref, bd_ref)
    lane = lax.broadcasted_iota(jnp.int32, gm_ref.shape, 1)
    ge = jnp.sum(jnp.where(lane == e, gm_ref[...], 0.0), axis=-1, keepdims=True)
    acc_s[...] = acc_s[...] + ge * y

    @pl.when(e == pl.num_programs(0) - 1)
    def _():
        o_ref[...] = _layer_norm(alpha * hm_ref[...] + acc_s[...], g_ref[...], b_ref[...])


def _meta_moe(hm, gm, wgu, bgu, wd, bd, g, b, *, alpha):
    m, d = hm.shape
    ff = wd.shape[1]
    const = lambda e: (0, 0)
    emap = lambda e: (e, 0, 0)
    return pl.pallas_call(
        functools.partial(_meta_moe_kernel, alpha=alpha), grid=(wgu.shape[0],),
        in_specs=[pl.BlockSpec((m, d), const), pl.BlockSpec((m, LANES), const),
                  pl.BlockSpec((None, d, 2 * ff), emap), pl.BlockSpec((None, 1, 2 * ff), emap),
                  pl.BlockSpec((None, ff, d), emap), pl.BlockSpec((None, 1, d), emap),
                  pl.BlockSpec(g.shape, const), pl.BlockSpec(b.shape, const)],
        out_specs=pl.BlockSpec((m, d), const),
        out_shape=jax.ShapeDtypeStruct((m, d), F32),
        scratch_shapes=[pltpu.VMEM((m, d), F32)],
        compiler_params=_cparams("arbitrary"), name="meta_experts")(hm, gm, wgu, bgu, wd, bd, g, b)


def _moe(h1, h1t, route, cnt, hm1, route_m, ep, g, b, *, alpha):
    m, d = h1.shape
    n_blk = -(-m * TOP_K // MOE_ROWS) + N_EXPERTS
    e = route[:, :TOP_K].astype(jnp.int32)
    rank = route[:, 2 * TOP_K:3 * TOP_K].astype(jnp.int32)
    counts = cnt[0, :N_EXPERTS].astype(jnp.int32)
    padded = (counts + MOE_ROWS - 1) // MOE_ROWS * MOE_ROWS
    pend = jnp.cumsum(padded)
    dest = (pend - padded)[e] + rank
    blk_start = jnp.arange(n_blk, dtype=jnp.int32) * MOE_ROWS
    blk_e = jnp.minimum(jnp.sum((pend[None, :] <= blk_start[:, None]).astype(jnp.int32), axis=1),
                        N_EXPERTS - 1)
    n_used = (pend[-1:] // MOE_ROWS).astype(jnp.int32)
    n_sorted = n_blk * MOE_ROWS
    inv = jnp.zeros((n_sorted,), jnp.int32).at[dest.reshape(-1)].set(
        jnp.arange(1, m * TOP_K + 1, dtype=jnp.int32))
    pair = jnp.maximum(inv - 1, 0)
    pos = jnp.arange(n_sorted, dtype=jnp.int32)
    pad_row = TOP_K * m + (pos // MOE_ROWS % 2) * MOE_ROWS + pos % MOE_ROWS
    tok3 = (pair // TOP_K).reshape(n_blk, 1, MOE_ROWS)
    dst3 = jnp.where(inv > 0, (pair % TOP_K) * m + pair // TOP_K, pad_row).reshape(n_blk, 1, MOE_ROWS)
    yk = _experts(blk_e, n_used, tok3, dst3, h1t, ep["wgu"], ep["bgu"], ep["wd"], ep["bd"],
                  TOP_K * m + 2 * MOE_ROWS)
    h2 = _combine(route, h1, g, b, yk, alpha=alpha)
    em = route_m[:, :TOP_K].astype(jnp.int32)
    gm = jnp.sum(jax.nn.one_hot(em, LANES, dtype=F32) * route_m[:, TOP_K:2 * TOP_K, None], axis=1)
    hm2 = _meta_moe(hm1, gm, ep["wgu"], ep["bgu"], ep["wd"], ep["bd"], g, b, alpha=alpha)
    return h2, hm2


def _split_in_weights(w, b, dsa, g=None, bn=None):
    cuts = np.cumsum([Q_COLS, KV_COLS, KV_COLS, IDX_HEADS * IDX_DIM, IDX_DIM]).tolist()
    wb = w.astype(BF16)
    row = lambda v: v.reshape(1, -1)
    col = lambda v: v.reshape(-1, 1)
    wp = {"wq": wb[:, :cuts[0]], "bq": row(b[:cuts[0]]),
          "wk": wb[:, cuts[0]:cuts[1]], "bk": row(b[cuts[0]:cuts[1]]),
          "wvt": wb[:, cuts[1]:cuts[2]].T, "bvt": col(b[cuts[1]:cuts[2]])}
    if dsa:
        pad = LANES - IDX_DIM
        rpad = 2 * IDX_HEADS - IDX_HEADS
        wp.update({
            "wqi": wb[:, cuts[2]:cuts[3]], "bqi": row(b[cuts[2]:cuts[3]]),
            "wk": jnp.concatenate([wp["wk"], jnp.pad(wb[:, cuts[3]:cuts[4]], ((0, 0), (0, pad)))], 1),
            "bk": jnp.concatenate([wp["bk"], row(jnp.pad(b[cuts[3]:cuts[4]], (0, pad)))], 1),
            "wvt": jnp.concatenate([wp["wvt"], jnp.pad(wb[:, cuts[4]:].T, ((0, rpad), (0, 0)))], 0),
            "bvt": jnp.concatenate([wp["bvt"], jnp.pad(col(b[cuts[4]:]), ((0, rpad), (0, 0)))], 0),
            "g": row(jnp.pad(g, (0, pad))), "bn": row(jnp.pad(bn, (0, pad)))})
    return wp


def kernel(x, meta_tokens, w_in_a, b_in_a, sinks_a, w_out_a, w_in_b, b_in_b, idx_k_norm_g, idx_k_norm_b, w_out_b, ln_mix_g, ln_mix_b, w_router, b_router, w_gate_up, b_gate_up, w_down, b_down, ln_ffn_g, ln_ffn_b):
    batch, seq, d = x.shape
    depth = ln_mix_g.shape[0]
    alpha = float((2 * depth) ** 0.25)
    row = lambda v: v.reshape(1, -1)
    nb = seq // BLOCK

    h = x.reshape(batch * seq, d)
    hm = jnp.zeros((BLOCK, d), F32).at[:N_META].set(meta_tokens.astype(F32))
    tab_q = _rope_tables(jnp.arange(N_META, N_META + seq), ROT_DIM)
    tab_i = _rope_tables(jnp.arange(N_META, N_META + seq), IDX_ROT_DIM)
    tab_qm = _rope_tables(jnp.arange(BLOCK), ROT_DIM)
    tab_im = _rope_tables(jnp.arange(BLOCK), IDX_ROT_DIM)
    no_sinks = jnp.full((N_HEADS,), NEG, F32)

    for i in range(depth):
        j = i // 2
        dsa = i % 2 == 1
        if dsa:
            wp = _split_in_weights(w_in_b[j], b_in_b[j], True, idx_k_norm_g[j], idx_k_norm_b[j])
            w_out, sinks = w_out_b[j], no_sinks
        else:
            wp = _split_in_weights(w_in_a[j], b_in_a[j], False)
            w_out, sinks = w_out_a[j], sinks_a[j].astype(F32)
        pr = _inproj(h, wp, tab_q, tab_i, dsa=dsa, seq=seq)
        pm = _inproj(hm, wp, tab_qm, tab_im, dsa=dsa, seq=BLOCK)
        q, k, vt3 = pr[:3]
        qm, km, vtm = pm[:3]
        if dsa:
            qi, ki, wt = pr[3:]
            att = _dsa_attn(q, qi, wt, k.reshape(batch * nb, BLOCK, KV_COLS), vt3,
                            ki.reshape(batch * nb, BLOCK, IDX_DIM), km, vtm, batch=batch)
        else:
            att = _local_attn(sinks, q, k, vt3, km, vtm, banded=True, batch=batch)
        att_m = _local_attn(sinks, qm, km, vtm, km, vtm, banded=False, batch=1)

        w_out = w_out.astype(BF16)
        wr = jnp.pad(w_router[i].astype(BF16), ((0, 0), (0, LANES - N_EXPERTS)))
        br = row(jnp.pad(b_router[i].astype(F32), (0, LANES - N_EXPERTS), constant_values=NEG))
        g1, b1 = row(ln_mix_g[i]), row(ln_mix_b[i])
        h1, h1t, route, cnt = _outproj(att, w_out, h, g1, b1, wr, br, alpha=alpha)
        hm1, _, route_m, _ = _outproj(att_m, w_out, hm, g1, b1, wr, br, alpha=alpha)

        ep = {"wgu": w_gate_up[i].astype(BF16), "bgu": b_gate_up[i][:, None, :],
              "wd": w_down[i].astype(BF16), "bd": b_down[i][:, None, :]}
        h, hm = _moe(h1, h1t, route, cnt, hm1, route_m, ep, row(ln_ffn_g[i]), row(ln_ffn_b[i]),
                     alpha=alpha)
    return h.reshape(batch, seq, d)
```

```python
import functools

import jax
import jax.numpy as jnp
import numpy as np
from jax import lax
from jax.experimental import pallas as pl
from jax.experimental.pallas import tpu as pltpu

N_META = 16
HEAD_DIM = 64
N_HEADS = 16
N_KV_HEADS = 2
GROUP = N_HEADS // N_KV_HEADS
Q_COLS = N_HEADS * HEAD_DIM
KV_COLS = N_KV_HEADS * HEAD_DIM
ROT_DIM = HEAD_DIM // 4
ROPE_THETA = 500000.0
ATTN_SCALE = HEAD_DIM ** -0.5
BLOCK = 128
IDX_HEADS = 8
IDX_DIM = 64
IDX_ROT_DIM = IDX_DIM // 2
IDX_SCALE = IDX_DIM ** -0.5
IDX_W_SCALE = IDX_HEADS ** -0.5
TOPK_MAX = 256
N_EXPERTS = 32
TOP_K = 4
SWIGLU_LIMIT = 7.0
SWIGLU_ALPHA = 1.702
LN_EPS = 1e-5
NEG = -1e30
INT_MIN = -2147483648

LANES = 128
TOKEN_ROWS = 8
ROW_TILE = 512
MOE_ROWS = 512
FF_CHUNK = 256
DMA_UNROLL = 8
OUTPROJ_GROUPS = 2
IDX_CHUNK_BLOCKS = 4
ATT_CHUNK_BLOCKS = 4
V_ROWS = HEAD_DIM + 16
COL_TILE = 256
HEADS_PER_TILE = COL_TILE // BLOCK
N_COL_TILES = GROUP // HEADS_PER_TILE
PIPE = 8
VMEM_LIMIT = 48 * 1024 * 1024

F32 = jnp.float32
BF16 = jnp.bfloat16


def _cparams(*sem):
    return pltpu.CompilerParams(dimension_semantics=sem, vmem_limit_bytes=VMEM_LIMIT)


def _dot(a, b):
    return jnp.dot(a, b, preferred_element_type=F32)


def _dot_nt(a, b):
    return lax.dot_general(a, b, (((1,), (1,)), ((), ())), preferred_element_type=F32)


def _dot_tn(a, b):
    return lax.dot_general(a, b, (((0,), (0,)), ((), ())), preferred_element_type=F32)


def _layer_norm(z, g, b):
    mu = jnp.mean(z, axis=-1, keepdims=True)
    d = z - mu
    var = jnp.mean(d * d, axis=-1, keepdims=True)
    return d * lax.rsqrt(var + LN_EPS) * g + b


def _store_token_tiles(ref, x):
    for j in range(TOKEN_ROWS):
        ref[pl.ds(j, x.shape[0], stride=TOKEN_ROWS), :] = x[:, j * LANES:(j + 1) * LANES]


def _load_token_tiles(ref, rows):
    return jnp.concatenate([ref[pl.ds(j, rows, stride=TOKEN_ROWS), :] for j in range(TOKEN_ROWS)], axis=1)


def _rope128(blk, tab_ref, half):
    return (blk * tab_ref[0] + pltpu.roll(blk, half, 1) * tab_ref[1]
            + pltpu.roll(blk, LANES - half, 1) * tab_ref[2])


def _inproj_kernel(*refs, dsa):
    if dsa:
        (h_ref, wq_ref, wk_ref, wvt_ref, wqi_ref, wki_ref, wwt_ref, bq_ref, bk_ref, bvt_ref,
         bqi_ref, bki_ref, bwt_ref, tabq_ref, tabi_ref, g_ref, bn_ref,
         q_ref, k_ref, vt_ref, qi_ref, ki_ref, wt_ref) = refs
    else:
        h_ref, wq_ref, wk_ref, wvt_ref, bq_ref, bk_ref, bvt_ref, tabq_ref, q_ref, k_ref, vt_ref = refs
    x = h_ref[...].astype(BF16)
    q = _dot(x, wq_ref[...]) + bq_ref[...]
    for j in range(Q_COLS // LANES):
        sl = slice(j * LANES, (j + 1) * LANES)
        q_ref[:, sl] = _rope128(q[:, sl], tabq_ref, ROT_DIM // 2).astype(BF16)
    k = _dot(x, wk_ref[...]) + bk_ref[...]
    k_ref[...] = _rope128(k, tabq_ref, ROT_DIM // 2).astype(BF16)
    vt = _dot_nt(wvt_ref[...], x) + bvt_ref[...]
    for j in range(vt_ref.shape[0]):
        vt_ref[j] = vt[:, j * BLOCK:(j + 1) * BLOCK].astype(BF16)
    if not dsa:
        return
    qi = _dot(x, wqi_ref[...]) + bqi_ref[...]
    for j in range(IDX_HEADS * IDX_DIM // LANES):
        sl = slice(j * LANES, (j + 1) * LANES)
        qi_ref[:, sl] = _rope128(qi[:, sl], tabi_ref, IDX_ROT_DIM // 2).astype(BF16)
    ki = _dot(x, wki_ref[...]) + bki_ref[...]
    real = lax.broadcasted_iota(jnp.int32, ki.shape, 1) < IDX_DIM
    mu = jnp.sum(jnp.where(real, ki, 0.0), axis=-1, keepdims=True) * (1.0 / IDX_DIM)
    d = jnp.where(real, ki - mu, 0.0)
    var = jnp.sum(d * d, axis=-1, keepdims=True) * (1.0 / IDX_DIM)
    kn = d * lax.rsqrt(var + LN_EPS) * g_ref[...] + bn_ref[...]
    ki_ref[...] = _rope128(kn, tabi_ref, IDX_ROT_DIM // 2)[:, :IDX_DIM].astype(BF16)
    wt_ref[...] = (_dot_nt(wwt_ref[...], x) + bwt_ref[...]) * IDX_W_SCALE


def _rope_tables(pos, rot_dim):
    half = rot_dim // 2
    inv_freq = ROPE_THETA ** (-jnp.arange(half, dtype=F32) / half)
    ang = pos.astype(F32)[:, None] * inv_freq[None, :]
    cos, sin = jnp.cos(ang), jnp.sin(ang)
    n = pos.shape[0]
    rest = HEAD_DIM - rot_dim
    c = jnp.concatenate([cos, cos, jnp.ones((n, rest), F32)], 1)
    s_up = jnp.concatenate([jnp.zeros((n, half), F32), sin, jnp.zeros((n, rest), F32)], 1)
    s_lo = jnp.concatenate([-sin, jnp.zeros((n, HEAD_DIM - half), F32)], 1)
    return jnp.stack([jnp.tile(t, (1, LANES // HEAD_DIM)) for t in (c, s_up, s_lo)])


def _inproj(h, wp, tabq, tabi, *, dsa, seq):
    m, d = h.shape
    tm = min(ROW_TILE, m)
    nsb = seq // tm
    row = lambda i: (i, 0)
    const = lambda i: (0, 0)
    tab_map = lambda i: (0, i % nsb, 0)
    full = lambda a: pl.BlockSpec(a.shape, const)
    names = ("wq", "wk", "wvt", "wqi", "wki", "wwt", "bq", "bk", "bvt", "bqi", "bki", "bwt") if dsa \
        else ("wq", "wk", "wvt", "bq", "bk", "bvt")
    ins = [h] + [wp[k] for k in names] + [tabq]
    in_specs = [pl.BlockSpec((tm, d), row)] + [full(wp[k]) for k in names] \
        + [pl.BlockSpec((3, tm, LANES), tab_map)]
    out_shape = [jax.ShapeDtypeStruct((m, Q_COLS), BF16), jax.ShapeDtypeStruct((m, KV_COLS), BF16),
                 jax.ShapeDtypeStruct((m // BLOCK, KV_COLS, BLOCK), BF16)]
    out_specs = [pl.BlockSpec((tm, Q_COLS), row), pl.BlockSpec((tm, KV_COLS), row),
                 pl.BlockSpec((tm // BLOCK, KV_COLS, BLOCK), lambda i: (i, 0, 0))]
    if dsa:
        ins += [tabi, wp["g"], wp["bn"]]
        in_specs += [pl.BlockSpec((3, tm, LANES), tab_map), full(wp["g"]), full(wp["bn"])]
        out_shape += [jax.ShapeDtypeStruct((m, IDX_HEADS * IDX_DIM), BF16),
                      jax.ShapeDtypeStruct((m, IDX_DIM), BF16),
                      jax.ShapeDtypeStruct((IDX_HEADS, m), F32)]
        out_specs += [pl.BlockSpec((tm, IDX_HEADS * IDX_DIM), row),
                      pl.BlockSpec((tm, IDX_DIM), row),
                      pl.BlockSpec((IDX_HEADS, tm), lambda i: (0, i))]
    return pl.pallas_call(
        functools.partial(_inproj_kernel, dsa=dsa), grid=(m // tm,), in_specs=in_specs,
        out_specs=out_specs, out_shape=out_shape, compiler_params=_cparams("parallel"),
        name="inproj_dsa" if dsa else "inproj_swa")(*ins)


def _stack_heads(ref, first, count, width=HEAD_DIM, scale=None):
    parts = [ref[:, (first + g) * width:(first + g + 1) * width] for g in range(count)]
    x = jnp.concatenate(parts, axis=0)
    return x if scale is None else x * scale


def _store_tile_t(o_ref, first_head, acc):
    out_t = acc[:HEAD_DIM] / acc[HEAD_DIM:HEAD_DIM + 1]
    for g in range(HEADS_PER_TILE):
        h = first_head + g
        o_ref[h * HEAD_DIM:(h + 1) * HEAD_DIM, :] = out_t[:, g * BLOCK:(g + 1) * BLOCK].astype(BF16)


def _values_with_ones(vt, kh):
    hs = slice(kh * HEAD_DIM, (kh + 1) * HEAD_DIM)
    return jnp.concatenate([vt[hs, :], jnp.ones((V_ROWS - HEAD_DIM, vt.shape[1]), BF16)], axis=0)


def _mask_bias(sel):
    bias = jnp.where(sel, 0.0, NEG).astype(BF16)
    return jnp.concatenate([bias] * HEADS_PER_TILE, axis=1)


def _local_attn_kernel(sink_ref, q_ref, kc_ref, kp_ref, km_ref, vtc_ref, vtp_ref, vtm_ref, o_ref,
                       *, banded):
    n = pl.program_id(1)
    r = lax.broadcasted_iota(jnp.int32, (BLOCK, BLOCK), 0)
    c = lax.broadcasted_iota(jnp.int32, (BLOCK, BLOCK), 1)
    blocks = [(kc_ref, vtc_ref, _mask_bias(r <= c))]
    if banded:
        blocks.append((kp_ref, vtp_ref, _mask_bias((r > c) & (n > 0))))
        blocks.append((km_ref, vtm_ref, _mask_bias(r < N_META)))
    sink_row = lax.broadcasted_iota(jnp.int32, (V_ROWS, COL_TILE), 0) == HEAD_DIM
    tiles = [(kh, kh * GROUP + ct * HEADS_PER_TILE)
             for kh in range(N_KV_HEADS) for ct in range(N_COL_TILES)]
    ks = [[k_ref[:, kh * HEAD_DIM:(kh + 1) * HEAD_DIM] for k_ref, _, _ in blocks]
          for kh in range(N_KV_HEADS)]
    vs = [[_values_with_ones(vt_ref[...], kh) for _, vt_ref, _ in blocks] for kh in range(N_KV_HEADS)]

    def logits(i):
        kh, first = tiles[i]
        qs = _stack_heads(q_ref, first, HEADS_PER_TILE, scale=ATTN_SCALE)
        return [_dot_nt(k, qs) for k in ks[kh]]

    sss = [logits(i) for i in range(PIPE)]
    for i, (kh, first) in enumerate(tiles):
        sink = jnp.concatenate(
            [jnp.full((1, BLOCK), sink_ref[first + g], F32) for g in range(HEADS_PER_TILE)], axis=1)
        ss = [s.astype(BF16) + bias for s, (_, _, bias) in zip(sss[i], blocks)]
        m = sink
        for s in ss:
            m = jnp.maximum(m, jnp.max(s, axis=0, keepdims=True).astype(F32))
        mb = m.astype(BF16)
        ps = [jnp.exp(s - mb) for s in ss]
        if i + PIPE < len(tiles):
            sss.append(logits(i + PIPE))
        acc = jnp.where(sink_row, jnp.exp(sink - mb.astype(F32)), 0.0)
        for p, v in zip(ps, vs[kh]):
            acc = acc + _dot(v, p)
        _store_tile_t(o_ref, first, acc)


def _local_attn(sinks, q, k, vt3, km, vtm, *, banded, batch):
    m = q.shape[0]
    nb = m // BLOCK // batch
    cur = lambda b, n, s: (b * nb + n, 0)
    prev = lambda b, n, s: (b * nb + jnp.maximum(n - 1, 0), 0)
    cur3 = lambda b, n, s: (b * nb + n, 0, 0)
    prev3 = lambda b, n, s: (b * nb + jnp.maximum(n - 1, 0), 0, 0)
    grid_spec = pltpu.PrefetchScalarGridSpec(
        num_scalar_prefetch=1, grid=(batch, nb),
        in_specs=[pl.BlockSpec((BLOCK, Q_COLS), cur), pl.BlockSpec((BLOCK, KV_COLS), cur),
                  pl.BlockSpec((BLOCK, KV_COLS), prev),
                  pl.BlockSpec((BLOCK, KV_COLS), lambda b, n, s: (0, 0)),
                  pl.BlockSpec((None, KV_COLS, BLOCK), cur3), pl.BlockSpec((None, KV_COLS, BLOCK), prev3),
                  pl.BlockSpec((None, KV_COLS, BLOCK), lambda b, n, s: (0, 0, 0))],
        out_specs=pl.BlockSpec((None, Q_COLS, BLOCK), cur3))
    return pl.pallas_call(
        functools.partial(_local_attn_kernel, banded=banded), grid_spec=grid_spec,
        out_shape=jax.ShapeDtypeStruct((m // BLOCK, Q_COLS, BLOCK), BF16),
        compiler_params=_cparams("parallel", "parallel"),
        name="swa_attn" if banded else "meta_attn")(sinks, q, k, k, km, vt3, vt3, vtm)


def _sortable(x):
    bits = lax.bitcast_convert_type(x, jnp.int32)
    return jnp.where(bits < 0, bits ^ jnp.int32(0x7FFFFFFF), bits)


def _dsa_kernel(q_ref, qi_ref, wt_ref, k3_ref, vt3_ref, ki3_ref, km_ref, vtm_ref, o_ref,
                key_s, acc_s, *, ksel, idx_bits):
    n = pl.program_id(1)
    cb = IDX_CHUNK_BLOCKS
    ck = cb * BLOCK
    n_chunks = n // cb + 1
    lane = lax.broadcasted_iota(jnp.int32, (1, BLOCK), 1)
    t_row = n * BLOCK + lane

    qis = _stack_heads(qi_ref, 0, IDX_HEADS, IDX_DIM)
    wts = [wt_ref[h:h + 1, :] for h in range(IDX_HEADS)]

    def score_chunk(c, carry):
        kic = ki3_ref[pl.ds(c * cb, cb)].reshape(ck, IDX_DIM)
        d = _dot_nt(kic, qis)
        sc = jnp.zeros((ck, BLOCK), F32)
        for h in range(IDX_HEADS):
            sc = sc + jnp.maximum(d[:, h * BLOCK:(h + 1) * BLOCK], 0.0) * wts[h]
        sc = sc * IDX_SCALE
        s_idx = c * ck + lax.broadcasted_iota(jnp.int32, (ck, BLOCK), 0)
        key = jnp.where(s_idx <= t_row, _sortable(sc), INT_MIN)
        key_s[pl.ds(c * cb, cb)] = key.reshape(cb, BLOCK, BLOCK)
        return carry

    lax.fori_loop(0, n_chunks, score_chunk, 0)

    def count(pred):
        def body(c, acc):
            keys = key_s[pl.ds(c * cb, cb)].reshape(ck, BLOCK)
            s_idx = c * ck + lax.broadcasted_iota(jnp.int32, (ck, BLOCK), 0)
            hit = jnp.where(pred(keys, s_idx), 1, 0).astype(jnp.int32)
            return acc + jnp.sum(hit.reshape(ck // 8, 8, BLOCK), axis=0)
        acc = lax.fori_loop(0, n_chunks, body, jnp.zeros((8, BLOCK), jnp.int32))
        return jnp.sum(acc, axis=0, keepdims=True)

    def thr_bit(i, tu):
        cand_u = tu | lax.shift_left(jnp.int32(1), 31 - i)
        cand = cand_u ^ jnp.int32(INT_MIN)
        return jnp.where(count(lambda k, s: k >= cand) >= ksel, cand_u, tu)

    thr_u = lax.fori_loop(0, 32, thr_bit, jnp.zeros((1, BLOCK), jnp.int32))
    thr = thr_u ^ jnp.int32(INT_MIN)
    n_gt = count(lambda k, s: k > thr)
    n_ge = count(lambda k, s: k >= thr)
    need = ksel - n_gt

    def tie_bit(i, x):
        cand = x | lax.shift_left(jnp.int32(1), idx_bits - 1 - i)
        return jnp.where(count(lambda k, s: (k == thr) & (s < cand)) <= need - 1, cand, x)

    def tie_search():
        return lax.fori_loop(0, idx_bits, tie_bit, jnp.zeros((1, BLOCK), jnp.int32))

    excess = jnp.max(jnp.where((n_ge > ksel) & (thr_u != 0), 1, 0))
    tie_x = lax.cond(excess > 0, tie_search,
                     lambda: jnp.full((1, BLOCK), (1 << idx_bits) - 1, jnp.int32))

    tiles = [(kh, kh * GROUP + ct * HEADS_PER_TILE)
             for kh in range(N_KV_HEADS) for ct in range(N_COL_TILES)]
    qs = [_stack_heads(q_ref, first, HEADS_PER_TILE, scale=ATTN_SCALE) for _, first in tiles]
    krow = lax.broadcasted_iota(jnp.int32, (BLOCK, BLOCK), 0)
    bias_m = _mask_bias(krow < N_META)
    sm = [_dot_nt(km_ref[:, kh * HEAD_DIM:(kh + 1) * HEAD_DIM], qs[i]) for i, (kh, _) in enumerate(tiles)]
    vm = [_values_with_ones(vtm_ref[...], kh) for kh in range(N_KV_HEADS)]
    m0 = []
    for i, (kh, _) in enumerate(tiles):
        s = sm[i].astype(BF16) + bias_m
        m = jnp.max(s, axis=0, keepdims=True)
        m0.append(m)
        acc_s[i] = _dot(vm[kh], jnp.exp(s - m))

    ab = ATT_CHUNK_BLOCKS
    ak = ab * BLOCK
    arow = lax.broadcasted_iota(jnp.int32, (ak, BLOCK), 0)

    def attend(c, ms):
        keys = key_s[pl.ds(c * ab, ab)].reshape(ak, BLOCK)
        s_idx = c * ak + arow
        sel = ((keys > thr) | ((keys == thr) & (s_idx <= tie_x))) & (s_idx <= t_row)
        bias = _mask_bias(sel)
        kc = k3_ref[pl.ds(c * ab, ab)].reshape(ak, KV_COLS)
        vtc = jnp.concatenate([vt3_ref[c * ab + g] for g in range(ab)], axis=1)
        vs = [_values_with_ones(vtc, kh) for kh in range(N_KV_HEADS)]
        ks = [kc[:, kh * HEAD_DIM:(kh + 1) * HEAD_DIM] for kh in range(N_KV_HEADS)]
        logits = lambda i: _dot_nt(ks[tiles[i][0]], qs[i])
        ss = [logits(i) for i in range(PIPE)]
        new_m = []
        for i, (kh, _) in enumerate(tiles):
            s = ss[i].astype(BF16) + bias
            m = jnp.maximum(ms[i], jnp.max(s, axis=0, keepdims=True))
            alpha = jnp.exp(ms[i].astype(F32) - m.astype(F32))
            p = jnp.exp(s - m)
            new_m.append(m)
            if i + PIPE < len(tiles):
                ss.append(logits(i + PIPE))
            acc_s[i] = acc_s[i] * alpha + _dot(vs[kh], p)
        return tuple(new_m)

    lax.fori_loop(0, n // ab + 1, attend, tuple(m0))
    for i, (_, first) in enumerate(tiles):
        _store_tile_t(o_ref, first, acc_s[i])


def _dsa_attn(q, qi, wt, k3, vt3, ki3, km, vtm, *, batch):
    m = q.shape[0]
    nb = m // BLOCK // batch
    seq = nb * BLOCK
    assert nb % IDX_CHUNK_BLOCKS == 0 and IDX_CHUNK_BLOCKS % ATT_CHUNK_BLOCKS == 0
    ksel = min(TOPK_MAX, seq // 4)
    idx_bits = max(1, int(np.ceil(np.log2(seq))))
    qmap = lambda b, n: (b * nb + n, 0)
    bmap = lambda b, n: (b, 0, 0)
    return pl.pallas_call(
        functools.partial(_dsa_kernel, ksel=ksel, idx_bits=idx_bits), grid=(batch, nb),
        in_specs=[pl.BlockSpec((BLOCK, Q_COLS), qmap),
                  pl.BlockSpec((BLOCK, IDX_HEADS * IDX_DIM), qmap),
                  pl.BlockSpec((IDX_HEADS, BLOCK), lambda b, n: (0, b * nb + n)),
                  pl.BlockSpec((nb, BLOCK, KV_COLS), bmap),
                  pl.BlockSpec((nb, KV_COLS, BLOCK), bmap),
                  pl.BlockSpec((nb, BLOCK, IDX_DIM), bmap),
                  pl.BlockSpec((BLOCK, KV_COLS), lambda b, n: (0, 0)),
                  pl.BlockSpec((None, KV_COLS, BLOCK), lambda b, n: (0, 0, 0))],
        out_specs=pl.BlockSpec((None, Q_COLS, BLOCK), lambda b, n: (b * nb + n, 0, 0)),
        out_shape=jax.ShapeDtypeStruct((m // BLOCK, Q_COLS, BLOCK), BF16),
        scratch_shapes=[pltpu.VMEM((nb, BLOCK, BLOCK), jnp.int32),
                        pltpu.VMEM((N_KV_HEADS * N_COL_TILES, V_ROWS, COL_TILE), F32)],
        compiler_params=_cparams("parallel", "arbitrary"), name="dsa_attn",
    )(q, qi, wt, k3, vt3, ki3, km, vtm)


def _outproj_kernel(a_ref, w_ref, h_ref, g_ref, b_ref, wr_ref, br_ref, tri_ref,
                    h1_ref, h1t_ref, route_ref, cnt_ref, carry_s, *, alpha):
    i = pl.program_id(0)

    @pl.when(i == 0)
    def _():
        carry_s[...] = jnp.zeros_like(carry_s)

    nblk = a_ref.shape[0]
    gb = max(1, nblk // OUTPROJ_GROUPS)
    gr = gb * BLOCK
    n_grp = nblk // gb
    ys = [jnp.concatenate([_dot_tn(a_ref[g * gb + j], w_ref[...]) for j in range(gb)], axis=0)
          for g in range(n_grp)]
    lane = lax.broadcasted_iota(jnp.int32, (gr, LANES), 1)
    picks = []
    for g in range(n_grp):
        rs = pl.ds(g * gr, gr)
        h1 = _layer_norm(alpha * h_ref[rs, :] + ys[g], g_ref[...], b_ref[...])
        h1_ref[rs, :] = h1
        _store_token_tiles(h1t_ref.at[pl.ds(g * gr * TOKEN_ROWS, gr * TOKEN_ROWS)], h1)
        rem = _dot(h1.astype(BF16), wr_ref[...]) + br_ref[...]
        idxs, vals = [], []
        for _ in range(TOP_K):
            mx = jnp.max(rem, axis=-1, keepdims=True)
            ix = jnp.min(jnp.where(rem == mx, lane, LANES), axis=-1, keepdims=True)
            idxs.append(ix)
            vals.append(mx)
            rem = jnp.where(lane == ix, -jnp.inf, rem)
        exps = [jnp.exp(v - vals[0]) for v in vals]
        den = exps[0] + exps[1] + exps[2] + exps[3]
        picks.append((idxs, [e / den for e in exps], jnp.where(rem == -jnp.inf, 1.0, 0.0)))
    sel = jnp.concatenate([p[2] for p in picks], axis=0)
    before = _dot(tri_ref[...], sel.astype(BF16)) + carry_s[...]
    for g, (idxs, gates, _) in enumerate(picks):
        bg = before[g * gr:(g + 1) * gr]
        route = jnp.zeros((gr, LANES), F32)
        for k in range(TOP_K):
            rank = jnp.sum(jnp.where(lane == idxs[k], bg, 0.0), axis=-1, keepdims=True)
            route = jnp.where(lane == k, idxs[k].astype(F32), route)
            route = jnp.where(lane == TOP_K + k, gates[k], route)
            route = jnp.where(lane == 2 * TOP_K + k, rank, route)
        route_ref[pl.ds(g * gr, gr), :] = route
    carry_s[...] = carry_s[...] + jnp.sum(sel, axis=0, keepdims=True)
    cnt_ref[...] = carry_s[...]


def _outproj(a_t, w, h, g, b, wr, br, *, alpha):
    m, d = h.shape
    tm = min(ROW_TILE, m)
    tri = jnp.tril(jnp.ones((tm, tm), BF16), -1)
    row = lambda i: (i, 0)
    const = lambda i: (0, 0)
    full = lambda x: pl.BlockSpec(x.shape, const)
    return pl.pallas_call(
        functools.partial(_outproj_kernel, alpha=alpha), grid=(m // tm,),
        in_specs=[pl.BlockSpec((tm // BLOCK, Q_COLS, BLOCK), lambda i: (i, 0, 0)), full(w),
                  pl.BlockSpec((tm, d), row), full(g), full(b), full(wr), full(br), full(tri)],
        out_specs=[pl.BlockSpec((tm, d), row), pl.BlockSpec((tm * TOKEN_ROWS, LANES), row),
                   pl.BlockSpec((tm, LANES), row), pl.BlockSpec((1, LANES), const)],
        out_shape=[jax.ShapeDtypeStruct((m, d), F32), jax.ShapeDtypeStruct((m * TOKEN_ROWS, LANES), F32),
                   jax.ShapeDtypeStruct((m, LANES), F32), jax.ShapeDtypeStruct((1, LANES), F32)],
        scratch_shapes=[pltpu.VMEM((1, LANES), F32)],
        compiler_params=_cparams("arbitrary"), name="outproj_router")(a_t, w, h, g, b, wr, br, tri)


def _expert_ffn(x, wgu_ref, bgu_ref, wd_ref, bd_ref):
    ff = wd_ref.shape[0]
    acc = jnp.zeros((x.shape[0], wd_ref.shape[1]), F32)
    for c in range(ff // FF_CHUNK):
        gs = slice(c * FF_CHUNK, (c + 1) * FF_CHUNK)
        us = slice(ff + c * FF_CHUNK, ff + (c + 1) * FF_CHUNK)
        gate = jnp.minimum(_dot(x, wgu_ref[:, gs]) + bgu_ref[:, gs], SWIGLU_LIMIT)
        up = jnp.clip(_dot(x, wgu_ref[:, us]) + bgu_ref[:, us], -SWIGLU_LIMIT, SWIGLU_LIMIT)
        act = gate * jax.nn.sigmoid(SWIGLU_ALPHA * gate) * (up + 1.0)
        acc = acc + _dot(act.astype(BF16), wd_ref[gs, :])
    return acc + bd_ref[...]


def _experts_kernel(be_ref, nu_ref, tok_ref, tok_next_ref, dst_ref, h_ref, wgu_ref, bgu_ref, wd_ref,
                    bd_ref, yk_ref, xbuf, ybuf, gsem, ssem):
    del be_ref
    i = pl.program_id(0)
    n_used = nu_ref[0]
    slot = i % 2
    rows = xbuf.shape[1] // TOKEN_ROWS
    tile = lambda t: pl.ds(pl.multiple_of(t * TOKEN_ROWS, TOKEN_ROWS), TOKEN_ROWS)

    def gather(src_ref, s):
        def body(r8, c):
            for u in range(DMA_UNROLL):
                r = r8 * DMA_UNROLL + u
                pltpu.make_async_copy(h_ref.at[tile(src_ref[0, 0, r])],
                                      xbuf.at[s, tile(r)], gsem.at[s]).start(priority=u % 2)
            return c
        lax.fori_loop(0, rows // DMA_UNROLL, body, 0)

    def scatter(s):
        def body(r8, c):
            for u in range(DMA_UNROLL):
                r = r8 * DMA_UNROLL + u
                pltpu.make_async_copy(ybuf.at[s, tile(r)],
                                      yk_ref.at[tile(dst_ref[0, 0, r])], ssem.at[s]).start(priority=u % 2)
            return c
        lax.fori_loop(0, rows // DMA_UNROLL, body, 0)

    def wait_gather(s):
        pltpu.make_async_copy(xbuf.at[s], xbuf.at[s], gsem.at[s]).wait()

    def wait_scatter(s):
        pltpu.make_async_copy(ybuf.at[s], ybuf.at[s], ssem.at[s]).wait()

    @pl.when(i == 0)
    def _():
        for s in range(2):
            ybuf[s] = jnp.zeros(ybuf.shape[1:], F32)
        fills = [pltpu.make_async_copy(
            ybuf.at[s], yk_ref.at[pl.ds(yk_ref.shape[0] - (2 - s) * ybuf.shape[1], ybuf.shape[1])],
            ssem.at[s]) for s in range(2)]
        for f in fills:
            f.start()
        for f in fills:
            f.wait()

    @pl.when((i == 0) & (n_used > 0))
    def _():
        gather(tok_ref, 0)

    @pl.when(i < n_used)
    def _():
        wait_gather(slot)

        @pl.when(i + 1 < n_used)
        def _():
            gather(tok_next_ref, 1 - slot)

        @pl.when(i >= 2)
        def _():
            wait_scatter(slot)

        x = _load_token_tiles(xbuf.at[slot], rows).astype(BF16)
        _store_token_tiles(ybuf.at[slot], _expert_ffn(x, wgu_ref, bgu_ref, wd_ref, bd_ref))
        scatter(slot)

    @pl.when(i == pl.num_programs(0) - 1)
    def _():
        @pl.when(n_used >= 1)
        def _():
            wait_scatter((n_used - 1) % 2)

        @pl.when(n_used >= 2)
        def _():
            wait_scatter(n_used % 2)


def _experts(blk_e, n_used, tok3, dst3, ht, wgu, bgu, wd, bd, n_out_rows):
    n_blk, _, rows = tok3.shape
    ff, d = wd.shape[1:]
    assert d == TOKEN_ROWS * LANES
    emap = lambda i, be, nu: (be[i], 0, 0)
    smem = lambda imap: pl.BlockSpec((1, 1, rows), imap, memory_space=pltpu.SMEM)
    grid_spec = pltpu.PrefetchScalarGridSpec(
        num_scalar_prefetch=2, grid=(n_blk,),
        in_specs=[smem(lambda i, be, nu: (i, 0, 0)),
                  smem(lambda i, be, nu: (jnp.minimum(i + 1, n_blk - 1), 0, 0)),
                  smem(lambda i, be, nu: (i, 0, 0)),
                  pl.BlockSpec(memory_space=pl.ANY),
                  pl.BlockSpec((None, d, 2 * ff), emap), pl.BlockSpec((None, 1, 2 * ff), emap),
                  pl.BlockSpec((None, ff, d), emap), pl.BlockSpec((None, 1, d), emap)],
        out_specs=pl.BlockSpec(memory_space=pl.ANY),
        scratch_shapes=[pltpu.VMEM((2, rows * TOKEN_ROWS, LANES), F32),
                        pltpu.VMEM((2, rows * TOKEN_ROWS, LANES), F32),
                        pltpu.SemaphoreType.DMA((2,)), pltpu.SemaphoreType.DMA((2,))])
    return pl.pallas_call(
        _experts_kernel, grid_spec=grid_spec,
        out_shape=jax.ShapeDtypeStruct((n_out_rows * TOKEN_ROWS, LANES), F32),
        compiler_params=_cparams("arbitrary"), name="moe_experts",
    )(blk_e, n_used, tok3, tok3, dst3, ht, wgu, bgu, wd, bd)


def _combine_kernel(route_ref, h_ref, g_ref, b_ref, y0_ref, y1_ref, y2_ref, y3_ref, o_ref, *, alpha):
    z = alpha * h_ref[...]
    for k, y_ref in enumerate((y0_ref, y1_ref, y2_ref, y3_ref)):
        z = z + route_ref[:, TOP_K + k:TOP_K + k + 1] * _load_token_tiles(y_ref, h_ref.shape[0])
    o_ref[...] = _layer_norm(z, g_ref[...], b_ref[...])


def _combine(route, h, g, b, yk, *, alpha):
    m, d = h.shape
    tm = min(ROW_TILE, m)
    nt = m // tm
    row = lambda i: (i, 0)
    const = lambda i: (0, 0)
    choice = lambda k: pl.BlockSpec((tm * TOKEN_ROWS, LANES), lambda i: (k * nt + i, 0))
    return pl.pallas_call(
        functools.partial(_combine_kernel, alpha=alpha), grid=(nt,),
        in_specs=[pl.BlockSpec((tm, LANES), row), pl.BlockSpec((tm, d), row),
                  pl.BlockSpec(g.shape, const), pl.BlockSpec(b.shape, const)]
                 + [choice(k) for k in range(TOP_K)],
        out_specs=pl.BlockSpec((tm, d), row),
        out_shape=jax.ShapeDtypeStruct((m, d), F32),
        compiler_params=_cparams("parallel"), name="moe_combine")(route, h, g, b, yk, yk, yk, yk)


def _meta_moe_kernel(hm_ref, gm_ref, wgu_ref, bgu_ref, wd_ref, bd_ref, g_ref, b_ref, o_ref, acc_s,
                     *, alpha):
    e = pl.program_id(0)

    @pl.when(e == 0)
    def _():
        acc_s[...] = jnp.zeros_like(acc_s)

    y = _expert_ffn(hm_ref[...].astype(BF16), wgu_ref, bgu_ref, wd_ref, bd_ref)
    lane = lax.broadcasted_iota(jnp.int32, gm_ref.shape, 1)
    ge = jnp.sum(jnp.where(lane == e, gm_ref[...], 0.0), axis=-1, keepdims=True)
    acc_s[...] = acc_s[...] + ge * y

    @pl.when(e == pl.num_programs(0) - 1)
    def _():
        o_ref[...] = _layer_norm(alpha * hm_ref[...] + acc_s[...], g_ref[...], b_ref[...])


def _meta_moe(hm, gm, wgu, bgu, wd, bd, g, b, *, alpha):
    m, d = hm.shape
    ff = wd.shape[1]
    const = lambda e: (0, 0)
    emap = lambda e: (e, 0, 0)
    return pl.pallas_call(
        functools.partial(_meta_moe_kernel, alpha=alpha), grid=(wgu.shape[0],),
        in_specs=[pl.BlockSpec((m, d), const), pl.BlockSpec((m, LANES), const),
                  pl.BlockSpec((None, d, 2 * ff), emap), pl.BlockSpec((None, 1, 2 * ff), emap),
                  pl.BlockSpec((None, ff, d), emap), pl.BlockSpec((None, 1, d), emap),
                  pl.BlockSpec(g.shape, const), pl.BlockSpec(b.shape, const)],
        out_specs=pl.BlockSpec((m, d), const),
        out_shape=jax.ShapeDtypeStruct((m, d), F32),
        scratch_shapes=[pltpu.VMEM((m, d), F32)],
        compiler_params=_cparams("arbitrary"), name="meta_experts")(hm, gm, wgu, bgu, wd, bd, g, b)


def _moe(h1, h1t, route, cnt, hm1, route_m, ep, g, b, *, alpha):
    m, d = h1.shape
    n_blk = -(-m * TOP_K // MOE_ROWS) + N_EXPERTS
    e = route[:, :TOP_K].astype(jnp.int32)
    rank = route[:, 2 * TOP_K:3 * TOP_K].astype(jnp.int32)
    counts = cnt[0, :N_EXPERTS].astype(jnp.int32)
    padded = (counts + MOE_ROWS - 1) // MOE_ROWS * MOE_ROWS
    pend = jnp.cumsum(padded)
    dest = (pend - padded)[e] + rank
    blk_start = jnp.arange(n_blk, dtype=jnp.int32) * MOE_ROWS
    blk_e = jnp.minimum(jnp.sum((pend[None, :] <= blk_start[:, None]).astype(jnp.int32), axis=1),
                        N_EXPERTS - 1)
    n_used = (pend[-1:] // MOE_ROWS).astype(jnp.int32)
    n_sorted = n_blk * MOE_ROWS
    inv = jnp.zeros((n_sorted,), jnp.int32).at[dest.reshape(-1)].set(
        jnp.arange(1, m * TOP_K + 1, dtype=jnp.int32))
    pair = jnp.maximum(inv - 1, 0)
    pos = jnp.arange(n_sorted, dtype=jnp.int32)
    pad_row = TOP_K * m + (pos // MOE_ROWS % 2) * MOE_ROWS + pos % MOE_ROWS
    tok3 = (pair // TOP_K).reshape(n_blk, 1, MOE_ROWS)
    dst3 = jnp.where(inv > 0, (pair % TOP_K) * m + pair // TOP_K, pad_row).reshape(n_blk, 1, MOE_ROWS)
    yk = _experts(blk_e, n_used, tok3, dst3, h1t, ep["wgu"], ep["bgu"], ep["wd"], ep["bd"],
                  TOP_K * m + 2 * MOE_ROWS)
    h2 = _combine(route, h1, g, b, yk, alpha=alpha)
    em = route_m[:, :TOP_K].astype(jnp.int32)
    gm = jnp.sum(jax.nn.one_hot(em, LANES, dtype=F32) * route_m[:, TOP_K:2 * TOP_K, None], axis=1)
    hm2 = _meta_moe(hm1, gm, ep["wgu"], ep["bgu"], ep["wd"], ep["bd"], g, b, alpha=alpha)
    return h2, hm2


def _split_in_weights(w, b, dsa, g=None, bn=None):
    cuts = np.cumsum([Q_COLS, KV_COLS, KV_COLS, IDX_HEADS * IDX_DIM, IDX_DIM]).tolist()
    wb = w.astype(BF16)
    row = lambda v: v.reshape(1, -1)
    col = lambda v: v.reshape(-1, 1)
    wp = {"wq": wb[:, :cuts[0]], "bq": row(b[:cuts[0]]),
          "wk": wb[:, cuts[0]:cuts[1]], "bk": row(b[cuts[0]:cuts[1]]),
          "wvt": wb[:, cuts[1]:cuts[2]].T, "bvt": col(b[cuts[1]:cuts[2]])}
    if dsa:
        pad = LANES - IDX_DIM
        wp.update({
            "wqi": wb[:, cuts[2]:cuts[3]], "bqi": row(b[cuts[2]:cuts[3]]),
            "wki": jnp.pad(wb[:, cuts[3]:cuts[4]], ((0, 0), (0, pad))),
            "bki": row(jnp.pad(b[cuts[3]:cuts[4]], (0, pad))),
            "wwt": wb[:, cuts[4]:].T, "bwt": col(b[cuts[4]:]),
            "g": row(jnp.pad(g, (0, pad))), "bn": row(jnp.pad(bn, (0, pad)))})
    return wp


def kernel(x, meta_tokens, w_in_a, b_in_a, sinks_a, w_out_a, w_in_b, b_in_b, idx_k_norm_g, idx_k_norm_b, w_out_b, ln_mix_g, ln_mix_b, w_router, b_router, w_gate_up, b_gate_up, w_down, b_down, ln_ffn_g, ln_ffn_b):
    batch, seq, d = x.shape
    depth = ln_mix_g.shape[0]
    alpha = float((2 * depth) ** 0.25)
    row = lambda v: v.reshape(1, -1)
    nb = seq // BLOCK

    h = x.reshape(batch * seq, d)
    hm = jnp.zeros((BLOCK, d), F32).at[:N_META].set(meta_tokens.astype(F32))
    tab_q = _rope_tables(jnp.arange(N_META, N_META + seq), ROT_DIM)
    tab_i = _rope_tables(jnp.arange(N_META, N_META + seq), IDX_ROT_DIM)
    tab_qm = _rope_tables(jnp.arange(BLOCK), ROT_DIM)
    tab_im = _rope_tables(jnp.arange(BLOCK), IDX_ROT_DIM)
    no_sinks = jnp.full((N_HEADS,), NEG, F32)

    for i in range(depth):
        j = i // 2
        dsa = i % 2 == 1
        if dsa:
            wp = _split_in_weights(w_in_b[j], b_in_b[j], True, idx_k_norm_g[j], idx_k_norm_b[j])
            w_out, sinks = w_out_b[j], no_sinks
        else:
            wp = _split_in_weights(w_in_a[j], b_in_a[j], False)
            w_out, sinks = w_out_a[j], sinks_a[j].astype(F32)
        pr = _inproj(h, wp, tab_q, tab_i, dsa=dsa, seq=seq)
        pm = _inproj(hm, wp, tab_qm, tab_im, dsa=dsa, seq=BLOCK)
        q, k, vt3 = pr[:3]
        qm, km, vtm = pm[:3]
        if dsa:
            qi, ki, wt = pr[3:]
            att = _dsa_attn(q, qi, wt, k.reshape(batch * nb, BLOCK, KV_COLS), vt3,
                            ki.reshape(batch * nb, BLOCK, IDX_DIM), km, vtm, batch=batch)
        else:
            att = _local_attn(sinks, q, k, vt3, km, vtm, banded=True, batch=batch)
        att_m = _local_attn(sinks, qm, km, vtm, km, vtm, banded=False, batch=1)

        w_out = w_out.astype(BF16)
        wr = jnp.pad(w_router[i].astype(BF16), ((0, 0), (0, LANES - N_EXPERTS)))
        br = row(jnp.pad(b_router[i].astype(F32), (0, LANES - N_EXPERTS), constant_values=NEG))
        g1, b1 = row(ln_mix_g[i]), row(ln_mix_b[i])
        h1, h1t, route, cnt = _outproj(att, w_out, h, g1, b1, wr, br, alpha=alpha)
        hm1, _, route_m, _ = _outproj(att_m, w_out, hm, g1, b1, wr, br, alpha=alpha)

        ep = {"wgu": w_gate_up[i].astype(BF16), "bgu": b_gate_up[i][:, None, :],
              "wd": w_down[i].astype(BF16), "bd": b_down[i][:, None, :]}
        h, hm = _moe(h1, h1t, route, cnt, hm1, route_m, ep, row(ln_ffn_g[i]), row(ln_ffn_b[i]),
                     alpha=alpha)
    return h.reshape(batch, seq, d)
```

```python
import functools

import jax
import jax.numpy as jnp
import numpy as np
from jax import lax
from jax.experimental import pallas as pl
from jax.experimental.pallas import tpu as pltpu

N_META = 16
HEAD_DIM = 64
N_HEADS = 16
N_KV_HEADS = 2
GROUP = N_HEADS // N_KV_HEADS
Q_COLS = N_HEADS * HEAD_DIM
KV_COLS = N_KV_HEADS * HEAD_DIM
ROT_DIM = HEAD_DIM // 4
ROPE_THETA = 500000.0
ATTN_SCALE = HEAD_DIM ** -0.5
BLOCK = 128
IDX_HEADS = 8
IDX_DIM = 64
IDX_ROT_DIM = IDX_DIM // 2
IDX_SCALE = IDX_DIM ** -0.5
IDX_W_SCALE = IDX_HEADS ** -0.5
TOPK_MAX = 256
N_EXPERTS = 32
TOP_K = 4
SWIGLU_LIMIT = 7.0
SWIGLU_ALPHA = 1.702
LN_EPS = 1e-5
NEG = -1e30
INT_MIN = -2147483648

LANES = 128
TOKEN_ROWS = 8
ROW_TILE = 512
MOE_ROWS = 512
FF_CHUNK = 256
DMA_UNROLL = 8
OUTPROJ_GROUPS = 2
OUTPROJ_ROWS = ROW_TILE // 2
IDX_CHUNK_BLOCKS = 4
ATT_CHUNK_BLOCKS = 4
V_ROWS = HEAD_DIM + 16
COL_TILE = 256
HEADS_PER_TILE = COL_TILE // BLOCK
N_COL_TILES = GROUP // HEADS_PER_TILE
PIPE = 8
VMEM_LIMIT = 48 * 1024 * 1024

F32 = jnp.float32
BF16 = jnp.bfloat16


def _cparams(*sem):
    return pltpu.CompilerParams(dimension_semantics=sem, vmem_limit_bytes=VMEM_LIMIT)


def _dot(a, b):
    return jnp.dot(a, b, preferred_element_type=F32)


def _dot_nt(a, b):
    return lax.dot_general(a, b, (((1,), (1,)), ((), ())), preferred_element_type=F32)


def _dot_tn(a, b):
    return lax.dot_general(a, b, (((0,), (0,)), ((), ())), preferred_element_type=F32)


def _layer_norm(z, g, b):
    mu = jnp.mean(z, axis=-1, keepdims=True)
    d = z - mu
    var = jnp.mean(d * d, axis=-1, keepdims=True)
    return d * lax.rsqrt(var + LN_EPS) * g + b


def _store_token_tiles(ref, x):
    for j in range(TOKEN_ROWS):
        ref[pl.ds(j, x.shape[0], stride=TOKEN_ROWS), :] = x[:, j * LANES:(j + 1) * LANES]


def _load_token_tiles(ref, rows):
    return jnp.concatenate([ref[pl.ds(j, rows, stride=TOKEN_ROWS), :] for j in range(TOKEN_ROWS)], axis=1)


def _rope128(blk, tab_ref, half):
    return (blk * tab_ref[0] + pltpu.roll(blk, half, 1) * tab_ref[1]
            + pltpu.roll(blk, LANES - half, 1) * tab_ref[2])


def _inproj_kernel(*refs, dsa):
    if dsa:
        (h_ref, wq_ref, wk_ref, wvt_ref, wqi_ref, wki_ref, wwt_ref, bq_ref, bk_ref, bvt_ref,
         bqi_ref, bki_ref, bwt_ref, tabq_ref, tabi_ref, g_ref, bn_ref,
         q_ref, k_ref, vt_ref, qi_ref, ki_ref, wt_ref) = refs
    else:
        h_ref, wq_ref, wk_ref, wvt_ref, bq_ref, bk_ref, bvt_ref, tabq_ref, q_ref, k_ref, vt_ref = refs
    x = h_ref[...].astype(BF16)
    q = _dot(x, wq_ref[...]) + bq_ref[...]
    for j in range(Q_COLS // LANES):
        sl = slice(j * LANES, (j + 1) * LANES)
        q_ref[:, sl] = _rope128(q[:, sl], tabq_ref, ROT_DIM // 2).astype(BF16)
    k = _dot(x, wk_ref[...]) + bk_ref[...]
    k_ref[...] = _rope128(k, tabq_ref, ROT_DIM // 2).astype(BF16)
    vt = _dot_nt(wvt_ref[...], x) + bvt_ref[...]
    for j in range(vt_ref.shape[0]):
        vt_ref[j] = vt[:, j * BLOCK:(j + 1) * BLOCK].astype(BF16)
    if not dsa:
        return
    qi = _dot(x, wqi_ref[...]) + bqi_ref[...]
    for j in range(IDX_HEADS * IDX_DIM // LANES):
        sl = slice(j * LANES, (j + 1) * LANES)
        qi_ref[:, sl] = _rope128(qi[:, sl], tabi_ref, IDX_ROT_DIM // 2).astype(BF16)
    ki = _dot(x, wki_ref[...]) + bki_ref[...]
    real = lax.broadcasted_iota(jnp.int32, ki.shape, 1) < IDX_DIM
    mu = jnp.sum(jnp.where(real, ki, 0.0), axis=-1, keepdims=True) * (1.0 / IDX_DIM)
    d = jnp.where(real, ki - mu, 0.0)
    var = jnp.sum(d * d, axis=-1, keepdims=True) * (1.0 / IDX_DIM)
    kn = d * lax.rsqrt(var + LN_EPS) * g_ref[...] + bn_ref[...]
    ki_ref[...] = _rope128(kn, tabi_ref, IDX_ROT_DIM // 2)[:, :IDX_DIM].astype(BF16)
    wt_ref[...] = (_dot_nt(wwt_ref[...], x) + bwt_ref[...]) * IDX_W_SCALE


def _rope_tables(pos, rot_dim):
    half = rot_dim // 2
    inv_freq = ROPE_THETA ** (-jnp.arange(half, dtype=F32) / half)
    ang = pos.astype(F32)[:, None] * inv_freq[None, :]
    cos, sin = jnp.cos(ang), jnp.sin(ang)
    n = pos.shape[0]
    rest = HEAD_DIM - rot_dim
    c = jnp.concatenate([cos, cos, jnp.ones((n, rest), F32)], 1)
    s_up = jnp.concatenate([jnp.zeros((n, half), F32), sin, jnp.zeros((n, rest), F32)], 1)
    s_lo = jnp.concatenate([-sin, jnp.zeros((n, HEAD_DIM - half), F32)], 1)
    return jnp.stack([jnp.tile(t, (1, LANES // HEAD_DIM)) for t in (c, s_up, s_lo)])


def _inproj(h, wp, tabq, tabi, *, dsa, seq):
    m, d = h.shape
    tm = min(ROW_TILE, m)
    nsb = seq // tm
    row = lambda i: (i, 0)
    const = lambda i: (0, 0)
    tab_map = lambda i: (0, i % nsb, 0)
    full = lambda a: pl.BlockSpec(a.shape, const)
    names = ("wq", "wk", "wvt", "wqi", "wki", "wwt", "bq", "bk", "bvt", "bqi", "bki", "bwt") if dsa \
        else ("wq", "wk", "wvt", "bq", "bk", "bvt")
    ins = [h] + [wp[k] for k in names] + [tabq]
    in_specs = [pl.BlockSpec((tm, d), row)] + [full(wp[k]) for k in names] \
        + [pl.BlockSpec((3, tm, LANES), tab_map)]
    out_shape = [jax.ShapeDtypeStruct((m, Q_COLS), BF16), jax.ShapeDtypeStruct((m, KV_COLS), BF16),
                 jax.ShapeDtypeStruct((m // BLOCK, KV_COLS, BLOCK), BF16)]
    out_specs = [pl.BlockSpec((tm, Q_COLS), row), pl.BlockSpec((tm, KV_COLS), row),
                 pl.BlockSpec((tm // BLOCK, KV_COLS, BLOCK), lambda i: (i, 0, 0))]
    if dsa:
        ins += [tabi, wp["g"], wp["bn"]]
        in_specs += [pl.BlockSpec((3, tm, LANES), tab_map), full(wp["g"]), full(wp["bn"])]
        out_shape += [jax.ShapeDtypeStruct((m, IDX_HEADS * IDX_DIM), BF16),
                      jax.ShapeDtypeStruct((m, IDX_DIM), BF16),
                      jax.ShapeDtypeStruct((IDX_HEADS, m), F32)]
        out_specs += [pl.BlockSpec((tm, IDX_HEADS * IDX_DIM), row),
                      pl.BlockSpec((tm, IDX_DIM), row),
                      pl.BlockSpec((IDX_HEADS, tm), lambda i: (0, i))]
    return pl.pallas_call(
        functools.partial(_inproj_kernel, dsa=dsa), grid=(m // tm,), in_specs=in_specs,
        out_specs=out_specs, out_shape=out_shape, compiler_params=_cparams("parallel"),
        name="inproj_dsa" if dsa else "inproj_swa")(*ins)


def _stack_heads(ref, first, count, width=HEAD_DIM, scale=None):
    parts = [ref[:, (first + g) * width:(first + g + 1) * width] for g in range(count)]
    x = jnp.concatenate(parts, axis=0)
    return x if scale is None else x * scale


def _store_tile_t(o_ref, first_head, acc):
    out_t = acc[:HEAD_DIM] / acc[HEAD_DIM:HEAD_DIM + 1]
    for g in range(HEADS_PER_TILE):
        h = first_head + g
        o_ref[h * HEAD_DIM:(h + 1) * HEAD_DIM, :] = out_t[:, g * BLOCK:(g + 1) * BLOCK].astype(BF16)


def _values_with_ones(vt, kh):
    hs = slice(kh * HEAD_DIM, (kh + 1) * HEAD_DIM)
    return jnp.concatenate([vt[hs, :], jnp.ones((V_ROWS - HEAD_DIM, vt.shape[1]), BF16)], axis=0)


def _mask_bias(sel):
    bias = jnp.where(sel, 0.0, NEG).astype(BF16)
    return jnp.concatenate([bias] * HEADS_PER_TILE, axis=1)


def _local_attn_kernel(sink_ref, q_ref, kc_ref, kp_ref, km_ref, vtc_ref, vtp_ref, vtm_ref, o_ref,
                       *, banded):
    n = pl.program_id(1)
    r = lax.broadcasted_iota(jnp.int32, (BLOCK, BLOCK), 0)
    c = lax.broadcasted_iota(jnp.int32, (BLOCK, BLOCK), 1)
    blocks = [(kc_ref, vtc_ref, _mask_bias(r <= c))]
    if banded:
        blocks.append((kp_ref, vtp_ref, _mask_bias((r > c) & (n > 0))))
        blocks.append((km_ref, vtm_ref, _mask_bias(r < N_META)))
    sink_row = lax.broadcasted_iota(jnp.int32, (V_ROWS, COL_TILE), 0) == HEAD_DIM
    tiles = [(kh, kh * GROUP + ct * HEADS_PER_TILE)
             for kh in range(N_KV_HEADS) for ct in range(N_COL_TILES)]
    ks = [[k_ref[:, kh * HEAD_DIM:(kh + 1) * HEAD_DIM] for k_ref, _, _ in blocks]
          for kh in range(N_KV_HEADS)]
    vs = [[_values_with_ones(vt_ref[...], kh) for _, vt_ref, _ in blocks] for kh in range(N_KV_HEADS)]

    def logits(i):
        kh, first = tiles[i]
        qs = _stack_heads(q_ref, first, HEADS_PER_TILE, scale=ATTN_SCALE)
        return [_dot_nt(k, qs) for k in ks[kh]]

    sss = [logits(i) for i in range(PIPE)]
    for i, (kh, first) in enumerate(tiles):
        sink = jnp.concatenate(
            [jnp.full((1, BLOCK), sink_ref[first + g], F32) for g in range(HEADS_PER_TILE)], axis=1)
        ss = [s.astype(BF16) + bias for s, (_, _, bias) in zip(sss[i], blocks)]
        m = sink
        for s in ss:
            m = jnp.maximum(m, jnp.max(s, axis=0, keepdims=True).astype(F32))
        mb = m.astype(BF16)
        ps = [jnp.exp(s - mb) for s in ss]
        if i + PIPE < len(tiles):
            sss.append(logits(i + PIPE))
        acc = jnp.where(sink_row, jnp.exp(sink - mb.astype(F32)), 0.0)
        for p, v in zip(ps, vs[kh]):
            acc = acc + _dot(v, p)
        _store_tile_t(o_ref, first, acc)


def _local_attn(sinks, q, k, vt3, km, vtm, *, banded, batch):
    m = q.shape[0]
    nb = m // BLOCK // batch
    cur = lambda b, n, s: (b * nb + n, 0)
    prev = lambda b, n, s: (b * nb + jnp.maximum(n - 1, 0), 0)
    cur3 = lambda b, n, s: (b * nb + n, 0, 0)
    prev3 = lambda b, n, s: (b * nb + jnp.maximum(n - 1, 0), 0, 0)
    grid_spec = pltpu.PrefetchScalarGridSpec(
        num_scalar_prefetch=1, grid=(batch, nb),
        in_specs=[pl.BlockSpec((BLOCK, Q_COLS), cur), pl.BlockSpec((BLOCK, KV_COLS), cur),
                  pl.BlockSpec((BLOCK, KV_COLS), prev),
                  pl.BlockSpec((BLOCK, KV_COLS), lambda b, n, s: (0, 0)),
                  pl.BlockSpec((None, KV_COLS, BLOCK), cur3), pl.BlockSpec((None, KV_COLS, BLOCK), prev3),
                  pl.BlockSpec((None, KV_COLS, BLOCK), lambda b, n, s: (0, 0, 0))],
        out_specs=pl.BlockSpec((None, Q_COLS, BLOCK), cur3))
    return pl.pallas_call(
        functools.partial(_local_attn_kernel, banded=banded), grid_spec=grid_spec,
        out_shape=jax.ShapeDtypeStruct((m // BLOCK, Q_COLS, BLOCK), BF16),
        compiler_params=_cparams("parallel", "parallel"),
        name="swa_attn" if banded else "meta_attn")(sinks, q, k, k, km, vt3, vt3, vtm)


def _sortable(x):
    bits = lax.bitcast_convert_type(x, jnp.int32)
    return jnp.where(bits < 0, bits ^ jnp.int32(0x7FFFFFFF), bits)


def _dsa_kernel(q_ref, qi_ref, wt_ref, k3_ref, vt3_ref, ki3_ref, km_ref, vtm_ref, o_ref,
                key_s, acc_s, *, ksel, idx_bits):
    n = pl.program_id(1)
    cb = IDX_CHUNK_BLOCKS
    ck = cb * BLOCK
    n_chunks = n // cb + 1
    lane = lax.broadcasted_iota(jnp.int32, (1, BLOCK), 1)
    t_row = n * BLOCK + lane

    qis = _stack_heads(qi_ref, 0, IDX_HEADS, IDX_DIM)
    wts = [wt_ref[h:h + 1, :] for h in range(IDX_HEADS)]

    def score_chunk(c, carry):
        kic = ki3_ref[pl.ds(c * cb, cb)].reshape(ck, IDX_DIM)
        d = _dot_nt(kic, qis)
        sc = jnp.zeros((ck, BLOCK), F32)
        for h in range(IDX_HEADS):
            sc = sc + jnp.maximum(d[:, h * BLOCK:(h + 1) * BLOCK], 0.0) * wts[h]
        sc = sc * IDX_SCALE
        s_idx = c * ck + lax.broadcasted_iota(jnp.int32, (ck, BLOCK), 0)
        key = jnp.where(s_idx <= t_row, _sortable(sc), INT_MIN)
        key_s[pl.ds(c * cb, cb)] = key.reshape(cb, BLOCK, BLOCK)
        return carry

    lax.fori_loop(0, n_chunks, score_chunk, 0)

    def count(pred):
        def body(c, acc):
            keys = key_s[pl.ds(c * cb, cb)].reshape(ck, BLOCK)
            s_idx = c * ck + lax.broadcasted_iota(jnp.int32, (ck, BLOCK), 0)
            hit = jnp.where(pred(keys, s_idx), 1, 0).astype(jnp.int32)
            return acc + jnp.sum(hit.reshape(ck // 8, 8, BLOCK), axis=0)
        acc = lax.fori_loop(0, n_chunks, body, jnp.zeros((8, BLOCK), jnp.int32))
        return jnp.sum(acc, axis=0, keepdims=True)

    def thr_bit(i, tu):
        cand_u = tu | lax.shift_left(jnp.int32(1), 31 - i)
        cand = cand_u ^ jnp.int32(INT_MIN)
        return jnp.where(count(lambda k, s: k >= cand) >= ksel, cand_u, tu)

    thr_u = lax.fori_loop(0, 32, thr_bit, jnp.zeros((1, BLOCK), jnp.int32))
    thr = thr_u ^ jnp.int32(INT_MIN)
    n_gt = count(lambda k, s: k > thr)
    n_ge = count(lambda k, s: k >= thr)
    need = ksel - n_gt

    def tie_bit(i, x):
        cand = x | lax.shift_left(jnp.int32(1), idx_bits - 1 - i)
        return jnp.where(count(lambda k, s: (k == thr) & (s < cand)) <= need - 1, cand, x)

    def tie_search():
        return lax.fori_loop(0, idx_bits, tie_bit, jnp.zeros((1, BLOCK), jnp.int32))

    excess = jnp.max(jnp.where((n_ge > ksel) & (thr_u != 0), 1, 0))
    tie_x = lax.cond(excess > 0, tie_search,
                     lambda: jnp.full((1, BLOCK), (1 << idx_bits) - 1, jnp.int32))

    tiles = [(kh, kh * GROUP + ct * HEADS_PER_TILE)
             for kh in range(N_KV_HEADS) for ct in range(N_COL_TILES)]
    qs = [_stack_heads(q_ref, first, HEADS_PER_TILE, scale=ATTN_SCALE) for _, first in tiles]
    krow = lax.broadcasted_iota(jnp.int32, (BLOCK, BLOCK), 0)
    bias_m = _mask_bias(krow < N_META)
    sm = [_dot_nt(km_ref[:, kh * HEAD_DIM:(kh + 1) * HEAD_DIM], qs[i]) for i, (kh, _) in enumerate(tiles)]
    vm = [_values_with_ones(vtm_ref[...], kh) for kh in range(N_KV_HEADS)]
    m0 = []
    for i, (kh, _) in enumerate(tiles):
        s = sm[i].astype(BF16) + bias_m
        m = jnp.max(s, axis=0, keepdims=True)
        m0.append(m)
        acc_s[i] = _dot(vm[kh], jnp.exp(s - m))

    ab = ATT_CHUNK_BLOCKS
    ak = ab * BLOCK
    arow = lax.broadcasted_iota(jnp.int32, (ak, BLOCK), 0)

    def attend(c, ms):
        keys = key_s[pl.ds(c * ab, ab)].reshape(ak, BLOCK)
        s_idx = c * ak + arow
        sel = ((keys > thr) | ((keys == thr) & (s_idx <= tie_x))) & (s_idx <= t_row)
        bias = _mask_bias(sel)
        kc = k3_ref[pl.ds(c * ab, ab)].reshape(ak, KV_COLS)
        vtc = jnp.concatenate([vt3_ref[c * ab + g] for g in range(ab)], axis=1)
        vs = [_values_with_ones(vtc, kh) for kh in range(N_KV_HEADS)]
        ks = [kc[:, kh * HEAD_DIM:(kh + 1) * HEAD_DIM] for kh in range(N_KV_HEADS)]
        logits = lambda i: _dot_nt(ks[tiles[i][0]], qs[i])
        ss = [logits(i) for i in range(PIPE)]
        new_m = []
        for i, (kh, _) in enumerate(tiles):
            s = ss[i].astype(BF16) + bias
            m = jnp.maximum(ms[i], jnp.max(s, axis=0, keepdims=True))
            alpha = jnp.exp(ms[i].astype(F32) - m.astype(F32))
            p = jnp.exp(s - m)
            new_m.append(m)
            if i + PIPE < len(tiles):
                ss.append(logits(i + PIPE))
            acc_s[i] = acc_s[i] * alpha + _dot(vs[kh], p)
        return tuple(new_m)

    lax.fori_loop(0, n // ab + 1, attend, tuple(m0))
    for i, (_, first) in enumerate(tiles):
        _store_tile_t(o_ref, first, acc_s[i])


def _dsa_attn(q, qi, wt, k3, vt3, ki3, km, vtm, *, batch):
    m = q.shape[0]
    nb = m // BLOCK // batch
    seq = nb * BLOCK
    assert nb % IDX_CHUNK_BLOCKS == 0 and IDX_CHUNK_BLOCKS % ATT_CHUNK_BLOCKS == 0
    ksel = min(TOPK_MAX, seq // 4)
    idx_bits = max(1, int(np.ceil(np.log2(seq))))
    qmap = lambda b, n: (b * nb + n, 0)
    bmap = lambda b, n: (b, 0, 0)
    return pl.pallas_call(
        functools.partial(_dsa_kernel, ksel=ksel, idx_bits=idx_bits), grid=(batch, nb),
        in_specs=[pl.BlockSpec((BLOCK, Q_COLS), qmap),
                  pl.BlockSpec((BLOCK, IDX_HEADS * IDX_DIM), qmap),
                  pl.BlockSpec((IDX_HEADS, BLOCK), lambda b, n: (0, b * nb + n)),
                  pl.BlockSpec((nb, BLOCK, KV_COLS), bmap),
                  pl.BlockSpec((nb, KV_COLS, BLOCK), bmap),
                  pl.BlockSpec((nb, BLOCK, IDX_DIM), bmap),
                  pl.BlockSpec((BLOCK, KV_COLS), lambda b, n: (0, 0)),
                  pl.BlockSpec((None, KV_COLS, BLOCK), lambda b, n: (0, 0, 0))],
        out_specs=pl.BlockSpec((None, Q_COLS, BLOCK), lambda b, n: (b * nb + n, 0, 0)),
        out_shape=jax.ShapeDtypeStruct((m // BLOCK, Q_COLS, BLOCK), BF16),
        scratch_shapes=[pltpu.VMEM((nb, BLOCK, BLOCK), jnp.int32),
                        pltpu.VMEM((N_KV_HEADS * N_COL_TILES, V_ROWS, COL_TILE), F32)],
        compiler_params=_cparams("parallel", "arbitrary"), name="dsa_attn",
    )(q, qi, wt, k3, vt3, ki3, km, vtm)


def _outproj_kernel(a_ref, w_ref, h_ref, g_ref, b_ref, wr_ref, br_ref, tri_ref,
                    h1_ref, h1t_ref, route_ref, cnt_ref, carry_s, *, alpha):
    i = pl.program_id(0)

    @pl.when(i == 0)
    def _():
        carry_s[...] = jnp.zeros_like(carry_s)

    nblk = a_ref.shape[0]
    gb = max(1, nblk // OUTPROJ_GROUPS)
    gr = gb * BLOCK
    n_grp = nblk // gb
    ys = [jnp.concatenate([_dot_tn(a_ref[g * gb + j], w_ref[...]) for j in range(gb)], axis=0)
          for g in range(n_grp)]
    lane = lax.broadcasted_iota(jnp.int32, (gr, LANES), 1)
    picks = []
    for g in range(n_grp):
        rs = pl.ds(g * gr, gr)
        h1 = _layer_norm(alpha * h_ref[rs, :] + ys[g], g_ref[...], b_ref[...])
        h1_ref[rs, :] = h1
        _store_token_tiles(h1t_ref.at[pl.ds(g * gr * TOKEN_ROWS, gr * TOKEN_ROWS)], h1)
        rem = _dot(h1.astype(BF16), wr_ref[...]) + br_ref[...]
        idxs, vals = [], []
        for _ in range(TOP_K):
            mx = jnp.max(rem, axis=-1, keepdims=True)
            ix = jnp.min(jnp.where(rem == mx, lane, LANES), axis=-1, keepdims=True)
            idxs.append(ix)
            vals.append(mx)
            rem = jnp.where(lane == ix, -jnp.inf, rem)
        exps = [jnp.exp(v - vals[0]) for v in vals]
        den = exps[0] + exps[1] + exps[2] + exps[3]
        picks.append((idxs, [e / den for e in exps], jnp.where(rem == -jnp.inf, 1.0, 0.0)))
    sel = jnp.concatenate([p[2] for p in picks], axis=0)
    before = _dot(tri_ref[...], sel.astype(BF16)) + carry_s[...]
    for g, (idxs, gates, _) in enumerate(picks):
        bg = before[g * gr:(g + 1) * gr]
        route = jnp.zeros((gr, LANES), F32)
        for k in range(TOP_K):
            rank = jnp.sum(jnp.where(lane == idxs[k], bg, 0.0), axis=-1, keepdims=True)
            route = jnp.where(lane == k, idxs[k].astype(F32), route)
            route = jnp.where(lane == TOP_K + k, gates[k], route)
            route = jnp.where(lane == 2 * TOP_K + k, rank, route)
        route_ref[pl.ds(g * gr, gr), :] = route
    carry_s[...] = carry_s[...] + jnp.sum(sel, axis=0, keepdims=True)
    cnt_ref[...] = carry_s[...]


def _outproj(a_t, w, h, g, b, wr, br, *, alpha):
    m, d = h.shape
    tm = min(OUTPROJ_ROWS, m)
    tri = jnp.tril(jnp.ones((tm, tm), BF16), -1)
    row = lambda i: (i, 0)
    const = lambda i: (0, 0)
    full = lambda x: pl.BlockSpec(x.shape, const)
    return pl.pallas_call(
        functools.partial(_outproj_kernel, alpha=alpha), grid=(m // tm,),
        in_specs=[pl.BlockSpec((tm // BLOCK, Q_COLS, BLOCK), lambda i: (i, 0, 0)), full(w),
                  pl.BlockSpec((tm, d), row), full(g), full(b), full(wr), full(br), full(tri)],
        out_specs=[pl.BlockSpec((tm, d), row), pl.BlockSpec((tm * TOKEN_ROWS, LANES), row),
                   pl.BlockSpec((tm, LANES), row), pl.BlockSpec((1, LANES), const)],
        out_shape=[jax.ShapeDtypeStruct((m, d), F32), jax.ShapeDtypeStruct((m * TOKEN_ROWS, LANES), F32),
                   jax.ShapeDtypeStruct((m, LANES), F32), jax.ShapeDtypeStruct((1, LANES), F32)],
        scratch_shapes=[pltpu.VMEM((1, LANES), F32)],
        compiler_params=_cparams("arbitrary"), name="outproj_router")(a_t, w, h, g, b, wr, br, tri)


def _expert_ffn(x, wgu_ref, bgu_ref, wd_ref, bd_ref):
    ff = wd_ref.shape[0]
    acc = jnp.zeros((x.shape[0], wd_ref.shape[1]), F32)
    for c in range(ff // FF_CHUNK):
        gs = slice(c * FF_CHUNK, (c + 1) * FF_CHUNK)
        us = slice(ff + c * FF_CHUNK, ff + (c + 1) * FF_CHUNK)
        gate = jnp.minimum(_dot(x, wgu_ref[:, gs]) + bgu_ref[:, gs], SWIGLU_LIMIT)
        up = jnp.clip(_dot(x, wgu_ref[:, us]) + bgu_ref[:, us], -SWIGLU_LIMIT, SWIGLU_LIMIT)
        act = gate * jax.nn.sigmoid(SWIGLU_ALPHA * gate) * (up + 1.0)
        acc = acc + _dot(act.astype(BF16), wd_ref[gs, :])
    return acc + bd_ref[...]


def _experts_kernel(be_ref, nu_ref, tok_ref, tok_next_ref, dst_ref, h_ref, wgu_ref, bgu_ref, wd_ref,
                    bd_ref, yk_ref, xbuf, ybuf, gsem, ssem):
    del be_ref
    i = pl.program_id(0)
    n_used = nu_ref[0]
    slot = i % 2
    rows = xbuf.shape[1] // TOKEN_ROWS
    tile = lambda t: pl.ds(pl.multiple_of(t * TOKEN_ROWS, TOKEN_ROWS), TOKEN_ROWS)

    def gather(src_ref, s):
        def body(r8, c):
            for u in range(DMA_UNROLL):
                r = r8 * DMA_UNROLL + u
                pltpu.make_async_copy(h_ref.at[tile(src_ref[0, 0, r])],
                                      xbuf.at[s, tile(r)], gsem.at[s]).start(priority=u % 2)
            return c
        lax.fori_loop(0, rows // DMA_UNROLL, body, 0)

    def scatter(s):
        def body(r8, c):
            for u in range(DMA_UNROLL):
                r = r8 * DMA_UNROLL + u
                pltpu.make_async_copy(ybuf.at[s, tile(r)],
                                      yk_ref.at[tile(dst_ref[0, 0, r])], ssem.at[s]).start(priority=u % 2)
            return c
        lax.fori_loop(0, rows // DMA_UNROLL, body, 0)

    def wait_gather(s):
        pltpu.make_async_copy(xbuf.at[s], xbuf.at[s], gsem.at[s]).wait()

    def wait_scatter(s):
        pltpu.make_async_copy(ybuf.at[s], ybuf.at[s], ssem.at[s]).wait()

    @pl.when(i == 0)
    def _():
        for s in range(2):
            ybuf[s] = jnp.zeros(ybuf.shape[1:], F32)
        fills = [pltpu.make_async_copy(
            ybuf.at[s], yk_ref.at[pl.ds(yk_ref.shape[0] - (2 - s) * ybuf.shape[1], ybuf.shape[1])],
            ssem.at[s]) for s in range(2)]
        for f in fills:
            f.start()
        for f in fills:
            f.wait()

    @pl.when((i == 0) & (n_used > 0))
    def _():
        gather(tok_ref, 0)

    @pl.when(i < n_used)
    def _():
        wait_gather(slot)

        @pl.when(i + 1 < n_used)
        def _():
            gather(tok_next_ref, 1 - slot)

        @pl.when(i >= 2)
        def _():
            wait_scatter(slot)

        x = _load_token_tiles(xbuf.at[slot], rows).astype(BF16)
        _store_token_tiles(ybuf.at[slot], _expert_ffn(x, wgu_ref, bgu_ref, wd_ref, bd_ref))
        scatter(slot)

    @pl.when(i == pl.num_programs(0) - 1)
    def _():
        @pl.when(n_used >= 1)
        def _():
            wait_scatter((n_used - 1) % 2)

        @pl.when(n_used >= 2)
        def _():
            wait_scatter(n_used % 2)


def _experts(blk_e, n_used, tok3, dst3, ht, wgu, bgu, wd, bd, n_out_rows):
    n_blk, _, rows = tok3.shape
    ff, d = wd.shape[1:]
    assert d == TOKEN_ROWS * LANES
    emap = lambda i, be, nu: (be[i], 0, 0)
    smem = lambda imap: pl.BlockSpec((1, 1, rows), imap, memory_space=pltpu.SMEM)
    grid_spec = pltpu.PrefetchScalarGridSpec(
        num_scalar_prefetch=2, grid=(n_blk,),
        in_specs=[smem(lambda i, be, nu: (i, 0, 0)),
                  smem(lambda i, be, nu: (jnp.minimum(i + 1, n_blk - 1), 0, 0)),
                  smem(lambda i, be, nu: (i, 0, 0)),
                  pl.BlockSpec(memory_space=pl.ANY),
                  pl.BlockSpec((None, d, 2 * ff), emap), pl.BlockSpec((None, 1, 2 * ff), emap),
                  pl.BlockSpec((None, ff, d), emap), pl.BlockSpec((None, 1, d), emap)],
        out_specs=pl.BlockSpec(memory_space=pl.ANY),
        scratch_shapes=[pltpu.VMEM((2, rows * TOKEN_ROWS, LANES), F32),
                        pltpu.VMEM((2, rows * TOKEN_ROWS, LANES), F32),
                        pltpu.SemaphoreType.DMA((2,)), pltpu.SemaphoreType.DMA((2,))])
    return pl.pallas_call(
        _experts_kernel, grid_spec=grid_spec,
        out_shape=jax.ShapeDtypeStruct((n_out_rows * TOKEN_ROWS, LANES), F32),
        compiler_params=_cparams("arbitrary"), name="moe_experts",
    )(blk_e, n_used, tok3, tok3, dst3, ht, wgu, bgu, wd, bd)


def _combine_kernel(route_ref, h_ref, g_ref, b_ref, y0_ref, y1_ref, y2_ref, y3_ref, o_ref, *, alpha):
    z = alpha * h_ref[...]
    for k, y_ref in enumerate((y0_ref, y1_ref, y2_ref, y3_ref)):
        z = z + route_ref[:, TOP_K + k:TOP_K + k + 1] * _load_token_tiles(y_ref, h_ref.shape[0])
    o_ref[...] = _layer_norm(z, g_ref[...], b_ref[...])


def _combine(route, h, g, b, yk, *, alpha):
    m, d = h.shape
    tm = min(ROW_TILE, m)
    nt = m // tm
    row = lambda i: (i, 0)
    const = lambda i: (0, 0)
    choice = lambda k: pl.BlockSpec((tm * TOKEN_ROWS, LANES), lambda i: (k * nt + i, 0))
    return pl.pallas_call(
        functools.partial(_combine_kernel, alpha=alpha), grid=(nt,),
        in_specs=[pl.BlockSpec((tm, LANES), row), pl.BlockSpec((tm, d), row),
                  pl.BlockSpec(g.shape, const), pl.BlockSpec(b.shape, const)]
                 + [choice(k) for k in range(TOP_K)],
        out_specs=pl.BlockSpec((tm, d), row),
        out_shape=jax.ShapeDtypeStruct((m, d), F32),
        compiler_params=_cparams("parallel"), name="moe_combine")(route, h, g, b, yk, yk, yk, yk)


def _meta_moe_kernel(hm_ref, gm_ref, wgu_ref, bgu_ref, wd_ref, bd_ref, g_ref, b_ref, o_ref, acc_s,
                     *, alpha):
    e = pl.program_id(0)

    @pl.when(e == 0)
    def _():
        acc_s[...] = jnp.zeros_like(acc_s)

    y = _expert_ffn(hm_ref[...].astype(BF16), wgu_ref, bgu_ref, wd_ref, bd_ref)
    lane = lax.broadcasted_iota(jnp.int32, gm_ref.shape, 1)
    ge = jnp.sum(jnp.where(lane == e, gm_ref[...], 0.0), axis=-1, keepdims=True)
    acc_s[...] = acc_s[...] + ge * y

    @pl.when(e == pl.num_programs(0) - 1)
    def _():
        o_ref[...] = _layer_norm(alpha * hm_ref[...] + acc_s[...], g_ref[...], b_ref[...])


def _meta_moe(hm, gm, wgu, bgu, wd, bd, g, b, *, alpha):
    m, d = hm.shape
    ff = wd.shape[1]
    const = lambda e: (0, 0)
    emap = lambda e: (e, 0, 0)
    return pl.pallas_call(
        functools.partial(_meta_moe_kernel, alpha=alpha), grid=(wgu.shape[0],),
        in_specs=[pl.BlockSpec((m, d), const), pl.BlockSpec((m, LANES), const),
                  pl.BlockSpec((None, d, 2 * ff), emap), pl.BlockSpec((None, 1, 2 * ff), emap),
                  pl.BlockSpec((None, ff, d), emap), pl.BlockSpec((None, 1, d), emap),
                  pl.BlockSpec(g.shape, const), pl.BlockSpec(b.shape, const)],
        out_specs=pl.BlockSpec((m, d), const),
        out_shape=jax.ShapeDtypeStruct((m, d), F32),
        scratch_shapes=[pltpu.VMEM((m, d), F32)],
        compiler_params=_cparams("arbitrary"), name="meta_experts")(hm, gm, wgu, bgu, wd, bd, g, b)


def _moe(h1, h1t, route, cnt, hm1, route_m, ep, g, b, *, alpha):
    m, d = h1.shape
    n_blk = -(-m * TOP_K // MOE_ROWS) + N_EXPERTS
    e = route[:, :TOP_K].astype(jnp.int32)
    rank = route[:, 2 * TOP_K:3 * TOP_K].astype(jnp.int32)
    counts = cnt[0, :N_EXPERTS].astype(jnp.int32)
    padded = (counts + MOE_ROWS - 1) // MOE_ROWS * MOE_ROWS
    pend = jnp.cumsum(padded)
    dest = (pend - padded)[e] + rank
    blk_start = jnp.arange(n_blk, dtype=jnp.int32) * MOE_ROWS
    blk_e = jnp.minimum(jnp.sum((pend[None, :] <= blk_start[:, None]).astype(jnp.int32), axis=1),
                        N_EXPERTS - 1)
    n_used = (pend[-1:] // MOE_ROWS).astype(jnp.int32)
    n_sorted = n_blk * MOE_ROWS
    inv = jnp.zeros((n_sorted,), jnp.int32).at[dest.reshape(-1)].set(
        jnp.arange(1, m * TOP_K + 1, dtype=jnp.int32))
    pair = jnp.maximum(inv - 1, 0)
    pos = jnp.arange(n_sorted, dtype=jnp.int32)
    pad_row = TOP_K * m + (pos // MOE_ROWS % 2) * MOE_ROWS + pos % MOE_ROWS
    tok3 = (pair // TOP_K).reshape(n_blk, 1, MOE_ROWS)
    dst3 = jnp.where(inv > 0, (pair % TOP_K) * m + pair // TOP_K, pad_row).reshape(n_blk, 1, MOE_ROWS)
    yk = _experts(blk_e, n_used, tok3, dst3, h1t, ep["wgu"], ep["bgu"], ep["wd"], ep["bd"],
                  TOP_K * m + 2 * MOE_ROWS)
    h2 = _combine(route, h1, g, b, yk, alpha=alpha)
    em = route_m[:, :TOP_K].astype(jnp.int32)
    gm = jnp.sum(jax.nn.one_hot(em, LANES, dtype=F32) * route_m[:, TOP_K:2 * TOP_K, None], axis=1)
    hm2 = _meta_moe(hm1, gm, ep["wgu"], ep["bgu"], ep["wd"], ep["bd"], g, b, alpha=alpha)
    return h2, hm2


def _split_in_weights(w, b, dsa, g=None, bn=None):
    cuts = np.cumsum([Q_COLS, KV_COLS, KV_COLS, IDX_HEADS * IDX_DIM, IDX_DIM]).tolist()
    wb = w.astype(BF16)
    row = lambda v: v.reshape(1, -1)
    col = lambda v: v.reshape(-1, 1)
    wp = {"wq": wb[:, :cuts[0]], "bq": row(b[:cuts[0]]),
          "wk": wb[:, cuts[0]:cuts[1]], "bk": row(b[cuts[0]:cuts[1]]),
          "wvt": wb[:, cuts[1]:cuts[2]].T, "bvt": col(b[cuts[1]:cuts[2]])}
    if dsa:
        pad = LANES - IDX_DIM
        wp.update({
            "wqi": wb[:, cuts[2]:cuts[3]], "bqi": row(b[cuts[2]:cuts[3]]),
            "wki": jnp.pad(wb[:, cuts[3]:cuts[4]], ((0, 0), (0, pad))),
            "bki": row(jnp.pad(b[cuts[3]:cuts[4]], (0, pad))),
            "wwt": wb[:, cuts[4]:].T, "bwt": col(b[cuts[4]:]),
            "g": row(jnp.pad(g, (0, pad))), "bn": row(jnp.pad(bn, (0, pad)))})
    return wp


def kernel(x, meta_tokens, w_in_a, b_in_a, sinks_a, w_out_a, w_in_b, b_in_b, idx_k_norm_g, idx_k_norm_b, w_out_b, ln_mix_g, ln_mix_b, w_router, b_router, w_gate_up, b_gate_up, w_down, b_down, ln_ffn_g, ln_ffn_b):
    batch, seq, d = x.shape
    depth = ln_mix_g.shape[0]
    alpha = float((2 * depth) ** 0.25)
    row = lambda v: v.reshape(1, -1)
    nb = seq // BLOCK

    h = x.reshape(batch * seq, d)
    hm = jnp.zeros((BLOCK, d), F32).at[:N_META].set(meta_tokens.astype(F32))
    tab_q = _rope_tables(jnp.arange(N_META, N_META + seq), ROT_DIM)
    tab_i = _rope_tables(jnp.arange(N_META, N_META + seq), IDX_ROT_DIM)
    tab_qm = _rope_tables(jnp.arange(BLOCK), ROT_DIM)
    tab_im = _rope_tables(jnp.arange(BLOCK), IDX_ROT_DIM)
    no_sinks = jnp.full((N_HEADS,), NEG, F32)

    for i in range(depth):
        j = i // 2
        dsa = i % 2 == 1
        if dsa:
            wp = _split_in_weights(w_in_b[j], b_in_b[j], True, idx_k_norm_g[j], idx_k_norm_b[j])
            w_out, sinks = w_out_b[j], no_sinks
        else:
            wp = _split_in_weights(w_in_a[j], b_in_a[j], False)
            w_out, sinks = w_out_a[j], sinks_a[j].astype(F32)
        pr = _inproj(h, wp, tab_q, tab_i, dsa=dsa, seq=seq)
        pm = _inproj(hm, wp, tab_qm, tab_im, dsa=dsa, seq=BLOCK)
        q, k, vt3 = pr[:3]
        qm, km, vtm = pm[:3]
        if dsa:
            qi, ki, wt = pr[3:]
            att = _dsa_attn(q, qi, wt, k.reshape(batch * nb, BLOCK, KV_COLS), vt3,
                            ki.reshape(batch * nb, BLOCK, IDX_DIM), km, vtm, batch=batch)
        else:
            att = _local_attn(sinks, q, k, vt3, km, vtm, banded=True, batch=batch)
        att_m = _local_attn(sinks, qm, km, vtm, km, vtm, banded=False, batch=1)

        w_out = w_out.astype(BF16)
        wr = jnp.pad(w_router[i].astype(BF16), ((0, 0), (0, LANES - N_EXPERTS)))
        br = row(jnp.pad(b_router[i].astype(F32), (0, LANES - N_EXPERTS), constant_values=NEG))
        g1, b1 = row(ln_mix_g[i]), row(ln_mix_b[i])
        h1, h1t, route, cnt = _outproj(att, w_out, h, g1, b1, wr, br, alpha=alpha)
        hm1, _, route_m, _ = _outproj(att_m, w_out, hm, g1, b1, wr, br, alpha=alpha)

        ep = {"wgu": w_gate_up[i].astype(BF16), "bgu": b_gate_up[i][:, None, :],
              "wd": w_down[i].astype(BF16), "bd": b_down[i][:, None, :]}
        h, hm = _moe(h1, h1t, route, cnt, hm1, route_m, ep, row(ln_ffn_g[i]), row(ln_ffn_b[i]),
                     alpha=alpha)
    return h.reshape(batch, seq, d)
```
